```python
import math
import jax, jax.numpy as jnp
from jax import lax
import numpy as np

D_MODEL = 1024
BATCH = 16
SEQ = 2048
DEPTH = 2

N_MIXERS = 2
N_NSA_LAYERS = (DEPTH + N_MIXERS - 1) // N_MIXERS
N_DIFF_LAYERS = DEPTH // N_MIXERS

REL_BUCKETS = 32
REL_MAX_DIST = 128
N_BIAS_MAPS = 16

NSA_HEADS = 16
NSA_GROUPS = 4
NSA_HPG = NSA_HEADS // NSA_GROUPS
NSA_HEAD_DIM = D_MODEL // NSA_HEADS
NSA_WIDTH = NSA_HEADS * NSA_HEAD_DIM
NSA_KV = NSA_GROUPS * NSA_HEAD_DIM
CMP_BLOCK = 32
CMP_STRIDE = 16
CMP_HIDDEN = 2 * NSA_HEAD_DIM
SEL_BLOCK = 64
SEL_TOPK = 8
WINDOW = 512
NSA_QBLOCK = 64
NSA_SPLIT_SIZES = [NSA_WIDTH] + [NSA_KV] * 6 + [3 * NSA_HEADS, NSA_WIDTH]
NSA_IN = sum(NSA_SPLIT_SIZES)

DIFF_HEADS = 8
DIFF_HALF = D_MODEL // (2 * DIFF_HEADS)
DIFF_VDIM = 2 * DIFF_HALF
DIFF_WIDTH = DIFF_HEADS * DIFF_VDIM
DIFF_IN = 4 * DIFF_WIDTH
DIFF_QBLOCK = 128

NEG = -1e30
BIG = 1e9
EPS = 1e-6

kernel_name = "hybrid_nsa_diffattn_sandwich"


def rmsnorm(x, g):
    xf = x.astype(jnp.float32)
    y = xf * lax.rsqrt(jnp.mean(xf * xf, axis=-1, keepdims=True) + EPS)
    return (y * g.astype(jnp.float32)).astype(x.dtype)


def rel_bucket(dist):
    n = jnp.maximum(dist, 0)
    max_exact = REL_BUCKETS // 2
    nf = jnp.maximum(n, 1).astype(jnp.float32)
    large = max_exact + (jnp.log(nf / max_exact) / math.log(REL_MAX_DIST / max_exact)
                         * (REL_BUCKETS - max_exact)).astype(jnp.int32)
    large = jnp.minimum(large, REL_BUCKETS - 1)
    return jnp.where(n < max_exact, n, large)


def compress(kv, pe, w1, w2):
    b, s, g, dh = kv.shape
    nc = (s - CMP_BLOCK) // CMP_STRIDE + 1
    idx = jnp.arange(nc)[:, None] * CMP_STRIDE + jnp.arange(CMP_BLOCK)[None, :]
    blocks = kv[:, idx] + pe[None, None, :, None, :]
    flat = jnp.moveaxis(blocks, 3, 2).reshape(b, nc, g, CMP_BLOCK * dh)
    return jax.nn.silu(flat @ w1) @ w2


def nsa_mixer(u, table, w_in, pe_k, w1_k, w2_k, pe_v, w1_v, w2_v, w_out):
    b, s, _ = u.shape
    G, J, dh = NSA_GROUPS, NSA_HPG, NSA_HEAD_DIM
    points = np.cumsum(NSA_SPLIT_SIZES)[:-1].tolist()
    q, kc, vc, ks, vs, kw, vw, gate, z = jnp.split(u @ w_in, points, axis=-1)
    q = q.reshape(b, s, G, J, dh) * (dh ** -0.5)
    kc, vc, ks, vs, kw, vw = [t.reshape(b, s, G, dh) for t in (kc, vc, ks, vs, kw, vw)]
    gate = jax.nn.sigmoid(gate.astype(jnp.float32)).reshape(b, s, G, J, 3)

    k_cmp = compress(kc, pe_k, w1_k, w2_k)
    v_cmp = compress(vc, pe_v, w1_v, w2_v)
    nc = k_cmp.shape[1]
    cmp_lo = jnp.arange(nc) * CMP_STRIDE
    cmp_end = cmp_lo + CMP_BLOCK - 1
    n_sel = s // SEL_BLOCK
    topk = min(SEL_TOPK, n_sel)
    sel_lo = jnp.arange(n_sel) * SEL_BLOCK
    overlap = jnp.clip(jnp.minimum(cmp_lo[:, None] + CMP_BLOCK, sel_lo[None, :] + SEL_BLOCK)
                       - jnp.maximum(cmp_lo[:, None], sel_lo[None, :]), 0).astype(jnp.float32) / CMP_BLOCK

    table_g = table.reshape(REL_BUCKETS, G, J)
    table_gt = jnp.transpose(table_g, (1, 0, 2))
    ks_t = jnp.transpose(ks, (0, 2, 1, 3))
    vs_t = jnp.transpose(vs, (0, 2, 1, 3))
    kw_pad = jnp.pad(kw, ((0, 0), (WINDOW, 0), (0, 0), (0, 0)))
    vw_pad = jnp.pad(vw, ((0, 0), (WINDOW, 0), (0, 0), (0, 0)))
    bi = jnp.arange(b)[:, None, None, None]
    gi = jnp.arange(G)[None, :, None, None]
    blk_n = jnp.arange(n_sel)

    def block(qb):
        start = qb * NSA_QBLOCK
        t = start + jnp.arange(NSA_QBLOCK)
        qblk = lax.dynamic_slice_in_dim(q, start, NSA_QBLOCK, axis=1)

        dist_c = t[:, None] - cmp_end[None, :]
        m_c = dist_c >= 0
        bias_c = jnp.transpose(table_g[rel_bucket(dist_c)], (2, 3, 0, 1))
        s_c = jnp.einsum('bqgjd,bcgd->bgjqc', qblk, k_cmp).astype(jnp.float32) + bias_c
        p_c = jax.nn.softmax(jnp.where(m_c, s_c, NEG), axis=-1) * jnp.any(m_c, axis=-1)[:, None]
        o_c = jnp.einsum('bgjqc,bcgd->bqgjd', p_c.astype(v_cmp.dtype), v_cmp)

        imp = jnp.einsum('bgjqc,cn->bgqn', p_c, overlap)
        cur = t // SEL_BLOCK
        forced = (blk_n[None, :] == 0) | (blk_n[None, :] == cur[:, None]) | (blk_n[None, :] == cur[:, None] - 1)
        future = blk_n[None, :] > cur[:, None]
        imp = jnp.where(forced, BIG, jnp.where(future, -BIG, imp))
        _, sel = lax.top_k(imp, topk)
        pos = (sel[..., None] * SEL_BLOCK + jnp.arange(SEL_BLOCK)).reshape(b, G, NSA_QBLOCK, topk * SEL_BLOCK)
        k_g = ks_t[bi, gi, pos]
        v_g = vs_t[bi, gi, pos]
        dist_s = t[None, None, :, None] - pos
        bias_s = jnp.moveaxis(table_gt[gi, rel_bucket(dist_s)], -1, 2)
        s_s = jnp.einsum('bqgjd,bgqkd->bgjqk', qblk, k_g).astype(jnp.float32) + bias_s
        p_s = jax.nn.softmax(jnp.where((dist_s >= 0)[:, :, None], s_s, NEG), axis=-1)
        o_s = jnp.einsum('bgjqk,bgqkd->bqgjd', p_s.astype(v_g.dtype), v_g)

        k_win = lax.dynamic_slice_in_dim(kw_pad, start, NSA_QBLOCK + WINDOW, axis=1)
        v_win = lax.dynamic_slice_in_dim(vw_pad, start, NSA_QBLOCK + WINDOW, axis=1)
        pos_w = start - WINDOW + jnp.arange(NSA_QBLOCK + WINDOW)
        dist_w = t[:, None] - pos_w[None, :]
        m_w = (dist_w >= 0) & (dist_w < WINDOW) & (pos_w[None, :] >= 0)
        bias_w = jnp.transpose(table_g[rel_bucket(dist_w)], (2, 3, 0, 1))
        s_w = jnp.einsum('bqgjd,bkgd->bgjqk', qblk, k_win).astype(jnp.float32) + bias_w
        p_w = jax.nn.softmax(jnp.where(m_w, s_w, NEG), axis=-1)
        o_w = jnp.einsum('bgjqk,bkgd->bqgjd', p_w.astype(v_win.dtype), v_win)
        return o_c, o_s, o_w

    o_c, o_s, o_w = lax.map(block, jnp.arange(s // NSA_QBLOCK))
    o_c, o_s, o_w = [jnp.moveaxis(o, 0, 1).reshape(b, s, G, J, dh) for o in (o_c, o_s, o_w)]
    o = gate[..., 0:1] * o_c + gate[..., 1:2] * o_s + gate[..., 2:3] * o_w
    o = o.astype(u.dtype).reshape(b, s, NSA_WIDTH) * jax.nn.silu(z)
    return o @ w_out


def diff_mixer(u, table, w_in, lq1, lk1, lq2, lk2, subln, w_out, lambda_init):
    b, s, _ = u.shape
    H, d, dv = DIFF_HEADS, DIFF_HALF, DIFF_VDIM
    q, k, v, z = jnp.split(u @ w_in, 4, axis=-1)
    q = q.reshape(b, s, H, 2, d) * (d ** -0.5)
    k = k.reshape(b, s, H, 2, d)
    v = v.reshape(b, s, H, dv)
    lam = (jnp.exp(jnp.sum(lq1.astype(jnp.float32) * lk1.astype(jnp.float32)))
           - jnp.exp(jnp.sum(lq2.astype(jnp.float32) * lk2.astype(jnp.float32))) + lambda_init)
    table_h = table.reshape(REL_BUCKETS, H, 2)
    kpos = jnp.arange(s)

    def block(qb):
        start = qb * DIFF_QBLOCK
        t = start + jnp.arange(DIFF_QBLOCK)
        qblk = lax.dynamic_slice_in_dim(q, start, DIFF_QBLOCK, axis=1)
        dist = t[:, None] - kpos[None, :]
        bias = jnp.transpose(table_h[rel_bucket(dist)], (2, 3, 0, 1))
        sc = jnp.einsum('bqhmd,bkhmd->bhmqk', qblk, k).astype(jnp.float32) + bias
        p = jax.nn.softmax(jnp.where(dist >= 0, sc, NEG), axis=-1)
        a = p[:, :, 0] - lam * p[:, :, 1]
        return jnp.einsum('bhqk,bkhe->bqhe', a.astype(v.dtype), v)

    o = lax.map(block, jnp.arange(s // DIFF_QBLOCK))
    o = jnp.moveaxis(o, 0, 1).reshape(b, s, H, dv)
    o = rmsnorm(o, subln.reshape(H, dv)) * (1.0 - lambda_init)
    o = o.reshape(b, s, DIFF_WIDTH) * jax.nn.silu(z)
    return o @ w_out


def setup_inputs(seed: int = 0) -> dict:
    key = jax.random.key(seed)
    ks = jax.random.split(key, 20)
    nrm = jax.random.normal
    f32 = jnp.float32
    A, Bn = N_NSA_LAYERS, N_DIFF_LAYERS
    ld = CMP_BLOCK * NSA_HEAD_DIM
    return {
        "x": nrm(ks[0], (BATCH, SEQ, D_MODEL), f32),
        "rel_bias_table": 0.3 * nrm(ks[1], (REL_BUCKETS, N_BIAS_MAPS), f32),
        "norm_pre": 1.0 + 0.05 * nrm(ks[2], (DEPTH, D_MODEL), f32),
        "norm_post": 1.0 + 0.05 * nrm(ks[3], (DEPTH, D_MODEL), f32),
        "nsa_w_in": nrm(ks[4], (A, D_MODEL, NSA_IN), f32) * D_MODEL ** -0.5,
        "nsa_cmp_pe_k": 0.5 * nrm(ks[5], (A, CMP_BLOCK, NSA_HEAD_DIM), f32),
        "nsa_cmp_w1_k": nrm(ks[6], (A, ld, CMP_HIDDEN), f32) * ld ** -0.5,
        "nsa_cmp_w2_k": nrm(ks[7], (A, CMP_HIDDEN, NSA_HEAD_DIM), f32) * CMP_HIDDEN ** -0.5,
        "nsa_cmp_pe_v": 0.5 * nrm(ks[8], (A, CMP_BLOCK, NSA_HEAD_DIM), f32),
        "nsa_cmp_w1_v": nrm(ks[9], (A, ld, CMP_HIDDEN), f32) * ld ** -0.5,
        "nsa_cmp_w2_v": nrm(ks[10], (A, CMP_HIDDEN, NSA_HEAD_DIM), f32) * CMP_HIDDEN ** -0.5,
        "nsa_w_out": nrm(ks[11], (A, NSA_WIDTH, D_MODEL), f32) * NSA_WIDTH ** -0.5,
        "diff_w_in": nrm(ks[12], (Bn, D_MODEL, DIFF_IN), f32) * D_MODEL ** -0.5,
        "diff_lambda_q1": 0.1 * nrm(ks[13], (Bn, DIFF_HALF), f32),
        "diff_lambda_k1": 0.1 * nrm(ks[14], (Bn, DIFF_HALF), f32),
        "diff_lambda_q2": 0.1 * nrm(ks[15], (Bn, DIFF_HALF), f32),
        "diff_lambda_k2": 0.1 * nrm(ks[16], (Bn, DIFF_HALF), f32),
        "diff_subln": 1.0 + 0.05 * nrm(ks[17], (Bn, DIFF_WIDTH), f32),
        "diff_w_out": nrm(ks[18], (Bn, DIFF_WIDTH, D_MODEL), f32) * DIFF_WIDTH ** -0.5,
    }


def reference(x, rel_bias_table, norm_pre, norm_post, nsa_w_in, nsa_cmp_pe_k, nsa_cmp_w1_k,
              nsa_cmp_w2_k, nsa_cmp_pe_v, nsa_cmp_w1_v, nsa_cmp_w2_v, nsa_w_out, diff_w_in,
              diff_lambda_q1, diff_lambda_k1, diff_lambda_q2, diff_lambda_k2, diff_subln, diff_w_out):
    for i in range(DEPTH):
        u = rmsnorm(x, norm_pre[i])
        j = i // N_MIXERS
        if i % N_MIXERS == 0:
            y = nsa_mixer(u, rel_bias_table, nsa_w_in[j], nsa_cmp_pe_k[j], nsa_cmp_w1_k[j],
                          nsa_cmp_w2_k[j], nsa_cmp_pe_v[j], nsa_cmp_w1_v[j], nsa_cmp_w2_v[j],
                          nsa_w_out[j])
        else:
            lambda_init = 0.8 - 0.6 * math.exp(-0.3 * i)
            y = diff_mixer(u, rel_bias_table, diff_w_in[j], diff_lambda_q1[j], diff_lambda_k1[j],
                           diff_lambda_q2[j], diff_lambda_k2[j], diff_subln[j], diff_w_out[j],
                           lambda_init)
        x = x + rmsnorm(y, norm_post[i])
    return x
```

```python
import functools
import math

import jax
import jax.numpy as jnp
import numpy as np
from jax import lax
from jax.experimental import pallas as pl
from jax.experimental.pallas import tpu as pltpu

F32 = jnp.float32
BF16 = jnp.bfloat16

D_MODEL = 1024
DEPTH = 2
REL_BUCKETS = 32
REL_MAX_DIST = 128
NSA_HEADS = 16
NSA_GROUPS = 4
NSA_HPG = NSA_HEADS // NSA_GROUPS
HEAD_DIM = 64
CMP_BLOCK = 32
CMP_STRIDE = 16
CMP_HIDDEN = 128
SEL_BLOCK = 64
SEL_TOPK = 8
WINDOW = 512
DIFF_HEADS = 8
DIFF_VDIM = 128
NEG = -1e30
BIG = 1e9
EPS = 1e-6

LANES = 128
TQ = 128
TK = 2 * LANES
N_BIAS_KINDS = 5
VMEM_LIMIT = 52 * 1024 * 1024


def _sigmoid(x):
    return 1.0 / (1.0 + jnp.exp(-x))


def _nt_dot(a, b):
    return lax.dot_general(a, b, (((1,), (1,)), ((), ())), preferred_element_type=F32)


def _norm_proj_kernel(x_ref, g_ref, *refs, n_w):
    w_refs, o_refs = refs[:n_w], refs[n_w:]
    x = x_ref[...]
    u = x * lax.rsqrt(jnp.mean(x * x, axis=-1, keepdims=True) + EPS) * g_ref[...]
    ub = u.astype(BF16)
    for w_ref, o_ref in zip(w_refs, o_refs):
        o_ref[...] = jnp.dot(ub, w_ref[...], preferred_element_type=F32).astype(o_ref.dtype)


def _norm_proj(x2, g, weights, out_dtypes, tm=256):
    n, d = x2.shape
    in_specs = [pl.BlockSpec((tm, d), lambda i: (i, 0)), pl.BlockSpec((1, d), lambda i: (0, 0))]
    out_specs, out_shape = [], []
    for w, dt in zip(weights, out_dtypes):
        c = w.shape[1]
        in_specs.append(pl.BlockSpec((d, c), lambda i: (0, 0)))
        out_specs.append(pl.BlockSpec((tm, c), lambda i: (i, 0)))
        out_shape.append(jax.ShapeDtypeStruct((n, c), dt))
    return pl.pallas_call(
        functools.partial(_norm_proj_kernel, n_w=len(weights)),
        grid=(n // tm,),
        in_specs=in_specs,
        out_specs=out_specs,
        out_shape=out_shape,
        compiler_params=pltpu.CompilerParams(
            dimension_semantics=("arbitrary",), vmem_limit_bytes=VMEM_LIMIT),
        name="norm_proj",
    )(x2, g.reshape(1, d), *weights)


def _out_proj_kernel(o_ref, w_ref, x_ref, g_ref, y_ref):
    y = jnp.dot(o_ref[...], w_ref[...], preferred_element_type=F32)
    r = y * lax.rsqrt(jnp.mean(y * y, axis=-1, keepdims=True) + EPS) * g_ref[...]
    y_ref[...] = x_ref[...] + r


def _out_proj(o2, w, x2, g, tm=512):
    n, d = x2.shape
    return pl.pallas_call(
        _out_proj_kernel,
        grid=(n // tm,),
        in_specs=[pl.BlockSpec((tm, o2.shape[1]), lambda i: (i, 0)),
                  pl.BlockSpec(w.shape, lambda i: (0, 0)),
                  pl.BlockSpec((tm, d), lambda i: (i, 0)),
                  pl.BlockSpec((1, d), lambda i: (0, 0))],
        out_specs=pl.BlockSpec((tm, d), lambda i: (i, 0)),
        out_shape=jax.ShapeDtypeStruct((n, d), F32),
        compiler_params=pltpu.CompilerParams(
            dimension_semantics=("arbitrary",), vmem_limit_bytes=VMEM_LIMIT),
        name="out_proj",
    )(o2, w, x2, g.reshape(1, d))


def _bias_kernel(tbl_ref, dn_ref, bc_ref):
    h = pl.program_id(0)
    c31 = tbl_ref[REL_BUCKETS - 1, h]
    max_exact = REL_BUCKETS // 2

    def rel_bias(dist):
        n = jnp.maximum(dist, 0)
        nf = jnp.maximum(n, 1).astype(F32)
        large = max_exact + (jnp.log(nf / max_exact) / math.log(REL_MAX_DIST / max_exact)
                             * (REL_BUCKETS - max_exact)).astype(jnp.int32)
        large = jnp.minimum(large, REL_BUCKETS - 1)
        bucket = jnp.where(n < max_exact, n, large)
        out = jnp.zeros(dist.shape, F32)
        for b in range(REL_BUCKETS):
            out = jnp.where(bucket == b, tbl_ref[b, h], out)
        return out - c31

    i = lax.broadcasted_iota(jnp.int32, (TQ, LANES), 0)
    j = lax.broadcasted_iota(jnp.int32, (TQ, LANES), 1)
    dn_ref[0, 0] = jnp.where(j <= i, rel_bias(i - j), NEG)
    dn_ref[0, 1] = rel_bias(LANES + i - j)
    dn_ref[0, 2] = jnp.zeros((TQ, LANES), F32)
    dn_ref[0, 3] = jnp.where(j > i, 0.0, NEG)
    dn_ref[0, 4] = jnp.full((TQ, LANES), NEG, F32)

    n_cmp = (bc_ref.shape[1] - CMP_BLOCK) // CMP_STRIDE + 1

    def body(r, carry):
        t = r * TQ + i
        d = t - (j * CMP_STRIDE + CMP_BLOCK - 1)
        valid = (d >= 0) & (j < n_cmp)
        bc_ref[0, pl.ds(pl.multiple_of(r * TQ, TQ), TQ), :] = jnp.where(valid, rel_bias(d), NEG)
        return carry

    lax.fori_loop(0, bc_ref.shape[1] // TQ, body, 0)


def _bias_tiles(table, seq):
    n_maps = table.shape[1]
    return pl.pallas_call(
        _bias_kernel,
        grid=(n_maps,),
        in_specs=[pl.BlockSpec(memory_space=pltpu.SMEM)],
        out_specs=[pl.BlockSpec((1, N_BIAS_KINDS, TQ, LANES), lambda h: (h, 0, 0, 0)),
                   pl.BlockSpec((1, seq, LANES), lambda h: (h, 0, 0))],
        out_shape=[jax.ShapeDtypeStruct((n_maps, N_BIAS_KINDS, TQ, LANES), F32),
                   jax.ShapeDtypeStruct((n_maps, seq, LANES), F32)],
        compiler_params=pltpu.CompilerParams(dimension_semantics=("arbitrary",)),
        name="bias_tiles",
    )(table)


def _compress_kernel(x_ref, pe_ref, w1_ref, w2_ref, o_ref):
    half = CMP_BLOCK // 2
    n_chunks = x_ref.shape[1] // CMP_STRIDE
    u = jnp.zeros((n_chunks, 2 * CMP_HIDDEN), F32)
    v = jnp.zeros((n_chunks, 2 * CMP_HIDDEN), F32)
    for l in range(half):
        a = x_ref[0, pl.ds(l, n_chunks, stride=CMP_STRIDE), :]
        u = u + jnp.dot((a + pe_ref[l:l + 1, :]).astype(BF16), w1_ref[l],
                        preferred_element_type=F32)
        v = v + jnp.dot((a + pe_ref[half + l:half + l + 1, :]).astype(BF16), w1_ref[half + l],
                        preferred_element_type=F32)
    hid = u + pltpu.roll(v, n_chunks - 1, 0)
    hid = hid * _sigmoid(hid)
    o_ref[0, 0] = jnp.dot(hid.astype(BF16), w2_ref[...], preferred_element_type=F32).astype(BF16)


def _compress(cmp_in, pe_kv, w1_kv, w2_kv):
    b, s, _ = cmp_in.shape
    n_chunks = s // CMP_STRIDE
    return pl.pallas_call(
        _compress_kernel,
        grid=(b, NSA_GROUPS),
        in_specs=[pl.BlockSpec((1, s, LANES), lambda bi, g: (bi, 0, g)),
                  pl.BlockSpec(pe_kv.shape, lambda bi, g: (0, 0)),
                  pl.BlockSpec(w1_kv.shape, lambda bi, g: (0, 0, 0)),
                  pl.BlockSpec(w2_kv.shape, lambda bi, g: (0, 0))],
        out_specs=pl.BlockSpec((1, 1, n_chunks, 2 * LANES), lambda bi, g: (bi, g, 0, 0)),
        out_shape=jax.ShapeDtypeStruct((b, NSA_GROUPS, n_chunks, 2 * LANES), BF16),
        compiler_params=pltpu.CompilerParams(
            dimension_semantics=("arbitrary", "arbitrary"), vmem_limit_bytes=VMEM_LIMIT),
        name="nsa_compress",
    )(cmp_in, pe_kv, w1_kv, w2_kv)


def _bias_kind(delta, window):
    kind = jnp.minimum(delta, 2)
    if window:
        kind = jnp.where(delta == WINDOW // LANES, 3, kind)
        kind = jnp.where(delta > WINDOW // LANES, 4, kind)
    return jnp.where(delta < 0, 4, kind)


def _attend(qx_ref, load_k, load_v, bias_ref, row_blocks, lo, hi, window, mask_ref,
            sc_ref, mx_ref, ls_ref, acc_ref):
    n_rows = qx_ref.shape[0]
    mx_ref[...] = jnp.full(mx_ref.shape, 3 * NEG, F32)

    def scores(kp, carry):
        s = _nt_dot(qx_ref[...], load_k(kp))
        for rb, (slot, qtile) in enumerate(row_blocks):
            rows = slice(rb * TQ, (rb + 1) * TQ)
            m_rb = mx_ref[rows, :]
            for hf in range(TK // LANES):
                cols = slice(hf * LANES, (hf + 1) * LANES)
                ktile = kp * (TK // LANES) + hf
                t = s[rows, cols] + bias_ref[slot, _bias_kind(qtile - ktile, window)]
                if mask_ref is not None:
                    t = t + mask_ref[ktile]
                sc_ref[kp, rows, cols] = t
                m_rb = jnp.maximum(m_rb, t)
            mx_ref[rows, :] = m_rb
        return carry

    lax.fori_loop(lo, hi, scores, 0)

    m = jnp.max(mx_ref[...], axis=-1, keepdims=True)
    mx_ref[...] = jnp.broadcast_to(m, mx_ref.shape)
    ls_ref[...] = jnp.zeros(ls_ref.shape, F32)
    acc_ref[...] = jnp.zeros(acc_ref.shape, F32)

    def weighted(kp, carry):
        mb = mx_ref[...]
        ps = [jnp.exp(sc_ref[kp, :, hf * LANES:(hf + 1) * LANES] - mb) for hf in range(TK // LANES)]
        ls_ref[...] += functools.reduce(lambda a, b: a + b, ps)
        pb = jnp.concatenate(ps, axis=1).astype(BF16)
        acc_ref[...] += jnp.dot(pb, load_v(kp), preferred_element_type=F32)
        return carry

    lax.fori_loop(lo, hi, weighted, 0)
    del n_rows


def _key_loader(ref):
    def load(kp):
        return ref[0, pl.ds(pl.multiple_of(kp * TK, TK), TK), :]
    return load


def _nsa_kernel(q_ref, ks_ref, vs_ref, kw_ref, vw_ref, kvc_ref, gate_ref, z_ref, dn_ref, bc_ref,
                ov_ref, et_ref, o_ref,
                qx_ref, sc_ref, mx_ref, ls_s_ref, acc_s_ref, ls_w_ref, acc_w_ref, mask_ref):
    qt = pl.program_id(2)
    lane = lax.broadcasted_iota(jnp.int32, (TQ, LANES), 1)
    row = lax.broadcasted_iota(jnp.int32, (TQ, LANES), 0)
    lo_half = lane < HEAD_DIM

    for j in range(NSA_HPG):
        slab = q_ref[0, :, (j // 2) * LANES:(j // 2 + 1) * LANES]
        keep = lo_half if j % 2 == 0 else jnp.logical_not(lo_half)
        qx_ref[j * TQ:(j + 1) * TQ, :] = jnp.where(keep, slab, jnp.zeros_like(slab))

    kcc = kvc_ref[0, 0, :, 0:LANES]
    vcc = kvc_ref[0, 0, :, LANES:2 * LANES]
    s_c = _nt_dot(qx_ref[...], kcc)
    t_row = qt * TQ + row[:, 0:1]
    any_valid = (t_row >= CMP_BLOCK - 1).astype(F32)
    p_sum = jnp.zeros((TQ, LANES), F32)
    p_list = []
    for j in range(NSA_HPG):
        sj = s_c[j * TQ:(j + 1) * TQ, :] + bc_ref[j]
        mj = jnp.max(sj, axis=-1, keepdims=True)
        pj = jnp.exp(sj - mj)
        pj = pj * (any_valid / jnp.sum(pj, axis=-1, keepdims=True))
        p_sum = p_sum + pj
        p_list.append(pj.astype(BF16))
    o_c = jnp.dot(jnp.concatenate(p_list, axis=0), vcc, preferred_element_type=F32)

    imp = jnp.dot(p_sum, ov_ref[...], precision=lax.Precision.HIGHEST, preferred_element_type=F32)
    cur = 2 * qt + (row >= SEL_BLOCK).astype(jnp.int32)
    forced = (lane == 0) | (lane == cur) | (lane == cur - 1)
    val = jnp.where(forced, BIG, jnp.where(lane > cur, -BIG, imp))
    n_sel = et_ref.shape[1] // SEL_BLOCK
    val = jnp.where(lane < n_sel, val, 3 * NEG)
    lane_f = lane.astype(F32)
    picked = jnp.zeros((TQ, LANES), F32)
    for _ in range(min(SEL_TOPK, n_sel)):
        top = jnp.max(val, axis=-1, keepdims=True)
        first = jnp.min(jnp.where(val == top, lane_f, 1e9), axis=-1, keepdims=True)
        hit = lane_f == first
        picked = jnp.where(hit, 1.0, picked)
        val = jnp.where(hit, 3 * NEG, val)
    block_mask = jnp.where(picked > 0.0, 0.0, NEG).astype(BF16)
    key_mask = jnp.dot(block_mask, et_ref[...], preferred_element_type=F32)
    for t in range(mask_ref.shape[0]):
        mask_ref[t] = key_mask[:, t * LANES:(t + 1) * LANES]

    heads = [(j, qt) for j in range(NSA_HPG)]
    hi = qt // 2 + 1
    _attend(qx_ref, _key_loader(ks_ref), _key_loader(vs_ref), dn_ref, heads, 0, hi, False,
            mask_ref, sc_ref, mx_ref, ls_s_ref, acc_s_ref)
    lo_w = jnp.maximum(qt - WINDOW // LANES, 0) // 2
    _attend(qx_ref, _key_loader(kw_ref), _key_loader(vw_ref), dn_ref, heads, lo_w, hi, True,
            None, sc_ref, mx_ref, ls_w_ref, acc_w_ref)

    gate = _sigmoid(gate_ref[0].astype(F32))
    outs = []
    for j in range(NSA_HPG):
        rows = slice(j * TQ, (j + 1) * TQ)
        l_s = jnp.sum(ls_s_ref[rows, :], axis=-1, keepdims=True)
        l_w = jnp.sum(ls_w_ref[rows, :], axis=-1, keepdims=True)
        outs.append(gate[:, 3 * j:3 * j + 1] * o_c[rows, :]
                    + (gate[:, 3 * j + 1:3 * j + 2] / l_s) * acc_s_ref[rows, :]
                    + (gate[:, 3 * j + 2:3 * j + 3] / l_w) * acc_w_ref[rows, :])
    o = jnp.concatenate([jnp.where(lo_half, outs[0], outs[1]),
                         jnp.where(lo_half, outs[2], outs[3])], axis=1)
    z = z_ref[0].astype(F32)
    o_ref[0] = (o * (z * _sigmoid(z))).astype(o_ref.dtype)


def _nsa_attention(p3, kvc, dn, bc, ov, et, col):
    b, s, _ = p3.shape
    n_qt = s // TQ
    n_rows = NSA_HPG * TQ
    wide = 2 * LANES

    def slab(first_lane_block, per_group):
        return pl.BlockSpec((1, s, LANES),
                            lambda bi, g, qt: (bi, 0, first_lane_block + per_group * g))

    in_specs = [
        pl.BlockSpec((1, TQ, wide), lambda bi, g, qt: (bi, qt, col["q"] // wide + g)),
        slab(col["sel"] // LANES, 2), slab(col["sel"] // LANES + 1, 2),
        slab(col["win"] // LANES, 2), slab(col["win"] // LANES + 1, 2),
        pl.BlockSpec((1, 1, kvc.shape[2], wide), lambda bi, g, qt: (bi, g, 0, 0)),
        pl.BlockSpec((1, TQ, LANES), lambda bi, g, qt: (bi, qt, col["gate"] // LANES + g)),
        pl.BlockSpec((1, TQ, wide), lambda bi, g, qt: (bi, qt, col["z"] // wide + g)),
        pl.BlockSpec((NSA_HPG, N_BIAS_KINDS, TQ, LANES), lambda bi, g, qt: (g, 0, 0, 0)),
        pl.BlockSpec((NSA_HPG, TQ, LANES), lambda bi, g, qt: (g, qt, 0)),
        pl.BlockSpec(ov.shape, lambda bi, g, qt: (0, 0)),
        pl.BlockSpec(et.shape, lambda bi, g, qt: (0, 0)),
    ]
    return pl.pallas_call(
        _nsa_kernel,
        grid=(b, NSA_GROUPS, n_qt),
        in_specs=in_specs,
        out_specs=pl.BlockSpec((1, TQ, wide), lambda bi, g, qt: (bi, qt, g)),
        out_shape=jax.ShapeDtypeStruct((b, s, NSA_GROUPS * wide), BF16),
        scratch_shapes=[
            pltpu.VMEM((n_rows, LANES), BF16),
            pltpu.VMEM((s // TK, n_rows, TK), F32),
            pltpu.VMEM((n_rows, LANES), F32),
            pltpu.VMEM((n_rows, LANES), F32),
            pltpu.VMEM((n_rows, LANES), F32),
            pltpu.VMEM((n_rows, LANES), F32),
            pltpu.VMEM((n_rows, LANES), F32),
            pltpu.VMEM((s // LANES, TQ, LANES), F32),
        ],
        compiler_params=pltpu.CompilerParams(
            dimension_semantics=("arbitrary", "arbitrary", "arbitrary"),
            vmem_limit_bytes=VMEM_LIMIT),
        name="nsa_attention",
    )(p3, p3, p3, p3, p3, kvc, p3, p3, dn, bc, ov, et)


def _diff_kernel(q_ref, k_ref, v_ref, z_ref, dn_ref, lq1_ref, lk1_ref, lq2_ref, lk2_ref, sub_ref,
                 o_ref, qx_ref, sc_ref, mx_ref, ls0_ref, acc0_ref, ls1_ref, acc1_ref,
                 *, lambda_init):
    qt2 = pl.program_id(2)
    n_rb = qx_ref.shape[0] // TQ
    lane = lax.broadcasted_iota(jnp.int32, qx_ref.shape, 1)
    q = q_ref[0]
    maps = []
    for m, (ls_ref, acc_ref) in enumerate(((ls0_ref, acc0_ref), (ls1_ref, acc1_ref))):
        keep = (lane < HEAD_DIM) if m == 0 else (lane >= HEAD_DIM)
        qx_ref[...] = jnp.where(keep, q, jnp.zeros_like(q))
        blocks = [(m, n_rb * qt2 + r) for r in range(n_rb)]
        _attend(qx_ref, _key_loader(k_ref), _key_loader(v_ref), dn_ref, blocks, 0,
                (n_rb * qt2 + n_rb - 1) // 2 + 1, False, None, sc_ref, mx_ref, ls_ref, acc_ref)
        maps.append(acc_ref[...] / jnp.sum(ls_ref[...], axis=-1, keepdims=True))
    lam = (jnp.exp(jnp.sum(lq1_ref[...] * lk1_ref[...], axis=-1, keepdims=True))
           - jnp.exp(jnp.sum(lq2_ref[...] * lk2_ref[...], axis=-1, keepdims=True)) + lambda_init)
    o = maps[0] - lam * maps[1]
    o = o * lax.rsqrt(jnp.mean(o * o, axis=-1, keepdims=True) + EPS) * sub_ref[...]
    o = o * (1.0 - lambda_init)
    z = z_ref[0].astype(F32)
    o_ref[0] = (o * (z * _sigmoid(z))).astype(o_ref.dtype)


def _diff_attention(p3, dn, lq1, lk1, lq2, lk2, subln, lambda_init, tq=2 * TQ):
    b, s, _ = p3.shape
    h = DIFF_HEADS
    vec = lambda a: a.reshape(1, -1)
    small = pl.BlockSpec((1, HEAD_DIM), lambda bi, hi, qi: (0, 0))
    in_specs = [
        pl.BlockSpec((1, tq, LANES), lambda bi, hi, qi: (bi, qi, hi)),
        pl.BlockSpec((1, s, LANES), lambda bi, hi, qi: (bi, 0, h + hi)),
        pl.BlockSpec((1, s, LANES), lambda bi, hi, qi: (bi, 0, 2 * h + hi)),
        pl.BlockSpec((1, tq, LANES), lambda bi, hi, qi: (bi, qi, 3 * h + hi)),
        pl.BlockSpec((2, N_BIAS_KINDS, TQ, LANES), lambda bi, hi, qi: (hi, 0, 0, 0)),
        small, small, small, small,
        pl.BlockSpec((1, DIFF_VDIM), lambda bi, hi, qi: (0, hi)),
    ]
    return pl.pallas_call(
        functools.partial(_diff_kernel, lambda_init=lambda_init),
        grid=(b, h, s // tq),
        in_specs=in_specs,
        out_specs=pl.BlockSpec((1, tq, LANES), lambda bi, hi, qi: (bi, qi, hi)),
        out_shape=jax.ShapeDtypeStruct((b, s, h * DIFF_VDIM), BF16),
        scratch_shapes=[
            pltpu.VMEM((tq, LANES), BF16),
            pltpu.VMEM((s // TK, tq, TK), F32),
            pltpu.VMEM((tq, LANES), F32),
            pltpu.VMEM((tq, LANES), F32),
            pltpu.VMEM((tq, LANES), F32),
            pltpu.VMEM((tq, LANES), F32),
            pltpu.VMEM((tq, LANES), F32),
        ],
        compiler_params=pltpu.CompilerParams(
            dimension_semantics=("arbitrary", "arbitrary", "arbitrary"),
            vmem_limit_bytes=VMEM_LIMIT),
        name="diff_attention",
    )(p3, p3, p3, p3, dn, vec(lq1), vec(lk1), vec(lq2), vec(lk2), vec(subln))


def _nsa_weight_layout(w_in):
    d = HEAD_DIM
    kv = NSA_GROUPS * d
    width = NSA_HEADS * d
    pts = np.cumsum([width] + [kv] * 6 + [3 * NSA_HEADS, width])
    q, kc, vc, ks, vs, kw, vw, gate, z = [w_in[:, a:b] for a, b in zip([0] + list(pts[:-1]), pts)]
    grp = lambda w, g: w[:, g * d:(g + 1) * d]
    dup = lambda k, v: jnp.concatenate(
        [jnp.concatenate([grp(k, g), grp(k, g), grp(v, g), grp(v, g)], axis=1)
         for g in range(NSA_GROUPS)], axis=1)
    n_gate = 3 * NSA_HPG
    gate_pad = jnp.concatenate(
        [jnp.pad(gate[:, g * n_gate:(g + 1) * n_gate], ((0, 0), (0, LANES - n_gate)))
         for g in range(NSA_GROUPS)], axis=1)
    parts = [("q", q * (d ** -0.5)), ("sel", dup(ks, vs)), ("win", dup(kw, vw)),
             ("gate", gate_pad), ("z", z)]
    col, off = {}, 0
    for name, w in parts:
        col[name] = off
        off += w.shape[1]
    w_main = jnp.concatenate([w for _, w in parts], axis=1).astype(BF16)
    w_cmp = jnp.concatenate(
        [jnp.concatenate([grp(kc, g), grp(vc, g)], axis=1) for g in range(NSA_GROUPS)],
        axis=1).astype(BF16)
    return w_main, w_cmp, col


def _compress_weight_layout(pe_k, w1_k, w2_k, pe_v, w1_v, w2_v):
    d, hdn = HEAD_DIM, CMP_HIDDEN
    pe_kv = jnp.concatenate([pe_k, pe_v], axis=1)
    w1k = w1_k.reshape(CMP_BLOCK, d, hdn)
    w1v = w1_v.reshape(CMP_BLOCK, d, hdn)
    zero = jnp.zeros_like(w1k)
    w1_kv = jnp.concatenate([jnp.concatenate([w1k, zero], axis=2),
                             jnp.concatenate([zero, w1v], axis=2)], axis=1).astype(BF16)
    z2 = jnp.zeros((hdn, 2 * d), w2_k.dtype)
    w2_kv = jnp.concatenate([jnp.concatenate([w2_k, w2_k, z2], axis=1),
                             jnp.concatenate([z2, w2_v, w2_v], axis=1)], axis=0).astype(BF16)
    return pe_kv, w1_kv, w2_kv


def _selection_constants(seq):
    n_cmp_pad = seq // CMP_STRIDE
    n_cmp = (seq - CMP_BLOCK) // CMP_STRIDE + 1
    n_sel = seq // SEL_BLOCK
    cmp_lo = np.arange(n_cmp_pad) * CMP_STRIDE
    sel_lo = np.arange(n_sel) * SEL_BLOCK
    overlap = np.maximum(np.minimum(cmp_lo[:, None] + CMP_BLOCK, sel_lo[None, :] + SEL_BLOCK)
                         - np.maximum(cmp_lo[:, None], sel_lo[None, :]), 0).astype(np.float32) / CMP_BLOCK
    overlap[n_cmp:] = 0.0
    ov = np.zeros((n_cmp_pad, LANES), np.float32)
    ov[:, :n_sel] = overlap
    et = np.zeros((LANES, seq), np.float32)
    et[np.arange(seq) // SEL_BLOCK, np.arange(seq)] = 1.0
    return jnp.asarray(ov), jnp.asarray(et, dtype=BF16)


def kernel(x, rel_bias_table, norm_pre, norm_post, nsa_w_in, nsa_cmp_pe_k, nsa_cmp_w1_k, nsa_cmp_w2_k,
           nsa_cmp_pe_v, nsa_cmp_w1_v, nsa_cmp_w2_v, nsa_w_out, diff_w_in, diff_lambda_q1,
           diff_lambda_k1, diff_lambda_q2, diff_lambda_k2, diff_subln, diff_w_out):
    b, s, d = x.shape
    n = b * s
    assert d == D_MODEL and s % TK == 0 and s // SEL_BLOCK <= LANES and s // CMP_STRIDE == LANES
    dn, bc = _bias_tiles(rel_bias_table, s)
    ov, et = _selection_constants(s)
    x2 = x.reshape(n, d)
    for i in range(DEPTH):
        j = i // 2
        if i % 2 == 0:
            w_main, w_cmp, col = _nsa_weight_layout(nsa_w_in[j])
            p_main, p_cmp = _norm_proj(x2, norm_pre[i], [w_main, w_cmp], [BF16, F32])
            kvc = _compress(p_cmp.reshape(b, s, -1),
                            *_compress_weight_layout(nsa_cmp_pe_k[j], nsa_cmp_w1_k[j], nsa_cmp_w2_k[j],
                                                     nsa_cmp_pe_v[j], nsa_cmp_w1_v[j], nsa_cmp_w2_v[j]))
            o = _nsa_attention(p_main.reshape(b, s, -1), kvc, dn, bc, ov, et, col)
            w_out = nsa_w_out[j]
        else:
            lambda_init = 0.8 - 0.6 * math.exp(-0.3 * i)
            w = diff_w_in[j]
            w = jnp.concatenate([w[:, :d] * (HEAD_DIM ** -0.5), w[:, d:]], axis=1).astype(BF16)
            (p_main,) = _norm_proj(x2, norm_pre[i], [w], [BF16])
            o = _diff_attention(p_main.reshape(b, s, -1), dn, diff_lambda_q1[j], diff_lambda_k1[j],
                                diff_lambda_q2[j], diff_lambda_k2[j], diff_subln[j], lambda_init)
            w_out = diff_w_out[j]
        x2 = _out_proj(o.reshape(n, -1), w_out.astype(BF16), x2, norm_post[i])
    return x2.reshape(b, s, d)
```

```python
import functools
import math

import jax
import jax.numpy as jnp
import numpy as np
from jax import lax
from jax.experimental import pallas as pl
from jax.experimental.pallas import tpu as pltpu

F32 = jnp.float32
BF16 = jnp.bfloat16

D_MODEL = 1024
DEPTH = 2
REL_BUCKETS = 32
REL_MAX_DIST = 128
NSA_HEADS = 16
NSA_GROUPS = 4
NSA_HPG = NSA_HEADS // NSA_GROUPS
HEAD_DIM = 64
CMP_BLOCK = 32
CMP_STRIDE = 16
CMP_HIDDEN = 128
SEL_BLOCK = 64
SEL_TOPK = 8
WINDOW = 512
DIFF_HEADS = 8
DIFF_VDIM = 128
NEG = -1e30
BIG = 1e9
EPS = 1e-6

LANES = 128
TQ = 128
TK = 2 * LANES
N_BIAS_KINDS = 5
VMEM_LIMIT = 52 * 1024 * 1024


def _sigmoid(x):
    return 1.0 / (1.0 + jnp.exp(-x))


def _nt_dot(a, b):
    return lax.dot_general(a, b, (((1,), (1,)), ((), ())), preferred_element_type=F32)


def _norm_proj_kernel(x_ref, g_ref, *refs, n_w):
    w_refs, o_refs = refs[:n_w], refs[n_w:]
    x = x_ref[...]
    u = x * lax.rsqrt(jnp.mean(x * x, axis=-1, keepdims=True) + EPS) * g_ref[...]
    ub = u.astype(BF16)
    for w_ref, o_ref in zip(w_refs, o_refs):
        o_ref[...] = jnp.dot(ub, w_ref[...], preferred_element_type=F32).astype(o_ref.dtype)


def _norm_proj(x2, g, weights, out_dtypes, tm=256):
    n, d = x2.shape
    in_specs = [pl.BlockSpec((tm, d), lambda i: (i, 0)), pl.BlockSpec((1, d), lambda i: (0, 0))]
    out_specs, out_shape = [], []
    for w, dt in zip(weights, out_dtypes):
        c = w.shape[1]
        in_specs.append(pl.BlockSpec((d, c), lambda i: (0, 0)))
        out_specs.append(pl.BlockSpec((tm, c), lambda i: (i, 0)))
        out_shape.append(jax.ShapeDtypeStruct((n, c), dt))
    return pl.pallas_call(
        functools.partial(_norm_proj_kernel, n_w=len(weights)),
        grid=(n // tm,),
        in_specs=in_specs,
        out_specs=out_specs,
        out_shape=out_shape,
        compiler_params=pltpu.CompilerParams(
            dimension_semantics=("arbitrary",), vmem_limit_bytes=VMEM_LIMIT),
        name="norm_proj",
    )(x2, g.reshape(1, d), *weights)


def _out_proj_kernel(o_ref, w_ref, x_ref, g_ref, y_ref):
    y = jnp.dot(o_ref[...], w_ref[...], preferred_element_type=F32)
    r = y * lax.rsqrt(jnp.mean(y * y, axis=-1, keepdims=True) + EPS) * g_ref[...]
    y_ref[...] = x_ref[...] + r


def _out_proj(o2, w, x2, g, tm=512):
    n, d = x2.shape
    return pl.pallas_call(
        _out_proj_kernel,
        grid=(n // tm,),
        in_specs=[pl.BlockSpec((tm, o2.shape[1]), lambda i: (i, 0)),
                  pl.BlockSpec(w.shape, lambda i: (0, 0)),
                  pl.BlockSpec((tm, d), lambda i: (i, 0)),
                  pl.BlockSpec((1, d), lambda i: (0, 0))],
        out_specs=pl.BlockSpec((tm, d), lambda i: (i, 0)),
        out_shape=jax.ShapeDtypeStruct((n, d), F32),
        compiler_params=pltpu.CompilerParams(
            dimension_semantics=("arbitrary",), vmem_limit_bytes=VMEM_LIMIT),
        name="out_proj",
    )(o2, w, x2, g.reshape(1, d))


def _bias_kernel(tbl_ref, dn_ref, bc_ref):
    h = pl.program_id(0)
    c31 = tbl_ref[REL_BUCKETS - 1, h]
    max_exact = REL_BUCKETS // 2

    def rel_bias(dist):
        n = jnp.maximum(dist, 0)
        nf = jnp.maximum(n, 1).astype(F32)
        large = max_exact + (jnp.log(nf / max_exact) / math.log(REL_MAX_DIST / max_exact)
                             * (REL_BUCKETS - max_exact)).astype(jnp.int32)
        large = jnp.minimum(large, REL_BUCKETS - 1)
        bucket = jnp.where(n < max_exact, n, large)
        out = jnp.zeros(dist.shape, F32)
        for b in range(REL_BUCKETS):
            out = jnp.where(bucket == b, tbl_ref[b, h], out)
        return out - c31

    i = lax.broadcasted_iota(jnp.int32, (TQ, LANES), 0)
    j = lax.broadcasted_iota(jnp.int32, (TQ, LANES), 1)
    dn_ref[0, 0] = jnp.where(j <= i, rel_bias(i - j), NEG)
    dn_ref[0, 1] = rel_bias(LANES + i - j)
    dn_ref[0, 2] = jnp.zeros((TQ, LANES), F32)
    dn_ref[0, 3] = jnp.where(j > i, 0.0, NEG)
    dn_ref[0, 4] = jnp.full((TQ, LANES), NEG, F32)

    n_cmp = (bc_ref.shape[1] - CMP_BLOCK) // CMP_STRIDE + 1

    def body(r, carry):
        t = r * TQ + i
        d = t - (j * CMP_STRIDE + CMP_BLOCK - 1)
        valid = (d >= 0) & (j < n_cmp)
        bc_ref[0, pl.ds(pl.multiple_of(r * TQ, TQ), TQ), :] = jnp.where(valid, rel_bias(d), NEG)
        return carry

    lax.fori_loop(0, bc_ref.shape[1] // TQ, body, 0)


def _bias_tiles(table, seq):
    n_maps = table.shape[1]
    return pl.pallas_call(
        _bias_kernel,
        grid=(n_maps,),
        in_specs=[pl.BlockSpec(memory_space=pltpu.SMEM)],
        out_specs=[pl.BlockSpec((1, N_BIAS_KINDS, TQ, LANES), lambda h: (h, 0, 0, 0)),
                   pl.BlockSpec((1, seq, LANES), lambda h: (h, 0, 0))],
        out_shape=[jax.ShapeDtypeStruct((n_maps, N_BIAS_KINDS, TQ, LANES), F32),
                   jax.ShapeDtypeStruct((n_maps, seq, LANES), F32)],
        compiler_params=pltpu.CompilerParams(dimension_semantics=("arbitrary",)),
        name="bias_tiles",
    )(table)


def _compress_kernel(x_ref, pe_ref, w1_ref, w2_ref, o_ref):
    half = CMP_BLOCK // 2
    n_chunks = x_ref.shape[1] // CMP_STRIDE
    u = jnp.zeros((n_chunks, 2 * CMP_HIDDEN), F32)
    v = jnp.zeros((n_chunks, 2 * CMP_HIDDEN), F32)
    for l in range(half):
        a = x_ref[0, pl.ds(l, n_chunks, stride=CMP_STRIDE), :]
        u = u + jnp.dot((a + pe_ref[l:l + 1, :]).astype(BF16), w1_ref[l],
                        preferred_element_type=F32)
        v = v + jnp.dot((a + pe_ref[half + l:half + l + 1, :]).astype(BF16), w1_ref[half + l],
                        preferred_element_type=F32)
    hid = u + pltpu.roll(v, n_chunks - 1, 0)
    hid = hid * _sigmoid(hid)
    o_ref[0, 0] = jnp.dot(hid.astype(BF16), w2_ref[...], preferred_element_type=F32).astype(BF16)


def _compress(cmp_in, pe_kv, w1_kv, w2_kv):
    b, s, _ = cmp_in.shape
    n_chunks = s // CMP_STRIDE
    return pl.pallas_call(
        _compress_kernel,
        grid=(b, NSA_GROUPS),
        in_specs=[pl.BlockSpec((1, s, LANES), lambda bi, g: (bi, 0, g)),
                  pl.BlockSpec(pe_kv.shape, lambda bi, g: (0, 0)),
                  pl.BlockSpec(w1_kv.shape, lambda bi, g: (0, 0, 0)),
                  pl.BlockSpec(w2_kv.shape, lambda bi, g: (0, 0))],
        out_specs=pl.BlockSpec((1, 1, n_chunks, 2 * LANES), lambda bi, g: (bi, g, 0, 0)),
        out_shape=jax.ShapeDtypeStruct((b, NSA_GROUPS, n_chunks, 2 * LANES), BF16),
        compiler_params=pltpu.CompilerParams(
            dimension_semantics=("arbitrary", "arbitrary"), vmem_limit_bytes=VMEM_LIMIT),
        name="nsa_compress",
    )(cmp_in, pe_kv, w1_kv, w2_kv)


def _bias_kind(delta, window):
    kind = jnp.minimum(delta, 2)
    if window:
        kind = jnp.where(delta == WINDOW // LANES, 3, kind)
        kind = jnp.where(delta > WINDOW // LANES, 4, kind)
    return jnp.where(delta < 0, 4, kind)


def _attend(qx_ref, load_k, load_v, bias_ref, row_blocks, lo, hi, window, mask_ref,
            sc_ref, mx_ref, ls_ref, acc_ref):
    n_rows = qx_ref.shape[0]
    mx_ref[...] = jnp.full(mx_ref.shape, 3 * NEG, F32)

    def scores(kp, carry):
        s = _nt_dot(qx_ref[...], load_k(kp))
        for rb, (slot, qtile) in enumerate(row_blocks):
            rows = slice(rb * TQ, (rb + 1) * TQ)
            m_rb = mx_ref[rows, :]
            for hf in range(TK // LANES):
                cols = slice(hf * LANES, (hf + 1) * LANES)
                ktile = kp * (TK // LANES) + hf
                t = s[rows, cols] + bias_ref[slot, _bias_kind(qtile - ktile, window)]
                if mask_ref is not None:
                    t = t + mask_ref[ktile]
                sc_ref[kp, rows, cols] = t
                m_rb = jnp.maximum(m_rb, t)
            mx_ref[rows, :] = m_rb
        return carry

    lax.fori_loop(lo, hi, scores, 0)

    m = jnp.max(mx_ref[...], axis=-1, keepdims=True)
    mx_ref[...] = jnp.broadcast_to(m, mx_ref.shape)
    ls_ref[...] = jnp.zeros(ls_ref.shape, F32)
    acc_ref[...] = jnp.zeros(acc_ref.shape, F32)

    def weighted(kp, carry):
        mb = mx_ref[...]
        ps = [jnp.exp(sc_ref[kp, :, hf * LANES:(hf + 1) * LANES] - mb) for hf in range(TK // LANES)]
        ls_ref[...] += functools.reduce(lambda a, b: a + b, ps)
        pb = jnp.concatenate(ps, axis=1).astype(BF16)
        acc_ref[...] += jnp.dot(pb, load_v(kp), preferred_element_type=F32)
        return carry

    lax.fori_loop(lo, hi, weighted, 0)
    del n_rows


def _key_loader(ref):
    def load(kp):
        return ref[0, pl.ds(pl.multiple_of(kp * TK, TK), TK), :]
    return load


def _nsa_kernel(q_ref, ks_ref, vs_ref, kw_ref, vw_ref, kvc_ref, gate_ref, z_ref, dn_ref, bc_ref,
                ov_ref, et_ref, o_ref,
                qx_ref, sc_ref, mx_ref, ls_s_ref, acc_s_ref, ls_w_ref, acc_w_ref, mask_ref):
    qt = pl.program_id(2)
    lane = lax.broadcasted_iota(jnp.int32, (TQ, LANES), 1)
    row = lax.broadcasted_iota(jnp.int32, (TQ, LANES), 0)
    lo_half = lane < HEAD_DIM

    for j in range(NSA_HPG):
        slab = q_ref[0, :, (j // 2) * LANES:(j // 2 + 1) * LANES]
        keep = lo_half if j % 2 == 0 else jnp.logical_not(lo_half)
        qx_ref[j * TQ:(j + 1) * TQ, :] = jnp.where(keep, slab, jnp.zeros_like(slab))

    kcc = kvc_ref[0, 0, :, 0:LANES]
    vcc = kvc_ref[0, 0, :, LANES:2 * LANES]
    s_c = _nt_dot(qx_ref[...], kcc)
    t_row = qt * TQ + row[:, 0:1]
    any_valid = (t_row >= CMP_BLOCK - 1).astype(F32)
    p_sum = jnp.zeros((TQ, LANES), F32)
    p_list = []
    for j in range(NSA_HPG):
        sj = s_c[j * TQ:(j + 1) * TQ, :] + bc_ref[j]
        mj = jnp.max(sj, axis=-1, keepdims=True)
        pj = jnp.exp(sj - mj)
        pj = pj * (any_valid / jnp.sum(pj, axis=-1, keepdims=True))
        p_sum = p_sum + pj
        p_list.append(pj.astype(BF16))
    o_c = jnp.dot(jnp.concatenate(p_list, axis=0), vcc, preferred_element_type=F32)

    imp = jnp.dot(p_sum, ov_ref[...], precision=lax.Precision.HIGHEST, preferred_element_type=F32)
    cur = 2 * qt + (row >= SEL_BLOCK).astype(jnp.int32)
    forced = (lane == 0) | (lane == cur) | (lane == cur - 1)
    val = jnp.where(forced, BIG, jnp.where(lane > cur, -BIG, imp))
    n_sel = et_ref.shape[1] // SEL_BLOCK
    val = jnp.where(lane < n_sel, val, 3 * NEG)
    lane_f = lane.astype(F32)
    picked = jnp.zeros((TQ, LANES), F32)
    for _ in range(min(SEL_TOPK, n_sel)):
        top = jnp.max(val, axis=-1, keepdims=True)
        first = jnp.min(jnp.where(val == top, lane_f, 1e9), axis=-1, keepdims=True)
        hit = lane_f == first
        picked = jnp.where(hit, 1.0, picked)
        val = jnp.where(hit, 3 * NEG, val)
    block_mask = jnp.where(picked > 0.0, 0.0, NEG).astype(BF16)
    key_mask = jnp.dot(block_mask, et_ref[...], preferred_element_type=F32)
    for t in range(mask_ref.shape[0]):
        mask_ref[t] = key_mask[:, t * LANES:(t + 1) * LANES]

    heads = [(j, qt) for j in range(NSA_HPG)]
    hi = qt // 2 + 1
    _attend(qx_ref, _key_loader(ks_ref), _key_loader(vs_ref), dn_ref, heads, 0, hi, False,
            mask_ref, sc_ref, mx_ref, ls_s_ref, acc_s_ref)
    lo_w = jnp.maximum(qt - WINDOW // LANES, 0) // 2
    _attend(qx_ref, _key_loader(kw_ref), _key_loader(vw_ref), dn_ref, heads, lo_w, hi, True,
            None, sc_ref, mx_ref, ls_w_ref, acc_w_ref)

    gate = _sigmoid(gate_ref[0].astype(F32))
    outs = []
    for j in range(NSA_HPG):
        rows = slice(j * TQ, (j + 1) * TQ)
        l_s = jnp.sum(ls_s_ref[rows, :], axis=-1, keepdims=True)
        l_w = jnp.sum(ls_w_ref[rows, :], axis=-1, keepdims=True)
        outs.append(gate[:, 3 * j:3 * j + 1] * o_c[rows, :]
                    + (gate[:, 3 * j + 1:3 * j + 2] / l_s) * acc_s_ref[rows, :]
                    + (gate[:, 3 * j + 2:3 * j + 3] / l_w) * acc_w_ref[rows, :])
    o = jnp.concatenate([jnp.where(lo_half, outs[0], outs[1]),
                         jnp.where(lo_half, outs[2], outs[3])], axis=1)
    z = z_ref[0].astype(F32)
    o_ref[0] = (o * (z * _sigmoid(z))).astype(o_ref.dtype)


def _nsa_attention(p3, kvc, dn, bc, ov, et, col):
    b, s, _ = p3.shape
    n_qt = s // TQ
    n_rows = NSA_HPG * TQ
    wide = 2 * LANES

    def slab(first_lane_block, per_group):
        return pl.BlockSpec((1, s, LANES),
                            lambda bi, g, qt: (bi, 0, first_lane_block + per_group * g))

    in_specs = [
        pl.BlockSpec((1, TQ, wide), lambda bi, g, qt: (bi, qt, col["q"] // wide + g)),
        slab(col["sel"] // LANES, 2), slab(col["sel"] // LANES + 1, 2),
        slab(col["win"] // LANES, 2), slab(col["win"] // LANES + 1, 2),
        pl.BlockSpec((1, 1, kvc.shape[2], wide), lambda bi, g, qt: (bi, g, 0, 0)),
        pl.BlockSpec((1, TQ, LANES), lambda bi, g, qt: (bi, qt, col["gate"] // LANES + g)),
        pl.BlockSpec((1, TQ, wide), lambda bi, g, qt: (bi, qt, col["z"] // wide + g)),
        pl.BlockSpec((NSA_HPG, N_BIAS_KINDS, TQ, LANES), lambda bi, g, qt: (g, 0, 0, 0)),
        pl.BlockSpec((NSA_HPG, TQ, LANES), lambda bi, g, qt: (g, qt, 0)),
        pl.BlockSpec(ov.shape, lambda bi, g, qt: (0, 0)),
        pl.BlockSpec(et.shape, lambda bi, g, qt: (0, 0)),
    ]
    return pl.pallas_call(
        _nsa_kernel,
        grid=(b, NSA_GROUPS, n_qt),
        in_specs=in_specs,
        out_specs=pl.BlockSpec((1, TQ, wide), lambda bi, g, qt: (bi, qt, g)),
        out_shape=jax.ShapeDtypeStruct((b, s, NSA_GROUPS * wide), BF16),
        scratch_shapes=[
            pltpu.VMEM((n_rows, LANES), BF16),
            pltpu.VMEM((s // TK, n_rows, TK), F32),
            pltpu.VMEM((n_rows, LANES), F32),
            pltpu.VMEM((n_rows, LANES), F32),
            pltpu.VMEM((n_rows, LANES), F32),
            pltpu.VMEM((n_rows, LANES), F32),
            pltpu.VMEM((n_rows, LANES), F32),
            pltpu.VMEM((s // LANES, TQ, LANES), F32),
        ],
        compiler_params=pltpu.CompilerParams(
            dimension_semantics=("arbitrary", "arbitrary", "arbitrary"),
            vmem_limit_bytes=VMEM_LIMIT),
        name="nsa_attention",
    )(p3, p3, p3, p3, p3, kvc, p3, p3, dn, bc, ov, et)


def _tile_kind(delta, window=False):
    if delta < 0 or (window and delta > WINDOW // LANES):
        return 4
    if window and delta == WINDOW // LANES:
        return 3
    return min(delta, 2)


def _attend_static(qx, k_tiles, v_tiles, kinds, bias_ref, sc_ref):
    n_rb = qx.shape[0] // TQ
    n_hf = TK // LANES
    mx = [None] * n_rb
    for i, load_k in enumerate(k_tiles):
        s = _nt_dot(qx, load_k())
        for rb in range(n_rb):
            rows = slice(rb * TQ, (rb + 1) * TQ)
            for hf in range(n_hf):
                cols = slice(hf * LANES, (hf + 1) * LANES)
                slot, kind = kinds[i][rb][hf]
                if kind == 4:
                    sc_ref[i, rows, cols] = jnp.full((TQ, LANES), NEG, F32)
                    continue
                t = s[rows, cols]
                if kind != 2:
                    t = t + bias_ref[slot, kind]
                sc_ref[i, rows, cols] = t
                mx[rb] = t if mx[rb] is None else jnp.maximum(mx[rb], t)
    mb = jnp.concatenate(
        [jnp.broadcast_to(jnp.max(v, axis=-1, keepdims=True), (TQ, LANES)) for v in mx], axis=0)
    acc = jnp.zeros((qx.shape[0], LANES), F32)
    ls = jnp.zeros((qx.shape[0], LANES), F32)
    for i, load_v in enumerate(v_tiles):
        ps = [jnp.exp(sc_ref[i, :, hf * LANES:(hf + 1) * LANES] - mb) for hf in range(n_hf)]
        ls = ls + functools.reduce(lambda a, b: a + b, ps)
        acc = acc + jnp.dot(jnp.concatenate(ps, axis=1).astype(BF16), load_v(),
                            preferred_element_type=F32)
    return acc, jnp.sum(ls, axis=-1, keepdims=True)


def _diff_kernel(q_ref, k_ref, v_ref, z_ref, dn_ref, lq1_ref, lk1_ref, lq2_ref, lk2_ref, sub_ref,
                 o_ref, sc_ref, *, lambda_init, tq):
    seq = q_ref.shape[1]
    n_rb = tq // TQ
    lane = lax.broadcasted_iota(jnp.int32, (tq, LANES), 1)
    lam = (jnp.exp(jnp.sum(lq1_ref[...] * lk1_ref[...], axis=-1, keepdims=True))
           - jnp.exp(jnp.sum(lq2_ref[...] * lk2_ref[...], axis=-1, keepdims=True)) + lambda_init)
    tile = lambda ref, kp: (lambda: ref[0, kp * TK:(kp + 1) * TK, :])
    for x in range(seq // tq):
        rows = slice(x * tq, (x + 1) * tq)
        q = q_ref[0, rows, :]
        steps = range((x * tq + tq - 1) // TK + 1)
        maps = []
        for m in range(2):
            keep = (lane < HEAD_DIM) if m == 0 else (lane >= HEAD_DIM)
            kinds = [[[(m, _tile_kind(x * n_rb + rb - (kp * (TK // LANES) + hf)))
                       for hf in range(TK // LANES)] for rb in range(n_rb)] for kp in steps]
            acc, l = _attend_static(jnp.where(keep, q, jnp.zeros_like(q)),
                                    [tile(k_ref, kp) for kp in steps],
                                    [tile(v_ref, kp) for kp in steps],
                                    kinds, dn_ref, sc_ref.at[(2 * x + m) % sc_ref.shape[0]])
            maps.append(acc / l)
        o = maps[0] - lam * maps[1]
        o = o * lax.rsqrt(jnp.mean(o * o, axis=-1, keepdims=True) + EPS) * sub_ref[...]
        o = o * (1.0 - lambda_init)
        z = z_ref[0, rows, :].astype(F32)
        o_ref[0, rows, :] = (o * (z * _sigmoid(z))).astype(o_ref.dtype)


def _diff_attention(p3, dn, lq1, lk1, lq2, lk2, subln, lambda_init, tq=2 * TQ):
    b, s, _ = p3.shape
    h = DIFF_HEADS
    vec = lambda a: a.reshape(1, -1)
    small = pl.BlockSpec((1, HEAD_DIM), lambda bi, hi: (0, 0))
    slab = lambda first: pl.BlockSpec((1, s, LANES), lambda bi, hi: (bi, 0, first + hi))
    in_specs = [
        slab(0), slab(h), slab(2 * h), slab(3 * h),
        pl.BlockSpec((2, N_BIAS_KINDS, TQ, LANES), lambda bi, hi: (hi, 0, 0, 0)),
        small, small, small, small,
        pl.BlockSpec((1, DIFF_VDIM), lambda bi, hi: (0, hi)),
    ]
    return pl.pallas_call(
        functools.partial(_diff_kernel, lambda_init=lambda_init, tq=tq),
        grid=(b, h),
        in_specs=in_specs,
        out_specs=slab(0),
        out_shape=jax.ShapeDtypeStruct((b, s, h * DIFF_VDIM), BF16),
        scratch_shapes=[pltpu.VMEM((4, s // TK, tq, TK), F32)],
        compiler_params=pltpu.CompilerParams(
            dimension_semantics=("arbitrary", "arbitrary"), vmem_limit_bytes=VMEM_LIMIT),
        name="diff_attention",
    )(p3, p3, p3, p3, dn, vec(lq1), vec(lk1), vec(lq2), vec(lk2), vec(subln))


def _nsa_weight_layout(w_in):
    d = HEAD_DIM
    kv = NSA_GROUPS * d
    width = NSA_HEADS * d
    pts = np.cumsum([width] + [kv] * 6 + [3 * NSA_HEADS, width])
    q, kc, vc, ks, vs, kw, vw, gate, z = [w_in[:, a:b] for a, b in zip([0] + list(pts[:-1]), pts)]
    grp = lambda w, g: w[:, g * d:(g + 1) * d]
    dup = lambda k, v: jnp.concatenate(
        [jnp.concatenate([grp(k, g), grp(k, g), grp(v, g), grp(v, g)], axis=1)
         for g in range(NSA_GROUPS)], axis=1)
    n_gate = 3 * NSA_HPG
    gate_pad = jnp.concatenate(
        [jnp.pad(gate[:, g * n_gate:(g + 1) * n_gate], ((0, 0), (0, LANES - n_gate)))
         for g in range(NSA_GROUPS)], axis=1)
    parts = [("q", q * (d ** -0.5)), ("sel", dup(ks, vs)), ("win", dup(kw, vw)),
             ("gate", gate_pad), ("z", z)]
    col, off = {}, 0
    for name, w in parts:
        col[name] = off
        off += w.shape[1]
    w_main = jnp.concatenate([w for _, w in parts], axis=1).astype(BF16)
    w_cmp = jnp.concatenate(
        [jnp.concatenate([grp(kc, g), grp(vc, g)], axis=1) for g in range(NSA_GROUPS)],
        axis=1).astype(BF16)
    return w_main, w_cmp, col


def _compress_weight_layout(pe_k, w1_k, w2_k, pe_v, w1_v, w2_v):
    d, hdn = HEAD_DIM, CMP_HIDDEN
    pe_kv = jnp.concatenate([pe_k, pe_v], axis=1)
    w1k = w1_k.reshape(CMP_BLOCK, d, hdn)
    w1v = w1_v.reshape(CMP_BLOCK, d, hdn)
    zero = jnp.zeros_like(w1k)
    w1_kv = jnp.concatenate([jnp.concatenate([w1k, zero], axis=2),
                             jnp.concatenate([zero, w1v], axis=2)], axis=1).astype(BF16)
    z2 = jnp.zeros((hdn, 2 * d), w2_k.dtype)
    w2_kv = jnp.concatenate([jnp.concatenate([w2_k, w2_k, z2], axis=1),
                             jnp.concatenate([z2, w2_v, w2_v], axis=1)], axis=0).astype(BF16)
    return pe_kv, w1_kv, w2_kv


def _selection_constants(seq):
    n_cmp_pad = seq // CMP_STRIDE
    n_cmp = (seq - CMP_BLOCK) // CMP_STRIDE + 1
    n_sel = seq // SEL_BLOCK
    cmp_lo = np.arange(n_cmp_pad) * CMP_STRIDE
    sel_lo = np.arange(n_sel) * SEL_BLOCK
    overlap = np.maximum(np.minimum(cmp_lo[:, None] + CMP_BLOCK, sel_lo[None, :] + SEL_BLOCK)
                         - np.maximum(cmp_lo[:, None], sel_lo[None, :]), 0).astype(np.float32) / CMP_BLOCK
    overlap[n_cmp:] = 0.0
    ov = np.zeros((n_cmp_pad, LANES), np.float32)
    ov[:, :n_sel] = overlap
    et = np.zeros((LANES, seq), np.float32)
    et[np.arange(seq) // SEL_BLOCK, np.arange(seq)] = 1.0
    return jnp.asarray(ov), jnp.asarray(et, dtype=BF16)


def kernel(x, rel_bias_table, norm_pre, norm_post, nsa_w_in, nsa_cmp_pe_k, nsa_cmp_w1_k, nsa_cmp_w2_k,
           nsa_cmp_pe_v, nsa_cmp_w1_v, nsa_cmp_w2_v, nsa_w_out, diff_w_in, diff_lambda_q1,
           diff_lambda_k1, diff_lambda_q2, diff_lambda_k2, diff_subln, diff_w_out):
    b, s, d = x.shape
    n = b * s
    assert d == D_MODEL and s % TK == 0 and s // SEL_BLOCK <= LANES and s // CMP_STRIDE == LANES
    dn, bc = _bias_tiles(rel_bias_table, s)
    ov, et = _selection_constants(s)
    x2 = x.reshape(n, d)
    for i in range(DEPTH):
        j = i // 2
        if i % 2 == 0:
            w_main, w_cmp, col = _nsa_weight_layout(nsa_w_in[j])
            p_main, p_cmp = _norm_proj(x2, norm_pre[i], [w_main, w_cmp], [BF16, F32])
            kvc = _compress(p_cmp.reshape(b, s, -1),
                            *_compress_weight_layout(nsa_cmp_pe_k[j], nsa_cmp_w1_k[j], nsa_cmp_w2_k[j],
                                                     nsa_cmp_pe_v[j], nsa_cmp_w1_v[j], nsa_cmp_w2_v[j]))
            o = _nsa_attention(p_main.reshape(b, s, -1), kvc, dn, bc, ov, et, col)
            w_out = nsa_w_out[j]
        else:
            lambda_init = 0.8 - 0.6 * math.exp(-0.3 * i)
            w = diff_w_in[j]
            w = jnp.concatenate([w[:, :d] * (HEAD_DIM ** -0.5), w[:, d:]], axis=1).astype(BF16)
            (p_main,) = _norm_proj(x2, norm_pre[i], [w], [BF16])
            o = _diff_attention(p_main.reshape(b, s, -1), dn, diff_lambda_q1[j], diff_lambda_k1[j],
                                diff_lambda_q2[j], diff_lambda_k2[j], diff_subln[j], lambda_init)
            w_out = diff_w_out[j]
        x2 = _out_proj(o.reshape(n, -1), w_out.astype(BF16), x2, norm_post[i])
    return x2.reshape(b, s, d)
```

```python
import functools
import math

import jax
import jax.numpy as jnp
import numpy as np
from jax import lax
from jax.experimental import pallas as pl
from jax.experimental.pallas import tpu as pltpu

F32 = jnp.float32
BF16 = jnp.bfloat16

D_MODEL = 1024
DEPTH = 2
REL_BUCKETS = 32
REL_MAX_DIST = 128
NSA_HEADS = 16
NSA_GROUPS = 4
NSA_HPG = NSA_HEADS // NSA_GROUPS
HEAD_DIM = 64
CMP_BLOCK = 32
CMP_STRIDE = 16
CMP_HIDDEN = 128
SEL_BLOCK = 64
SEL_TOPK = 8
WINDOW = 512
DIFF_HEADS = 8
DIFF_VDIM = 128
NEG = -1e30
BIG = 1e9
EPS = 1e-6

LANES = 128
TQ = 128
TK = 2 * LANES
N_BIAS_KINDS = 5
VMEM_LIMIT = 52 * 1024 * 1024


def _sigmoid(x):
    return 1.0 / (1.0 + jnp.exp(-x))


def _nt_dot(a, b, precision=None):
    return lax.dot_general(a, b, (((1,), (1,)), ((), ())), precision=precision,
                           preferred_element_type=F32)


def _norm_proj_kernel(x_ref, g_ref, *refs, n_w):
    w_refs, o_refs = refs[:n_w], refs[n_w:]
    x = x_ref[...]
    u = x * lax.rsqrt(jnp.mean(x * x, axis=-1, keepdims=True) + EPS) * g_ref[...]
    ub = u.astype(BF16)
    for w_ref, o_ref in zip(w_refs, o_refs):
        o_ref[...] = jnp.dot(ub, w_ref[...], preferred_element_type=F32).astype(o_ref.dtype)


def _norm_proj(x2, g, weights, out_dtypes, tm=256):
    n, d = x2.shape
    in_specs = [pl.BlockSpec((tm, d), lambda i: (i, 0)), pl.BlockSpec((1, d), lambda i: (0, 0))]
    out_specs, out_shape = [], []
    for w, dt in zip(weights, out_dtypes):
        c = w.shape[1]
        in_specs.append(pl.BlockSpec((d, c), lambda i: (0, 0)))
        out_specs.append(pl.BlockSpec((tm, c), lambda i: (i, 0)))
        out_shape.append(jax.ShapeDtypeStruct((n, c), dt))
    return pl.pallas_call(
        functools.partial(_norm_proj_kernel, n_w=len(weights)),
        grid=(n // tm,),
        in_specs=in_specs,
        out_specs=out_specs,
        out_shape=out_shape,
        compiler_params=pltpu.CompilerParams(
            dimension_semantics=("arbitrary",), vmem_limit_bytes=VMEM_LIMIT),
        name="norm_proj",
    )(x2, g.reshape(1, d), *weights)


def _out_proj_kernel(o_ref, w_ref, x_ref, g_ref, y_ref):
    y = jnp.dot(o_ref[...], w_ref[...], preferred_element_type=F32)
    r = y * lax.rsqrt(jnp.mean(y * y, axis=-1, keepdims=True) + EPS) * g_ref[...]
    y_ref[...] = x_ref[...] + r


def _out_proj(o2, w, x2, g, tm=512):
    n, d = x2.shape
    return pl.pallas_call(
        _out_proj_kernel,
        grid=(n // tm,),
        in_specs=[pl.BlockSpec((tm, o2.shape[1]), lambda i: (i, 0)),
                  pl.BlockSpec(w.shape, lambda i: (0, 0)),
                  pl.BlockSpec((tm, d), lambda i: (i, 0)),
                  pl.BlockSpec((1, d), lambda i: (0, 0))],
        out_specs=pl.BlockSpec((tm, d), lambda i: (i, 0)),
        out_shape=jax.ShapeDtypeStruct((n, d), F32),
        compiler_params=pltpu.CompilerParams(
            dimension_semantics=("arbitrary",), vmem_limit_bytes=VMEM_LIMIT),
        name="out_proj",
    )(o2, w, x2, g.reshape(1, d))


def _bias_kernel(tbl_ref, dn_ref, bc_ref):
    h = pl.program_id(0)
    c31 = tbl_ref[REL_BUCKETS - 1, h]
    max_exact = REL_BUCKETS // 2

    def rel_bias(dist):
        n = jnp.maximum(dist, 0)
        nf = jnp.maximum(n, 1).astype(F32)
        large = max_exact + (jnp.log(nf / max_exact) / math.log(REL_MAX_DIST / max_exact)
                             * (REL_BUCKETS - max_exact)).astype(jnp.int32)
        large = jnp.minimum(large, REL_BUCKETS - 1)
        bucket = jnp.where(n < max_exact, n, large)
        out = jnp.zeros(dist.shape, F32)
        for b in range(REL_BUCKETS):
            out = jnp.where(bucket == b, tbl_ref[b, h], out)
        return out - c31

    i = lax.broadcasted_iota(jnp.int32, (TQ, LANES), 0)
    j = lax.broadcasted_iota(jnp.int32, (TQ, LANES), 1)
    dn_ref[0, 0] = jnp.where(j <= i, rel_bias(i - j), NEG)
    dn_ref[0, 1] = rel_bias(LANES + i - j)
    dn_ref[0, 2] = jnp.zeros((TQ, LANES), F32)
    dn_ref[0, 3] = jnp.where(j > i, 0.0, NEG)
    dn_ref[0, 4] = jnp.full((TQ, LANES), NEG, F32)

    n_cmp = (bc_ref.shape[1] - CMP_BLOCK) // CMP_STRIDE + 1

    def body(r, carry):
        t = r * TQ + i
        d = t - (j * CMP_STRIDE + CMP_BLOCK - 1)
        valid = (d >= 0) & (j < n_cmp)
        bc_ref[0, pl.ds(pl.multiple_of(r * TQ, TQ), TQ), :] = jnp.where(valid, rel_bias(d), NEG)
        return carry

    lax.fori_loop(0, bc_ref.shape[1] // TQ, body, 0)


def _bias_tiles(table, seq):
    n_maps = table.shape[1]
    return pl.pallas_call(
        _bias_kernel,
        grid=(n_maps,),
        in_specs=[pl.BlockSpec(memory_space=pltpu.SMEM)],
        out_specs=[pl.BlockSpec((1, N_BIAS_KINDS, TQ, LANES), lambda h: (h, 0, 0, 0)),
                   pl.BlockSpec((1, seq, LANES), lambda h: (h, 0, 0))],
        out_shape=[jax.ShapeDtypeStruct((n_maps, N_BIAS_KINDS, TQ, LANES), F32),
                   jax.ShapeDtypeStruct((n_maps, seq, LANES), F32)],
        compiler_params=pltpu.CompilerParams(dimension_semantics=("arbitrary",)),
        name="bias_tiles",
    )(table)


def _compress_kernel(x_ref, pe_ref, w1_ref, w2_ref, o_ref):
    half = CMP_BLOCK // 2
    n_chunks = x_ref.shape[1] // CMP_STRIDE
    u = jnp.zeros((n_chunks, 2 * CMP_HIDDEN), F32)
    v = jnp.zeros((n_chunks, 2 * CMP_HIDDEN), F32)
    for l in range(half):
        a = x_ref[0, pl.ds(l, n_chunks, stride=CMP_STRIDE), :]
        u = u + jnp.dot((a + pe_ref[l:l + 1, :]).astype(BF16), w1_ref[l],
                        preferred_element_type=F32)
        v = v + jnp.dot((a + pe_ref[half + l:half + l + 1, :]).astype(BF16), w1_ref[half + l],
                        preferred_element_type=F32)
    hid = u + pltpu.roll(v, n_chunks - 1, 0)
    hid = hid * _sigmoid(hid)
    o_ref[0, 0] = jnp.dot(hid.astype(BF16), w2_ref[...], preferred_element_type=F32).astype(BF16)


def _compress(cmp_in, pe_kv, w1_kv, w2_kv):
    b, s, _ = cmp_in.shape
    n_chunks = s // CMP_STRIDE
    return pl.pallas_call(
        _compress_kernel,
        grid=(b, NSA_GROUPS),
        in_specs=[pl.BlockSpec((1, s, LANES), lambda bi, g: (bi, 0, g)),
                  pl.BlockSpec(pe_kv.shape, lambda bi, g: (0, 0)),
                  pl.BlockSpec(w1_kv.shape, lambda bi, g: (0, 0, 0)),
                  pl.BlockSpec(w2_kv.shape, lambda bi, g: (0, 0))],
        out_specs=pl.BlockSpec((1, 1, n_chunks, 2 * LANES), lambda bi, g: (bi, g, 0, 0)),
        out_shape=jax.ShapeDtypeStruct((b, NSA_GROUPS, n_chunks, 2 * LANES), BF16),
        compiler_params=pltpu.CompilerParams(
            dimension_semantics=("arbitrary", "arbitrary"), vmem_limit_bytes=VMEM_LIMIT),
        name="nsa_compress",
    )(cmp_in, pe_kv, w1_kv, w2_kv)


def _tile_kind(delta, window=False):
    if delta < 0 or (window and delta > WINDOW // LANES):
        return 4
    if window and delta == WINDOW // LANES:
        return 3
    return min(delta, 2)


def _tile_kinds(slot, first_q_tile, n_rb, steps, window=False):
    n_hf = TK // LANES
    return [[[(slot, _tile_kind(first_q_tile + rb - (kp * n_hf + hf), window))
              for hf in range(n_hf)] for rb in range(n_rb)] for kp in steps]


def _attend_static(qx, k_tiles, v_tiles, kinds, bias_ref, sc_ref):
    n_rb = qx.shape[0] // TQ
    n_hf = TK // LANES
    mx = [None] * n_rb
    for i, load_k in enumerate(k_tiles):
        s = _nt_dot(qx, load_k())
        for rb in range(n_rb):
            rows = slice(rb * TQ, (rb + 1) * TQ)
            for hf in range(n_hf):
                cols = slice(hf * LANES, (hf + 1) * LANES)
                slot, kind = kinds[i][rb][hf]
                if kind == 4:
                    sc_ref[i, rows, cols] = jnp.full((TQ, LANES), NEG, F32)
                    continue
                t = s[rows, cols]
                if kind != 2:
                    t = t + bias_ref[slot, kind]
                sc_ref[i, rows, cols] = t
                mx[rb] = t if mx[rb] is None else jnp.maximum(mx[rb], t)
    mb = jnp.concatenate(
        [jnp.broadcast_to(jnp.max(v, axis=-1, keepdims=True), (TQ, LANES)) for v in mx], axis=0)
    acc = jnp.zeros((qx.shape[0], LANES), F32)
    ls = jnp.zeros((qx.shape[0], LANES), F32)
    for i, load_v in enumerate(v_tiles):
        ps = [jnp.exp(sc_ref[i, :, hf * LANES:(hf + 1) * LANES] - mb) for hf in range(n_hf)]
        ls = ls + functools.reduce(lambda a, b: a + b, ps)
        acc = acc + jnp.dot(jnp.concatenate(ps, axis=1).astype(BF16), load_v(),
                            preferred_element_type=F32)
    return acc, jnp.sum(ls, axis=-1, keepdims=True)


def _key_tile(ref, kp):
    return lambda: ref[0, kp * TK:(kp + 1) * TK, :]


def _nsa_kernel(q_ref, ks_ref, vs_ref, kw_ref, vw_ref, kvc_ref, gate_ref, z_ref, dn_ref, bc_ref,
                ovt_ref, en_ref, on_ref, o_ref, sc_s_ref, sc_w_ref, *, tq):
    seq = q_ref.shape[1]
    n_rb = tq // TQ
    n_sel = seq // SEL_BLOCK

    def query_tile(x):
        lane = lax.broadcasted_iota(jnp.int32, (tq, LANES), 1)
        lo_half = lane < HEAD_DIM
        blk = lax.broadcasted_iota(jnp.int32, (n_sel, tq), 0)
        tok = lax.broadcasted_iota(jnp.int32, (n_sel, tq), 1)
        row = lax.broadcasted_iota(jnp.int32, (tq, 1), 0)
        kcc = kvc_ref[0, 0, :, 0:LANES]
        vcc = kvc_ref[0, 0, :, LANES:2 * LANES]
        rows = slice(x * tq, (x + 1) * tq)
        q = q_ref[0, rows, :]
        qx = []
        for j in range(NSA_HPG):
            slab = q[:, (j // 2) * LANES:(j // 2 + 1) * LANES]
            keep = lo_half if j % 2 == 0 else jnp.logical_not(lo_half)
            qx.append(jnp.where(keep, slab, jnp.zeros_like(slab)))

        any_valid = (x * tq + row >= CMP_BLOCK - 1).astype(F32)
        p_sum = jnp.zeros((tq, LANES), F32)
        o_c = []
        for j in range(NSA_HPG):
            sj = _nt_dot(qx[j], kcc) + bc_ref[j, rows, :]
            pj = jnp.exp(sj - jnp.max(sj, axis=-1, keepdims=True))
            pj = pj * (any_valid / jnp.sum(pj, axis=-1, keepdims=True))
            p_sum = p_sum + pj
            o_c.append(jnp.dot(pj.astype(BF16), vcc, preferred_element_type=F32))

        imp = _nt_dot(ovt_ref[...], p_sum, precision=lax.Precision.HIGHEST)[0:n_sel, :]
        cur = lax.shift_right_logical(x * tq + tok, int(math.log2(SEL_BLOCK)))
        forced = (blk == 0) | (blk == cur) | (blk == cur - 1)
        val = jnp.where(forced, BIG, jnp.where(blk > cur, -BIG, imp))
        rank = jnp.zeros((n_sel, tq), F32)
        for other in range(min(n_sel, (x * tq + tq - 1) // SEL_BLOCK + 1)):
            r = val[other:other + 1, :]
            ahead = (r > val) | ((r == val) & (blk > other))
            rank = rank + jnp.where(ahead, 1.0, 0.0)
        block_mask_t = jnp.where(rank < float(min(SEL_TOPK, n_sel)), 0.0, NEG)
        block_mask = jnp.concatenate(
            [block_mask_t, jnp.zeros((LANES - n_sel, tq), F32)], axis=0).T.astype(BF16)

        sel_steps = range((x * tq + tq - 1) // TK + 1)
        win_steps = range(max(0, (x * tq - (WINDOW - 1)) // TK), sel_steps[-1] + 1)
        gate = _sigmoid(gate_ref[0, rows, :].astype(F32))
        outs = []
        for j in range(NSA_HPG):
            call = x * NSA_HPG + j
            acc_s, l_s = _attend_static(
                jnp.concatenate([qx[j], block_mask], axis=1),
                [(lambda kp=kp: jnp.concatenate(
                    [ks_ref[0, kp * TK:(kp + 1) * TK, :], en_ref[kp * TK:(kp + 1) * TK, :]], axis=1))
                 for kp in sel_steps],
                [_key_tile(vs_ref, kp) for kp in sel_steps],
                _tile_kinds(j, x * n_rb, n_rb, sel_steps), dn_ref,
                sc_s_ref.at[call % sc_s_ref.shape[0]])
            acc_w, l_w = _attend_static(
                qx[j], [_key_tile(kw_ref, kp) for kp in win_steps],
                [_key_tile(vw_ref, kp) for kp in win_steps],
                _tile_kinds(j, x * n_rb, n_rb, win_steps, window=True), dn_ref,
                sc_w_ref.at[call % sc_w_ref.shape[0]])
            outs.append(gate[:, 3 * j:3 * j + 1] * o_c[j]
                        + (gate[:, 3 * j + 1:3 * j + 2] / l_s) * acc_s
                        + (gate[:, 3 * j + 2:3 * j + 3] / l_w) * acc_w)
        o = jnp.concatenate([jnp.where(lo_half, outs[0], outs[1]),
                             jnp.where(lo_half, outs[2], outs[3])], axis=1)
        z = z_ref[0, rows, :].astype(F32)
        o_ref[0, rows, :] = (o * (z * _sigmoid(z))).astype(o_ref.dtype)

    for x in range(seq // tq):
        pl.when(on_ref[x] != 0)(functools.partial(query_tile, x))


def _nsa_attention(p3, kvc, dn, bc, ovt, en, col, tq=2 * TQ):
    b, s, _ = p3.shape
    wide = 2 * LANES
    n_win_steps = (WINDOW + tq - 2) // TK + 2

    def slab(first_lane_block, per_group, width=LANES):
        return pl.BlockSpec((1, s, width), lambda g, bi: (bi, 0, first_lane_block + per_group * g))

    in_specs = [
        slab(col["q"] // wide, 1, wide),
        slab(col["sel"] // LANES, 2), slab(col["sel"] // LANES + 1, 2),
        slab(col["win"] // LANES, 2), slab(col["win"] // LANES + 1, 2),
        pl.BlockSpec((1, 1, kvc.shape[2], wide), lambda g, bi: (bi, g, 0, 0)),
        slab(col["gate"] // LANES, 1),
        slab(col["z"] // wide, 1, wide),
        pl.BlockSpec((NSA_HPG, N_BIAS_KINDS, TQ, LANES), lambda g, bi: (g, 0, 0, 0)),
        pl.BlockSpec((NSA_HPG, s, LANES), lambda g, bi: (g, 0, 0)),
        pl.BlockSpec(ovt.shape, lambda g, bi: (0, 0)),
        pl.BlockSpec(en.shape, lambda g, bi: (0, 0)),
        pl.BlockSpec(memory_space=pltpu.SMEM),
    ]
    return pl.pallas_call(
        functools.partial(_nsa_kernel, tq=tq),
        grid=(NSA_GROUPS, b),
        in_specs=in_specs,
        out_specs=pl.BlockSpec((1, s, wide), lambda g, bi: (bi, 0, g)),
        out_shape=jax.ShapeDtypeStruct((b, s, NSA_GROUPS * wide), BF16),
        scratch_shapes=[pltpu.VMEM((4, s // TK, tq, TK), F32),
                        pltpu.VMEM((4, n_win_steps, tq, TK), F32)],
        compiler_params=pltpu.CompilerParams(
            dimension_semantics=("arbitrary", "arbitrary"), vmem_limit_bytes=VMEM_LIMIT),
        name="nsa_attention",
    )(p3, p3, p3, p3, p3, kvc, p3, p3, dn, bc, ovt, en, jnp.ones((s // tq,), jnp.int32))


def _diff_kernel(q_ref, k_ref, v_ref, z_ref, dn_ref, lq1_ref, lk1_ref, lq2_ref, lk2_ref, sub_ref,
                 o_ref, sc_ref, *, lambda_init, tq):
    seq = q_ref.shape[1]
    n_rb = tq // TQ
    lane = lax.broadcasted_iota(jnp.int32, (tq, LANES), 1)
    lam = (jnp.exp(jnp.sum(lq1_ref[...] * lk1_ref[...], axis=-1, keepdims=True))
           - jnp.exp(jnp.sum(lq2_ref[...] * lk2_ref[...], axis=-1, keepdims=True)) + lambda_init)
    for x in range(seq // tq):
        rows = slice(x * tq, (x + 1) * tq)
        q = q_ref[0, rows, :]
        steps = range((x * tq + tq - 1) // TK + 1)
        maps = []
        for m in range(2):
            keep = (lane < HEAD_DIM) if m == 0 else (lane >= HEAD_DIM)
            acc, l = _attend_static(jnp.where(keep, q, jnp.zeros_like(q)),
                                    [_key_tile(k_ref, kp) for kp in steps],
                                    [_key_tile(v_ref, kp) for kp in steps],
                                    _tile_kinds(m, x * n_rb, n_rb, steps), dn_ref,
                                    sc_ref.at[(2 * x + m) % sc_ref.shape[0]])
            maps.append(acc / l)
        o = maps[0] - lam * maps[1]
        o = o * lax.rsqrt(jnp.mean(o * o, axis=-1, keepdims=True) + EPS) * sub_ref[...]
        o = o * (1.0 - lambda_init)
        z = z_ref[0, rows, :].astype(F32)
        o_ref[0, rows, :] = (o * (z * _sigmoid(z))).astype(o_ref.dtype)


def _diff_attention(p3, dn, lq1, lk1, lq2, lk2, subln, lambda_init, tq=2 * TQ):
    b, s, _ = p3.shape
    h = DIFF_HEADS
    vec = lambda a: a.reshape(1, -1)
    small = pl.BlockSpec((1, HEAD_DIM), lambda bi, hi: (0, 0))
    slab = lambda first: pl.BlockSpec((1, s, LANES), lambda bi, hi: (bi, 0, first + hi))
    in_specs = [
        slab(0), slab(h), slab(2 * h), slab(3 * h),
        pl.BlockSpec((2, N_BIAS_KINDS, TQ, LANES), lambda bi, hi: (hi, 0, 0, 0)),
        small, small, small, small,
        pl.BlockSpec((1, DIFF_VDIM), lambda bi, hi: (0, hi)),
    ]
    return pl.pallas_call(
        functools.partial(_diff_kernel, lambda_init=lambda_init, tq=tq),
        grid=(b, h),
        in_specs=in_specs,
        out_specs=slab(0),
        out_shape=jax.ShapeDtypeStruct((b, s, h * DIFF_VDIM), BF16),
        scratch_shapes=[pltpu.VMEM((4, s // TK, tq, TK), F32)],
        compiler_params=pltpu.CompilerParams(
            dimension_semantics=("arbitrary", "arbitrary"), vmem_limit_bytes=VMEM_LIMIT),
        name="diff_attention",
    )(p3, p3, p3, p3, dn, vec(lq1), vec(lk1), vec(lq2), vec(lk2), vec(subln))


def _nsa_weight_layout(w_in):
    d = HEAD_DIM
    kv = NSA_GROUPS * d
    width = NSA_HEADS * d
    pts = np.cumsum([width] + [kv] * 6 + [3 * NSA_HEADS, width])
    q, kc, vc, ks, vs, kw, vw, gate, z = [w_in[:, a:b] for a, b in zip([0] + list(pts[:-1]), pts)]
    grp = lambda w, g: w[:, g * d:(g + 1) * d]
    dup = lambda k, v: jnp.concatenate(
        [jnp.concatenate([grp(k, g), grp(k, g), grp(v, g), grp(v, g)], axis=1)
         for g in range(NSA_GROUPS)], axis=1)
    n_gate = 3 * NSA_HPG
    gate_pad = jnp.concatenate(
        [jnp.pad(gate[:, g * n_gate:(g + 1) * n_gate], ((0, 0), (0, LANES - n_gate)))
         for g in range(NSA_GROUPS)], axis=1)
    parts = [("q", q * (d ** -0.5)), ("sel", dup(ks, vs)), ("win", dup(kw, vw)),
             ("gate", gate_pad), ("z", z)]
    col, off = {}, 0
    for name, w in parts:
        col[name] = off
        off += w.shape[1]
    w_main = jnp.concatenate([w for _, w in parts], axis=1).astype(BF16)
    w_cmp = jnp.concatenate(
        [jnp.concatenate([grp(kc, g), grp(vc, g)], axis=1) for g in range(NSA_GROUPS)],
        axis=1).astype(BF16)
    return w_main, w_cmp, col


def _compress_weight_layout(pe_k, w1_k, w2_k, pe_v, w1_v, w2_v):
    d, hdn = HEAD_DIM, CMP_HIDDEN
    pe_kv = jnp.concatenate([pe_k, pe_v], axis=1)
    w1k = w1_k.reshape(CMP_BLOCK, d, hdn)
    w1v = w1_v.reshape(CMP_BLOCK, d, hdn)
    zero = jnp.zeros_like(w1k)
    w1_kv = jnp.concatenate([jnp.concatenate([w1k, zero], axis=2),
                             jnp.concatenate([zero, w1v], axis=2)], axis=1).astype(BF16)
    z2 = jnp.zeros((hdn, 2 * d), w2_k.dtype)
    w2_kv = jnp.concatenate([jnp.concatenate([w2_k, w2_k, z2], axis=1),
                             jnp.concatenate([z2, w2_v, w2_v], axis=1)], axis=0).astype(BF16)
    return pe_kv, w1_kv, w2_kv


def _selection_constants(seq):
    n_cmp_pad = seq // CMP_STRIDE
    n_cmp = (seq - CMP_BLOCK) // CMP_STRIDE + 1
    n_sel = seq // SEL_BLOCK
    cmp_lo = np.arange(n_cmp_pad) * CMP_STRIDE
    sel_lo = np.arange(n_sel) * SEL_BLOCK
    overlap = np.maximum(np.minimum(cmp_lo[:, None] + CMP_BLOCK, sel_lo[None, :] + SEL_BLOCK)
                         - np.maximum(cmp_lo[:, None], sel_lo[None, :]), 0).astype(np.float32) / CMP_BLOCK
    overlap[n_cmp:] = 0.0
    ovt = np.zeros((LANES, n_cmp_pad), np.float32)
    ovt[:n_sel] = overlap.T
    en = np.zeros((seq, LANES), np.float32)
    en[np.arange(seq), np.arange(seq) // SEL_BLOCK] = 1.0
    return jnp.asarray(ovt), jnp.asarray(en, dtype=BF16)


def kernel(x, rel_bias_table, norm_pre, norm_post, nsa_w_in, nsa_cmp_pe_k, nsa_cmp_w1_k, nsa_cmp_w2_k,
           nsa_cmp_pe_v, nsa_cmp_w1_v, nsa_cmp_w2_v, nsa_w_out, diff_w_in, diff_lambda_q1,
           diff_lambda_k1, diff_lambda_q2, diff_lambda_k2, diff_subln, diff_w_out):
    b, s, d = x.shape
    n = b * s
    assert d == D_MODEL and s % TK == 0 and s // SEL_BLOCK <= LANES and s // CMP_STRIDE == LANES
    dn, bc = _bias_tiles(rel_bias_table, s)
    ovt, en = _selection_constants(s)
    x2 = x.reshape(n, d)
    for i in range(DEPTH):
        j = i // 2
        if i % 2 == 0:
            w_main, w_cmp, col = _nsa_weight_layout(nsa_w_in[j])
            p_main, p_cmp = _norm_proj(x2, norm_pre[i], [w_main, w_cmp], [BF16, F32])
            kvc = _compress(p_cmp.reshape(b, s, -1),
                            *_compress_weight_layout(nsa_cmp_pe_k[j], nsa_cmp_w1_k[j], nsa_cmp_w2_k[j],
                                                     nsa_cmp_pe_v[j], nsa_cmp_w1_v[j], nsa_cmp_w2_v[j]))
            o = _nsa_attention(p_main.reshape(b, s, -1), kvc, dn, bc, ovt, en, col)
            w_out = nsa_w_out[j]
        else:
            lambda_init = 0.8 - 0.6 * math.exp(-0.3 * i)
            w = diff_w_in[j]
            w = jnp.concatenate([w[:, :d] * (HEAD_DIM ** -0.5), w[:, d:]], axis=1).astype(BF16)
            (p_main,) = _norm_proj(x2, norm_pre[i], [w], [BF16])
            o = _diff_attention(p_main.reshape(b, s, -1), dn, diff_lambda_q1[j], diff_lambda_k1[j],
                                diff_lambda_q2[j], diff_lambda_k2[j], diff_subln[j], lambda_init)
            w_out = diff_w_out[j]
        x2 = _out_proj(o.reshape(n, -1), w_out.astype(BF16), x2, norm_post[i])
    return x2.reshape(b, s, d)
```

```python
import functools
import math

import jax
import jax.numpy as jnp
import numpy as np
from jax import lax
from jax.experimental import pallas as pl
from jax.experimental.pallas import tpu as pltpu

F32 = jnp.float32
BF16 = jnp.bfloat16

D_MODEL = 1024
DEPTH = 2
REL_BUCKETS = 32
REL_MAX_DIST = 128
NSA_HEADS = 16
NSA_GROUPS = 4
NSA_HPG = NSA_HEADS // NSA_GROUPS
HEAD_DIM = 64
CMP_BLOCK = 32
CMP_STRIDE = 16
CMP_HIDDEN = 128
SEL_BLOCK = 64
SEL_TOPK = 8
WINDOW = 512
DIFF_HEADS = 8
DIFF_VDIM = 128
NEG = -1e30
BIG = 1e9
EPS = 1e-6

LANES = 128
TQ = 128
TK = 2 * LANES
N_BIAS_KINDS = 5
TILES_PER_REGION = 8
VMEM_LIMIT = 52 * 1024 * 1024


def _sigmoid(x):
    return 1.0 / (1.0 + jnp.exp(-x))


def _nt_dot(a, b, precision=None):
    return lax.dot_general(a, b, (((1,), (1,)), ((), ())), precision=precision,
                           preferred_element_type=F32)


def _resident(shape, index_map):
    return pl.BlockSpec(shape, index_map, pipeline_mode=pl.Buffered(1))


def _norm_proj_kernel(x_ref, g_ref, *refs, n_w, n_wt):
    w_refs, wt_refs, o_refs = refs[:n_w], refs[n_w:n_w + n_wt], refs[n_w + n_wt:]
    x = x_ref[...]
    u = x * lax.rsqrt(jnp.mean(x * x, axis=-1, keepdims=True) + EPS) * g_ref[...]
    ub = u.astype(BF16)
    for w_ref, o_ref in zip(w_refs, o_refs[:n_w]):
        o_ref[...] = jnp.dot(ub, w_ref[...], preferred_element_type=F32).astype(o_ref.dtype)
    for wt_ref, o_ref in zip(wt_refs, o_refs[n_w:]):
        o_ref[0] = _nt_dot(wt_ref[...], ub).astype(o_ref.dtype)


def _norm_proj(x2, g, seq, weights, out_dtypes, weights_t=(), tm=256):
    n, d = x2.shape
    n_st = seq // tm
    in_specs = [pl.BlockSpec((tm, d), lambda i: (i, 0)), pl.BlockSpec((1, d), lambda i: (0, 0))]
    out_specs, out_shape = [], []
    for w, dt in zip(weights, out_dtypes):
        c = w.shape[1]
        in_specs.append(_resident((d, c), lambda i: (0, 0)))
        out_specs.append(pl.BlockSpec((tm, c), lambda i: (i, 0)))
        out_shape.append(jax.ShapeDtypeStruct((n, c), dt))
    for wt in weights_t:
        c = wt.shape[0]
        in_specs.append(_resident((c, d), lambda i: (0, 0)))
        out_specs.append(pl.BlockSpec((1, c, tm), lambda i: (i // n_st, 0, i % n_st)))
        out_shape.append(jax.ShapeDtypeStruct((n // seq, c, seq), BF16))
    return pl.pallas_call(
        functools.partial(_norm_proj_kernel, n_w=len(weights), n_wt=len(weights_t)),
        grid=(n // tm,),
        in_specs=in_specs,
        out_specs=out_specs,
        out_shape=out_shape,
        compiler_params=pltpu.CompilerParams(
            dimension_semantics=("arbitrary",), vmem_limit_bytes=VMEM_LIMIT),
        name="norm_proj",
    )(x2, g.reshape(1, d), *weights, *weights_t)


def _out_proj_kernel(o_ref, w_ref, x_ref, g_ref, y_ref):
    y = jnp.dot(o_ref[...], w_ref[...], preferred_element_type=F32)
    r = y * lax.rsqrt(jnp.mean(y * y, axis=-1, keepdims=True) + EPS) * g_ref[...]
    y_ref[...] = x_ref[...] + r


def _out_proj(o2, w, x2, g, tm=512):
    n, d = x2.shape
    return pl.pallas_call(
        _out_proj_kernel,
        grid=(n // tm,),
        in_specs=[pl.BlockSpec((tm, o2.shape[1]), lambda i: (i, 0)),
                  _resident(w.shape, lambda i: (0, 0)),
                  pl.BlockSpec((tm, d), lambda i: (i, 0)),
                  pl.BlockSpec((1, d), lambda i: (0, 0))],
        out_specs=pl.BlockSpec((tm, d), lambda i: (i, 0)),
        out_shape=jax.ShapeDtypeStruct((n, d), F32),
        compiler_params=pltpu.CompilerParams(
            dimension_semantics=("arbitrary",), vmem_limit_bytes=VMEM_LIMIT),
        name="out_proj",
    )(o2, w, x2, g.reshape(1, d))


def _bias_kernel(tbl_ref, dn_ref, bc_ref):
    h = pl.program_id(0)
    c31 = tbl_ref[REL_BUCKETS - 1, h]
    max_exact = REL_BUCKETS // 2

    def rel_bias(dist):
        n = jnp.maximum(dist, 0)
        nf = jnp.maximum(n, 1).astype(F32)
        large = max_exact + (jnp.log(nf / max_exact) / math.log(REL_MAX_DIST / max_exact)
                             * (REL_BUCKETS - max_exact)).astype(jnp.int32)
        large = jnp.minimum(large, REL_BUCKETS - 1)
        bucket = jnp.where(n < max_exact, n, large)
        out = jnp.zeros(dist.shape, F32)
        for b in range(REL_BUCKETS):
            out = jnp.where(bucket == b, tbl_ref[b, h], out)
        return out - c31

    i = lax.broadcasted_iota(jnp.int32, (TQ, LANES), 0)
    j = lax.broadcasted_iota(jnp.int32, (TQ, LANES), 1)
    dn_ref[0, 0] = jnp.where(j <= i, rel_bias(i - j), NEG)
    dn_ref[0, 1] = rel_bias(LANES + i - j)
    dn_ref[0, 2] = jnp.zeros((TQ, LANES), F32)
    dn_ref[0, 3] = jnp.where(j > i, 0.0, NEG)
    dn_ref[0, 4] = jnp.full((TQ, LANES), NEG, F32)

    n_cmp = (bc_ref.shape[1] - CMP_BLOCK) // CMP_STRIDE + 1

    def body(r, carry):
        t = r * TQ + i
        d = t - (j * CMP_STRIDE + CMP_BLOCK - 1)
        valid = (d >= 0) & (j < n_cmp)
        bc_ref[0, pl.ds(pl.multiple_of(r * TQ, TQ), TQ), :] = jnp.where(valid, rel_bias(d), NEG)
        return carry

    lax.fori_loop(0, bc_ref.shape[1] // TQ, body, 0)


def _bias_tiles(table, seq):
    n_maps = table.shape[1]
    return pl.pallas_call(
        _bias_kernel,
        grid=(n_maps,),
        in_specs=[pl.BlockSpec(memory_space=pltpu.SMEM)],
        out_specs=[pl.BlockSpec((1, N_BIAS_KINDS, TQ, LANES), lambda h: (h, 0, 0, 0)),
                   pl.BlockSpec((1, seq, LANES), lambda h: (h, 0, 0))],
        out_shape=[jax.ShapeDtypeStruct((n_maps, N_BIAS_KINDS, TQ, LANES), F32),
                   jax.ShapeDtypeStruct((n_maps, seq, LANES), F32)],
        compiler_params=pltpu.CompilerParams(dimension_semantics=("arbitrary",)),
        name="bias_tiles",
    )(table)


def _compress_kernel(x_ref, pe_ref, w1_ref, w2_ref, o_ref):
    half = CMP_BLOCK // 2
    n_chunks = x_ref.shape[1] // CMP_STRIDE
    u = jnp.zeros((n_chunks, 2 * CMP_HIDDEN), F32)
    v = jnp.zeros((n_chunks, 2 * CMP_HIDDEN), F32)
    for l in range(half):
        a = x_ref[0, pl.ds(l, n_chunks, stride=CMP_STRIDE), :]
        u = u + jnp.dot((a + pe_ref[l:l + 1, :]).astype(BF16), w1_ref[l],
                        preferred_element_type=F32)
        v = v + jnp.dot((a + pe_ref[half + l:half + l + 1, :]).astype(BF16), w1_ref[half + l],
                        preferred_element_type=F32)
    hid = u + pltpu.roll(v, n_chunks - 1, 0)
    hid = hid * _sigmoid(hid)
    o_ref[0, 0] = jnp.dot(hid.astype(BF16), w2_ref[...], preferred_element_type=F32).astype(BF16)


def _compress(cmp_in, pe_kv, w1_kv, w2_kv):
    b, s, _ = cmp_in.shape
    n_chunks = s // CMP_STRIDE
    return pl.pallas_call(
        _compress_kernel,
        grid=(b, NSA_GROUPS),
        in_specs=[pl.BlockSpec((1, s, LANES), lambda bi, g: (bi, 0, g)),
                  pl.BlockSpec(pe_kv.shape, lambda bi, g: (0, 0)),
                  pl.BlockSpec(w1_kv.shape, lambda bi, g: (0, 0, 0)),
                  pl.BlockSpec(w2_kv.shape, lambda bi, g: (0, 0))],
        out_specs=pl.BlockSpec((1, 1, n_chunks, 2 * LANES), lambda bi, g: (bi, g, 0, 0)),
        out_shape=jax.ShapeDtypeStruct((b, NSA_GROUPS, n_chunks, 2 * LANES), BF16),
        compiler_params=pltpu.CompilerParams(
            dimension_semantics=("arbitrary", "arbitrary"), vmem_limit_bytes=VMEM_LIMIT),
        name="nsa_compress",
    )(cmp_in, pe_kv, w1_kv, w2_kv)


def _tile_kind(delta, window=False):
    if delta < 0 or (window and delta > WINDOW // LANES):
        return 4
    if window and delta == WINDOW // LANES:
        return 3
    return min(delta, 2)


def _tile_kinds(row_blocks, steps, window=False):
    n_hf = TK // LANES
    return [[[(slot, _tile_kind(q_tile - (kp * n_hf + hf), window)) for hf in range(n_hf)]
             for slot, q_tile in row_blocks] for kp in steps]


def _attend_static(qx, kt_tiles, v_tiles, kinds, bias_ref):
    n_rb = qx.shape[0] // TQ
    n_hf = TK // LANES
    mx = [None] * n_rb
    sc = []
    for i, load_kt in enumerate(kt_tiles):
        s = jnp.dot(qx, load_kt(), preferred_element_type=F32)
        step = []
        for rb in range(n_rb):
            tiles = []
            for hf in range(n_hf):
                slot, kind = kinds[i][rb][hf]
                if kind == 4:
                    tiles.append(None)
                    continue
                t = s[rb * TQ:(rb + 1) * TQ, hf * LANES:(hf + 1) * LANES]
                if kind != 2:
                    t = t + bias_ref[slot, kind]
                tiles.append(t)
                mx[rb] = t if mx[rb] is None else jnp.maximum(mx[rb], t)
            step.append(tiles)
        sc.append(step)
    mb = [jnp.broadcast_to(jnp.max(v, axis=-1, keepdims=True), (TQ, LANES)) for v in mx]
    acc = jnp.zeros((qx.shape[0], 2 * LANES), F32)
    for i, load_v in enumerate(v_tiles):
        p = jnp.concatenate(
            [jnp.concatenate([jnp.zeros((TQ, LANES), F32) if t is None else jnp.exp(t - mb[rb])
                              for t in sc[i][rb]], axis=1) for rb in range(n_rb)], axis=0)
        acc = acc + jnp.dot(p.astype(BF16), load_v(), preferred_element_type=F32)
    return acc[:, :LANES], acc[:, LANES:]


def _with_ones(v_tile):
    return jnp.concatenate([v_tile, jnp.ones(v_tile.shape, v_tile.dtype)], axis=1)


def _nsa_kernel(q_ref, kst_ref, kwt_ref, vs_ref, vw_ref, kvc_ref, gate_ref, z_ref, dn_ref, bc_ref,
                ovt_ref, et_ref, gsel_ref, on_ref, o_ref, *, tq):
    seq = q_ref.shape[1]
    n_rb = tq // TQ
    n_sel = seq // SEL_BLOCK

    def query_tile(x):
        lane = lax.broadcasted_iota(jnp.int32, (tq, LANES), 1)
        lo_half = lane < HEAD_DIM
        blk = lax.broadcasted_iota(jnp.int32, (n_sel, tq), 0)
        tok = lax.broadcasted_iota(jnp.int32, (n_sel, tq), 1)
        row = lax.broadcasted_iota(jnp.int32, (tq, 1), 0)
        kcc = kvc_ref[0, 0, :, 0:LANES]
        vcc = kvc_ref[0, 0, :, LANES:2 * LANES]
        rows = slice(x * tq, (x + 1) * tq)
        keys = lambda kp: slice(kp * TK, (kp + 1) * TK)
        q = q_ref[0, rows, :]
        qx = []
        for j in range(NSA_HPG):
            slab = q[:, (j // 2) * LANES:(j // 2 + 1) * LANES]
            keep = lo_half if j % 2 == 0 else jnp.logical_not(lo_half)
            qx.append(jnp.where(keep, slab, jnp.zeros_like(slab)))
        qx_all = jnp.concatenate(qx, axis=0)
        row_blocks = [(j, x * n_rb + r) for j in range(NSA_HPG) for r in range(n_rb)]
        sel_steps = range((x * tq + tq - 1) // TK + 1)
        win_steps = range(max(0, (x * tq - (WINDOW - 1)) // TK), sel_steps[-1] + 1)

        any_valid = (x * tq + row >= CMP_BLOCK - 1).astype(F32)
        p_sum = jnp.zeros((tq, LANES), F32)
        o_c = []
        for j in range(NSA_HPG):
            sj = _nt_dot(qx[j], kcc) + bc_ref[j, rows, :]
            pj = jnp.exp(sj - jnp.max(sj, axis=-1, keepdims=True))
            pj = pj * (any_valid / jnp.sum(pj, axis=-1, keepdims=True))
            p_sum = p_sum + pj
            o_c.append(jnp.dot(pj.astype(BF16), vcc, preferred_element_type=F32))

        num_w, den_w = _attend_static(
            qx_all, [(lambda kp=kp: kwt_ref[0, :, keys(kp)]) for kp in win_steps],
            [(lambda kp=kp: _with_ones(vw_ref[0, keys(kp), :])) for kp in win_steps],
            _tile_kinds(row_blocks, win_steps, window=True), dn_ref)

        imp = _nt_dot(ovt_ref[...], p_sum, precision=lax.Precision.HIGHEST)[0:n_sel, :]
        cur = lax.shift_right_logical(x * tq + tok, int(math.log2(SEL_BLOCK)))
        forced = (blk == 0) | (blk == cur) | (blk == cur - 1)
        val = jnp.where(forced, BIG, jnp.where(blk > cur, -BIG, imp))
        rank = jnp.zeros((n_sel, tq), F32)
        for other in range(min(n_sel, (x * tq + tq - 1) // SEL_BLOCK + 1)):
            r = val[other:other + 1, :]
            ahead = (r > val) | ((r == val) & (blk > other))
            rank = rank + jnp.where(ahead, 1.0, 0.0)
        block_mask_t = jnp.where(rank < float(min(SEL_TOPK, n_sel)), 0.0, NEG)
        block_mask = jnp.concatenate(
            [block_mask_t, jnp.zeros((LANES - n_sel, tq), F32)], axis=0).T.astype(BF16)

        num_s, den_s = _attend_static(
            jnp.concatenate([qx_all, jnp.concatenate([block_mask] * NSA_HPG, axis=0)], axis=1),
            [(lambda kp=kp: jnp.concatenate([kst_ref[0, :, keys(kp)], et_ref[:, keys(kp)]], axis=0))
             for kp in sel_steps],
            [(lambda kp=kp: _with_ones(vs_ref[0, keys(kp), :])) for kp in sel_steps],
            _tile_kinds(row_blocks, sel_steps), dn_ref)

        gates = _sigmoid(jnp.dot(gate_ref[0, rows, :], gsel_ref[...], preferred_element_type=F32))
        pairs = []
        for pr in range(NSA_HPG // 2):
            def merged(per_head):
                return jnp.where(lo_half, per_head(2 * pr), per_head(2 * pr + 1))
            head_rows = lambda a, j: a[j * tq:(j + 1) * tq, :]
            g = lambda i: gates[:, (i * 2 + pr) * LANES:(i * 2 + pr + 1) * LANES]
            pairs.append(g(0) * merged(lambda j: o_c[j])
                         + g(1) * merged(lambda j: head_rows(num_s, j) / head_rows(den_s, j))
                         + g(2) * merged(lambda j: head_rows(num_w, j) / head_rows(den_w, j)))
        o = jnp.concatenate(pairs, axis=1)
        z = z_ref[0, rows, :].astype(F32)
        o_ref[0, rows, :] = (o * (z * _sigmoid(z))).astype(o_ref.dtype)

    def region(first):
        for x in range(first, min(first + TILES_PER_REGION, seq // tq)):
            query_tile(x)

    for first in range(0, seq // tq, TILES_PER_REGION):
        pl.when(on_ref[first] != 0)(functools.partial(region, first))


def _nsa_attention(p3, kt3, kvc, dn, bc, ovt, et, gsel, col, tq=2 * TQ):
    b, s, _ = p3.shape
    wide = 2 * LANES

    def slab(first_lane_block, per_group, width=LANES):
        return pl.BlockSpec((1, s, width), lambda g, bi: (bi, 0, first_lane_block + per_group * g))

    in_specs = [
        slab(col["q"] // wide, 1, wide),
        pl.BlockSpec((1, LANES, s), lambda g, bi: (bi, g, 0)),
        pl.BlockSpec((1, LANES, s), lambda g, bi: (bi, NSA_GROUPS + g, 0)),
        slab(col["vsel"] // LANES, 1), slab(col["vwin"] // LANES, 1),
        pl.BlockSpec((1, 1, kvc.shape[2], wide), lambda g, bi: (bi, g, 0, 0)),
        slab(col["gate"] // LANES, 1),
        slab(col["z"] // wide, 1, wide),
        _resident((NSA_HPG, N_BIAS_KINDS, TQ, LANES), lambda g, bi: (g, 0, 0, 0)),
        _resident((NSA_HPG, s, LANES), lambda g, bi: (g, 0, 0)),
        _resident(ovt.shape, lambda g, bi: (0, 0)),
        _resident(et.shape, lambda g, bi: (0, 0)),
        _resident(gsel.shape, lambda g, bi: (0, 0)),
        pl.BlockSpec(memory_space=pltpu.SMEM),
    ]
    return pl.pallas_call(
        functools.partial(_nsa_kernel, tq=tq),
        grid=(NSA_GROUPS, b),
        in_specs=in_specs,
        out_specs=pl.BlockSpec((1, s, wide), lambda g, bi: (bi, 0, g)),
        out_shape=jax.ShapeDtypeStruct((b, s, NSA_GROUPS * wide), BF16),
        compiler_params=pltpu.CompilerParams(
            dimension_semantics=("arbitrary", "arbitrary"), vmem_limit_bytes=VMEM_LIMIT),
        name="nsa_attention",
    )(p3, kt3, kt3, p3, p3, kvc, p3, p3, dn, bc, ovt, et, gsel, jnp.ones((s // tq,), jnp.int32))


def _diff_kernel(q_ref, kt_ref, v_ref, z_ref, dn_ref, lq1_ref, lk1_ref, lq2_ref, lk2_ref, sub_ref,
                 o_ref, *, lambda_init, tq):
    seq = q_ref.shape[1]
    n_rb = tq // TQ
    lane = lax.broadcasted_iota(jnp.int32, (tq, LANES), 1)
    lam = (jnp.exp(jnp.sum(lq1_ref[...] * lk1_ref[...], axis=-1, keepdims=True))
           - jnp.exp(jnp.sum(lq2_ref[...] * lk2_ref[...], axis=-1, keepdims=True)) + lambda_init)
    keys = lambda kp: slice(kp * TK, (kp + 1) * TK)
    for x in range(seq // tq):
        rows = slice(x * tq, (x + 1) * tq)
        q = q_ref[0, rows, :]
        steps = range((x * tq + tq - 1) // TK + 1)
        zero = jnp.zeros_like(q)
        qx_all = jnp.concatenate([jnp.where(lane < HEAD_DIM, q, zero),
                                  jnp.where(lane >= HEAD_DIM, q, zero)], axis=0)
        row_blocks = [(m, x * n_rb + r) for m in range(2) for r in range(n_rb)]
        num, den = _attend_static(
            qx_all, [(lambda kp=kp: kt_ref[0, :, keys(kp)]) for kp in steps],
            [(lambda kp=kp: _with_ones(v_ref[0, keys(kp), :])) for kp in steps],
            _tile_kinds(row_blocks, steps), dn_ref)
        a = num / den
        o = a[:tq, :] - lam * a[tq:, :]
        o = o * lax.rsqrt(jnp.mean(o * o, axis=-1, keepdims=True) + EPS) * sub_ref[...]
        o = o * (1.0 - lambda_init)
        z = z_ref[0, rows, :].astype(F32)
        o_ref[0, rows, :] = (o * (z * _sigmoid(z))).astype(o_ref.dtype)


def _diff_attention(p3, kt3, dn, lq1, lk1, lq2, lk2, subln, lambda_init, tq=2 * TQ):
    b, s, _ = p3.shape
    h = DIFF_HEADS
    vec = lambda a: a.reshape(1, -1)
    small = pl.BlockSpec((1, HEAD_DIM), lambda bi, hi: (0, 0))
    slab = lambda first: pl.BlockSpec((1, s, LANES), lambda bi, hi: (bi, 0, first + hi))
    in_specs = [
        slab(0),
        pl.BlockSpec((1, LANES, s), lambda bi, hi: (bi, hi, 0)),
        slab(h), slab(2 * h),
        pl.BlockSpec((2, N_BIAS_KINDS, TQ, LANES), lambda bi, hi: (hi, 0, 0, 0)),
        small, small, small, small,
        pl.BlockSpec((1, DIFF_VDIM), lambda bi, hi: (0, hi)),
    ]
    return pl.pallas_call(
        functools.partial(_diff_kernel, lambda_init=lambda_init, tq=tq),
        grid=(b, h),
        in_specs=in_specs,
        out_specs=slab(0),
        out_shape=jax.ShapeDtypeStruct((b, s, h * DIFF_VDIM), BF16),
        compiler_params=pltpu.CompilerParams(
            dimension_semantics=("arbitrary", "arbitrary"), vmem_limit_bytes=VMEM_LIMIT),
        name="diff_attention",
    )(p3, kt3, p3, p3, dn, vec(lq1), vec(lk1), vec(lq2), vec(lk2), vec(subln))


def _nsa_weight_layout(w_in):
    d = HEAD_DIM
    kv = NSA_GROUPS * d
    width = NSA_HEADS * d
    pts = np.cumsum([width] + [kv] * 6 + [3 * NSA_HEADS, width])
    q, kc, vc, ks, vs, kw, vw, gate, z = [w_in[:, a:b] for a, b in zip([0] + list(pts[:-1]), pts)]
    grp = lambda w, g: w[:, g * d:(g + 1) * d]
    dup = lambda w: jnp.concatenate(
        [jnp.concatenate([grp(w, g), grp(w, g)], axis=1) for g in range(NSA_GROUPS)], axis=1)
    n_gate = 3 * NSA_HPG
    gate_pad = jnp.concatenate(
        [jnp.pad(gate[:, g * n_gate:(g + 1) * n_gate], ((0, 0), (0, LANES - n_gate)))
         for g in range(NSA_GROUPS)], axis=1)
    parts = [("q", q * (d ** -0.5)), ("vsel", dup(vs)), ("vwin", dup(vw)),
             ("gate", gate_pad), ("z", z)]
    col, off = {}, 0
    for name, w in parts:
        col[name] = off
        off += w.shape[1]
    w_main = jnp.concatenate([w for _, w in parts], axis=1).astype(BF16)
    w_keys_t = jnp.concatenate([dup(ks), dup(kw)], axis=1).T.astype(BF16)
    w_cmp = jnp.concatenate(
        [jnp.concatenate([grp(kc, g), grp(vc, g)], axis=1) for g in range(NSA_GROUPS)],
        axis=1).astype(BF16)
    return w_main, w_cmp, w_keys_t, col


def _compress_weight_layout(pe_k, w1_k, w2_k, pe_v, w1_v, w2_v):
    d, hdn = HEAD_DIM, CMP_HIDDEN
    pe_kv = jnp.concatenate([pe_k, pe_v], axis=1)
    w1k = w1_k.reshape(CMP_BLOCK, d, hdn)
    w1v = w1_v.reshape(CMP_BLOCK, d, hdn)
    zero = jnp.zeros_like(w1k)
    w1_kv = jnp.concatenate([jnp.concatenate([w1k, zero], axis=2),
                             jnp.concatenate([zero, w1v], axis=2)], axis=1).astype(BF16)
    z2 = jnp.zeros((hdn, 2 * d), w2_k.dtype)
    w2_kv = jnp.concatenate([jnp.concatenate([w2_k, w2_k, z2], axis=1),
                             jnp.concatenate([z2, w2_v, w2_v], axis=1)], axis=0).astype(BF16)
    return pe_kv, w1_kv, w2_kv


def _selection_constants(seq):
    n_cmp_pad = seq // CMP_STRIDE
    n_cmp = (seq - CMP_BLOCK) // CMP_STRIDE + 1
    n_sel = seq // SEL_BLOCK
    cmp_lo = np.arange(n_cmp_pad) * CMP_STRIDE
    sel_lo = np.arange(n_sel) * SEL_BLOCK
    overlap = np.maximum(np.minimum(cmp_lo[:, None] + CMP_BLOCK, sel_lo[None, :] + SEL_BLOCK)
                         - np.maximum(cmp_lo[:, None], sel_lo[None, :]), 0).astype(np.float32) / CMP_BLOCK
    overlap[n_cmp:] = 0.0
    ovt = np.zeros((LANES, n_cmp_pad), np.float32)
    ovt[:n_sel] = overlap.T
    et = np.zeros((LANES, seq), np.float32)
    et[np.arange(seq) // SEL_BLOCK, np.arange(seq)] = 1.0
    n_pairs = NSA_HPG // 2
    gsel = np.zeros((LANES, 3 * n_pairs * LANES), np.float32)
    for i in range(3):
        for pr in range(n_pairs):
            for n in range(LANES):
                gsel[3 * (2 * pr + n // HEAD_DIM) + i, (i * n_pairs + pr) * LANES + n] = 1.0
    return jnp.asarray(ovt), jnp.asarray(et, dtype=BF16), jnp.asarray(gsel, dtype=BF16)


def kernel(x, rel_bias_table, norm_pre, norm_post, nsa_w_in, nsa_cmp_pe_k, nsa_cmp_w1_k, nsa_cmp_w2_k,
           nsa_cmp_pe_v, nsa_cmp_w1_v, nsa_cmp_w2_v, nsa_w_out, diff_w_in, diff_lambda_q1,
           diff_lambda_k1, diff_lambda_q2, diff_lambda_k2, diff_subln, diff_w_out):
    b, s, d = x.shape
    n = b * s
    assert d == D_MODEL and s % TK == 0 and s // SEL_BLOCK <= LANES and s // CMP_STRIDE == LANES
    dn, bc = _bias_tiles(rel_bias_table, s)
    ovt, et, gsel = _selection_constants(s)
    x2 = x.reshape(n, d)
    for i in range(DEPTH):
        j = i // 2
        if i % 2 == 0:
            w_main, w_cmp, w_keys_t, col = _nsa_weight_layout(nsa_w_in[j])
            p_main, p_cmp, keys_t = _norm_proj(x2, norm_pre[i], s, [w_main, w_cmp], [BF16, F32],
                                               [w_keys_t])
            kvc = _compress(p_cmp.reshape(b, s, -1),
                            *_compress_weight_layout(nsa_cmp_pe_k[j], nsa_cmp_w1_k[j], nsa_cmp_w2_k[j],
                                                     nsa_cmp_pe_v[j], nsa_cmp_w1_v[j], nsa_cmp_w2_v[j]))
            o = _nsa_attention(p_main.reshape(b, s, -1), keys_t, kvc, dn, bc, ovt, et, gsel, col)
            w_out = nsa_w_out[j]
        else:
            lambda_init = 0.8 - 0.6 * math.exp(-0.3 * i)
            w = diff_w_in[j]
            w_main = jnp.concatenate([w[:, :d] * (HEAD_DIM ** -0.5), w[:, 2 * d:]], axis=1).astype(BF16)
            p_main, keys_t = _norm_proj(x2, norm_pre[i], s, [w_main], [BF16],
                                        [w[:, d:2 * d].T.astype(BF16)])
            o = _diff_attention(p_main.reshape(b, s, -1), keys_t, dn, diff_lambda_q1[j],
                                diff_lambda_k1[j], diff_lambda_q2[j], diff_lambda_k2[j],
                                diff_subln[j], lambda_init)
            w_out = diff_w_out[j]
        x2 = _out_proj(o.reshape(n, -1), w_out.astype(BF16), x2, norm_post[i])
    return x2.reshape(b, s, d)
```

```python
import functools
import math

import jax
import jax.numpy as jnp
import numpy as np
from jax import lax
from jax.experimental import pallas as pl
from jax.experimental.pallas import tpu as pltpu

F32 = jnp.float32
BF16 = jnp.bfloat16

D_MODEL = 1024
DEPTH = 2
REL_BUCKETS = 32
REL_MAX_DIST = 128
NSA_HEADS = 16
NSA_GROUPS = 4
NSA_HPG = NSA_HEADS // NSA_GROUPS
HEAD_DIM = 64
CMP_BLOCK = 32
CMP_STRIDE = 16
CMP_HIDDEN = 128
SEL_BLOCK = 64
SEL_TOPK = 8
WINDOW = 512
DIFF_HEADS = 8
DIFF_VDIM = 128
NEG = -1e30
BIG = 1e9
EPS = 1e-6

LANES = 128
TQ = 128
TK = 2 * LANES
N_BIAS_KINDS = 5
VMEM_LIMIT = 52 * 1024 * 1024


def _sigmoid(x):
    return 1.0 / (1.0 + jnp.exp(-x))


def _nt_dot(a, b, precision=None):
    return lax.dot_general(a, b, (((1,), (1,)), ((), ())), precision=precision,
                           preferred_element_type=F32)


def _resident(shape, index_map):
    return pl.BlockSpec(shape, index_map, pipeline_mode=pl.Buffered(1))


def _norm_proj_kernel(x_ref, g_ref, *refs, n_w, n_wt):
    w_refs, wt_refs, o_refs = refs[:n_w], refs[n_w:n_w + n_wt], refs[n_w + n_wt:]
    x = x_ref[...]
    u = x * lax.rsqrt(jnp.mean(x * x, axis=-1, keepdims=True) + EPS) * g_ref[...]
    ub = u.astype(BF16)
    for w_ref, o_ref in zip(w_refs, o_refs[:n_w]):
        o_ref[...] = jnp.dot(ub, w_ref[...], preferred_element_type=F32).astype(o_ref.dtype)
    for wt_ref, o_ref in zip(wt_refs, o_refs[n_w:]):
        o_ref[0] = _nt_dot(wt_ref[...], ub).astype(o_ref.dtype)


def _norm_proj(x2, g, seq, weights, out_dtypes, weights_t=(), tm=256):
    n, d = x2.shape
    n_st = seq // tm
    in_specs = [pl.BlockSpec((tm, d), lambda i: (i, 0)), pl.BlockSpec((1, d), lambda i: (0, 0))]
    out_specs, out_shape = [], []
    for w, dt in zip(weights, out_dtypes):
        c = w.shape[1]
        in_specs.append(_resident((d, c), lambda i: (0, 0)))
        out_specs.append(pl.BlockSpec((tm, c), lambda i: (i, 0)))
        out_shape.append(jax.ShapeDtypeStruct((n, c), dt))
    for wt in weights_t:
        c = wt.shape[0]
        in_specs.append(_resident((c, d), lambda i: (0, 0)))
        out_specs.append(pl.BlockSpec((1, c, tm), lambda i: (i // n_st, 0, i % n_st)))
        out_shape.append(jax.ShapeDtypeStruct((n // seq, c, seq), BF16))
    return pl.pallas_call(
        functools.partial(_norm_proj_kernel, n_w=len(weights), n_wt=len(weights_t)),
        grid=(n // tm,),
        in_specs=in_specs,
        out_specs=out_specs,
        out_shape=out_shape,
        compiler_params=pltpu.CompilerParams(
            dimension_semantics=("arbitrary",), vmem_limit_bytes=VMEM_LIMIT),
        name="norm_proj",
    )(x2, g.reshape(1, d), *weights, *weights_t)


def _out_proj_kernel(o_ref, w_ref, x_ref, g_ref, y_ref):
    y = jnp.dot(o_ref[...], w_ref[...], preferred_element_type=F32)
    r = y * lax.rsqrt(jnp.mean(y * y, axis=-1, keepdims=True) + EPS) * g_ref[...]
    y_ref[...] = x_ref[...] + r


def _out_proj(o2, w, x2, g, tm=512):
    n, d = x2.shape
    return pl.pallas_call(
        _out_proj_kernel,
        grid=(n // tm,),
        in_specs=[pl.BlockSpec((tm, o2.shape[1]), lambda i: (i, 0)),
                  _resident(w.shape, lambda i: (0, 0)),
                  pl.BlockSpec((tm, d), lambda i: (i, 0)),
                  pl.BlockSpec((1, d), lambda i: (0, 0))],
        out_specs=pl.BlockSpec((tm, d), lambda i: (i, 0)),
        out_shape=jax.ShapeDtypeStruct((n, d), F32),
        compiler_params=pltpu.CompilerParams(
            dimension_semantics=("arbitrary",), vmem_limit_bytes=VMEM_LIMIT),
        name="out_proj",
    )(o2, w, x2, g.reshape(1, d))


def _bias_kernel(tbl_ref, dn_ref, bc_ref):
    h = pl.program_id(0)
    c31 = tbl_ref[REL_BUCKETS - 1, h]
    max_exact = REL_BUCKETS // 2

    def rel_bias(dist):
        n = jnp.maximum(dist, 0)
        nf = jnp.maximum(n, 1).astype(F32)
        large = max_exact + (jnp.log(nf / max_exact) / math.log(REL_MAX_DIST / max_exact)
                             * (REL_BUCKETS - max_exact)).astype(jnp.int32)
        large = jnp.minimum(large, REL_BUCKETS - 1)
        bucket = jnp.where(n < max_exact, n, large)
        out = jnp.zeros(dist.shape, F32)
        for b in range(REL_BUCKETS):
            out = jnp.where(bucket == b, tbl_ref[b, h], out)
        return out - c31

    i = lax.broadcasted_iota(jnp.int32, (TQ, LANES), 0)
    j = lax.broadcasted_iota(jnp.int32, (TQ, LANES), 1)
    dn_ref[0, 0] = jnp.where(j <= i, rel_bias(i - j), NEG)
    dn_ref[0, 1] = rel_bias(LANES + i - j)
    dn_ref[0, 2] = jnp.zeros((TQ, LANES), F32)
    dn_ref[0, 3] = jnp.where(j > i, 0.0, NEG)
    dn_ref[0, 4] = jnp.full((TQ, LANES), NEG, F32)

    n_cmp = (bc_ref.shape[1] - CMP_BLOCK) // CMP_STRIDE + 1

    def body(r, carry):
        t = r * TQ + i
        d = t - (j * CMP_STRIDE + CMP_BLOCK - 1)
        valid = (d >= 0) & (j < n_cmp)
        bc_ref[0, pl.ds(pl.multiple_of(r * TQ, TQ), TQ), :] = jnp.where(valid, rel_bias(d), NEG)
        return carry

    lax.fori_loop(0, bc_ref.shape[1] // TQ, body, 0)


def _bias_tiles(table, seq):
    n_maps = table.shape[1]
    return pl.pallas_call(
        _bias_kernel,
        grid=(n_maps,),
        in_specs=[pl.BlockSpec(memory_space=pltpu.SMEM)],
        out_specs=[pl.BlockSpec((1, N_BIAS_KINDS, TQ, LANES), lambda h: (h, 0, 0, 0)),
                   pl.BlockSpec((1, seq, LANES), lambda h: (h, 0, 0))],
        out_shape=[jax.ShapeDtypeStruct((n_maps, N_BIAS_KINDS, TQ, LANES), F32),
                   jax.ShapeDtypeStruct((n_maps, seq, LANES), F32)],
        compiler_params=pltpu.CompilerParams(dimension_semantics=("arbitrary",)),
        name="bias_tiles",
    )(table)


def _compress_kernel(x_ref, pe_ref, w1_ref, w2_ref, o_ref):
    half = CMP_BLOCK // 2
    n_chunks = x_ref.shape[1] // CMP_STRIDE
    u = jnp.zeros((n_chunks, 2 * CMP_HIDDEN), F32)
    v = jnp.zeros((n_chunks, 2 * CMP_HIDDEN), F32)
    for l in range(half):
        a = x_ref[0, pl.ds(l, n_chunks, stride=CMP_STRIDE), :]
        u = u + jnp.dot((a + pe_ref[l:l + 1, :]).astype(BF16), w1_ref[l],
                        preferred_element_type=F32)
        v = v + jnp.dot((a + pe_ref[half + l:half + l + 1, :]).astype(BF16), w1_ref[half + l],
                        preferred_element_type=F32)
    hid = u + pltpu.roll(v, n_chunks - 1, 0)
    hid = hid * _sigmoid(hid)
    o_ref[0, 0] = jnp.dot(hid.astype(BF16), w2_ref[...], preferred_element_type=F32).astype(BF16)


def _compress(cmp_in, pe_kv, w1_kv, w2_kv):
    b, s, _ = cmp_in.shape
    n_chunks = s // CMP_STRIDE
    return pl.pallas_call(
        _compress_kernel,
        grid=(b, NSA_GROUPS),
        in_specs=[pl.BlockSpec((1, s, LANES), lambda bi, g: (bi, 0, g)),
                  pl.BlockSpec(pe_kv.shape, lambda bi, g: (0, 0)),
                  pl.BlockSpec(w1_kv.shape, lambda bi, g: (0, 0, 0)),
                  pl.BlockSpec(w2_kv.shape, lambda bi, g: (0, 0))],
        out_specs=pl.BlockSpec((1, 1, n_chunks, 2 * LANES), lambda bi, g: (bi, g, 0, 0)),
        out_shape=jax.ShapeDtypeStruct((b, NSA_GROUPS, n_chunks, 2 * LANES), BF16),
        compiler_params=pltpu.CompilerParams(
            dimension_semantics=("arbitrary", "arbitrary"), vmem_limit_bytes=VMEM_LIMIT),
        name="nsa_compress",
    )(cmp_in, pe_kv, w1_kv, w2_kv)


def _tile_kind(delta, window=False):
    if delta < 0 or (window and delta > WINDOW // LANES):
        return 4
    if window and delta == WINDOW // LANES:
        return 3
    return min(delta, 2)


def _tile_kinds(row_blocks, steps, window=False):
    n_hf = TK // LANES
    return [[[(slot, _tile_kind(q_tile - (kp * n_hf + hf), window)) for hf in range(n_hf)]
             for slot, q_tile in row_blocks] for kp in steps]


def _score_steps(qx, kt_tiles, kinds, bias_ref, st):
    n_rb = qx.shape[0] // TQ
    n_hf = TK // LANES
    st["mx"] = [None] * n_rb
    st["sc"] = []
    for i, load_kt in enumerate(kt_tiles):
        s = jnp.dot(qx, load_kt(), preferred_element_type=F32)
        step = []
        for rb in range(n_rb):
            tiles = []
            for hf in range(n_hf):
                slot, kind = kinds[i][rb][hf]
                if kind == 4:
                    tiles.append(None)
                    continue
                t = s[rb * TQ:(rb + 1) * TQ, hf * LANES:(hf + 1) * LANES]
                if kind != 2:
                    t = t + bias_ref[slot, kind]
                tiles.append(t)
                st["mx"][rb] = t if st["mx"][rb] is None else jnp.maximum(st["mx"][rb], t)
            step.append(tiles)
        st["sc"].append(step)
        yield


def _weight_steps(v_tiles, st):
    n_rb = len(st["mx"])
    mb = [jnp.broadcast_to(jnp.max(v, axis=-1, keepdims=True), (TQ, LANES)) for v in st["mx"]]
    st["acc"] = jnp.zeros((n_rb * TQ, 2 * LANES), F32)
    for i, load_v in enumerate(v_tiles):
        p = jnp.concatenate(
            [jnp.concatenate([jnp.zeros((TQ, LANES), F32) if t is None else jnp.exp(t - mb[rb])
                              for t in st["sc"][i][rb]], axis=1) for rb in range(n_rb)], axis=0)
        st["acc"] = st["acc"] + jnp.dot(p.astype(BF16), load_v(), preferred_element_type=F32)
        yield


def _chain(*gens):
    for g in gens:
        yield from g


def _pipeline_tiles(order, n_steps, setup, score_steps, weight_steps, finish):
    n = len(order)
    state = {}
    qk_at = [0, 0]

    def qk_stream():
        state[order[0]] = setup(order[0])
        for i, x in enumerate(order):
            if i + 1 < n:
                state[order[i + 1]] = setup(order[i + 1])
            qk_at[:] = [i, 0]
            for _ in score_steps(x, state[x]):
                qk_at[1] += 1
                yield
        qk_at[:] = [n, 0]

    def pv_stream():
        for i, x in enumerate(order):
            while not (qk_at[0] > i + 1 or qk_at[0] >= n
                       or (qk_at[0] == i + 1 and 2 * qk_at[1] >= n_steps(order[i + 1]))):
                yield
            yield from weight_steps(x, state[x])
            finish(x, state.pop(x))

    live = [qk_stream(), pv_stream()]
    while live:
        for g in list(live):
            try:
                next(g)
            except StopIteration:
                live.remove(g)


def _tile_order(n):
    return list(range(0, n, 2)) + list(range(n - 1 - n % 2, 0, -2))


def _with_ones(v_tile):
    return jnp.concatenate([v_tile, jnp.ones(v_tile.shape, v_tile.dtype)], axis=1)


def _nsa_kernel(q_ref, kst_ref, kwt_ref, vs_ref, vw_ref, kvc_ref, gate_ref, z_ref, dn_ref, bc_ref,
                ovt_ref, et_ref, gsel_ref, o_ref, *, tq):
    seq = q_ref.shape[1]
    n_rb = tq // TQ
    n_sel = seq // SEL_BLOCK

    keys = lambda kp: slice(kp * TK, (kp + 1) * TK)

    def setup(x):
        lane = lax.broadcasted_iota(jnp.int32, (tq, LANES), 1)
        lo_half = lane < HEAD_DIM
        blk = lax.broadcasted_iota(jnp.int32, (n_sel, tq), 0)
        tok = lax.broadcasted_iota(jnp.int32, (n_sel, tq), 1)
        row = lax.broadcasted_iota(jnp.int32, (tq, 1), 0)
        kcc = kvc_ref[0, 0, :, 0:LANES]
        vcc = kvc_ref[0, 0, :, LANES:2 * LANES]
        rows = slice(x * tq, (x + 1) * tq)
        q = q_ref[0, rows, :]
        qx = []
        for j in range(NSA_HPG):
            slab = q[:, (j // 2) * LANES:(j // 2 + 1) * LANES]
            keep = lo_half if j % 2 == 0 else jnp.logical_not(lo_half)
            qx.append(jnp.where(keep, slab, jnp.zeros_like(slab)))
        qx_all = jnp.concatenate(qx, axis=0)

        any_valid = (x * tq + row >= CMP_BLOCK - 1).astype(F32)
        p_sum = jnp.zeros((tq, LANES), F32)
        o_c = []
        for j in range(NSA_HPG):
            sj = _nt_dot(qx[j], kcc) + bc_ref[j, rows, :]
            pj = jnp.exp(sj - jnp.max(sj, axis=-1, keepdims=True))
            pv = jnp.dot(pj.astype(BF16), _with_ones(vcc), preferred_element_type=F32)
            inv = any_valid / pv[:, LANES:]
            p_sum = p_sum + pj * inv
            o_c.append(pv[:, :LANES] * inv)

        imp = _nt_dot(ovt_ref[...], p_sum, precision=lax.Precision.HIGHEST)[0:n_sel, :]
        cur = lax.shift_right_logical(x * tq + tok, int(math.log2(SEL_BLOCK)))
        forced = (blk == 0) | (blk == cur) | (blk == cur - 1)
        val = jnp.where(forced, BIG, jnp.where(blk > cur, -BIG, imp))
        rank = jnp.zeros((n_sel, tq), F32)
        for other in range(min(n_sel, (x * tq + tq - 1) // SEL_BLOCK + 1)):
            r = val[other:other + 1, :]
            ahead = (r > val) | ((r == val) & (blk > other))
            rank = rank + jnp.where(ahead, 1.0, 0.0)
        block_mask_t = jnp.where(rank < float(min(SEL_TOPK, n_sel)), 0.0, NEG)
        block_mask = jnp.concatenate(
            [block_mask_t, jnp.zeros((LANES - n_sel, tq), F32)], axis=0).T.astype(BF16)

        q_sel = jnp.concatenate([qx_all, jnp.concatenate([block_mask] * NSA_HPG, axis=0)], axis=1)
        return dict(q_win=qx_all, q_sel=q_sel, o_c=o_c, win={}, sel={})

    def steps_of(x):
        sel_steps = range((x * tq + tq - 1) // TK + 1)
        win_steps = range(max(0, (x * tq - (WINDOW - 1)) // TK), sel_steps[-1] + 1)
        return win_steps, sel_steps

    def score_steps(x, st):
        win_steps, sel_steps = steps_of(x)
        row_blocks = [(j, x * n_rb + r) for j in range(NSA_HPG) for r in range(n_rb)]
        return _chain(
            _score_steps(st["q_win"], [(lambda kp=kp: kwt_ref[0, :, keys(kp)]) for kp in win_steps],
                         _tile_kinds(row_blocks, win_steps, window=True), dn_ref, st["win"]),
            _score_steps(st["q_sel"],
                         [(lambda kp=kp: jnp.concatenate(
                             [kst_ref[0, :, keys(kp)], et_ref[:, keys(kp)]], axis=0))
                          for kp in sel_steps],
                         _tile_kinds(row_blocks, sel_steps), dn_ref, st["sel"]))

    def weight_steps(x, st):
        win_steps, sel_steps = steps_of(x)
        return _chain(
            _weight_steps([(lambda kp=kp: _with_ones(vw_ref[0, keys(kp), :])) for kp in win_steps],
                          st["win"]),
            _weight_steps([(lambda kp=kp: _with_ones(vs_ref[0, keys(kp), :])) for kp in sel_steps],
                          st["sel"]))

    def finish(x, st):
        rows = slice(x * tq, (x + 1) * tq)
        lo_half = lax.broadcasted_iota(jnp.int32, (tq, LANES), 1) < HEAD_DIM
        gates = _sigmoid(jnp.dot(gate_ref[0, rows, :], gsel_ref[...], preferred_element_type=F32))
        ratio = {}
        for name in ("win", "sel"):
            acc = st[name]["acc"]
            ratio[name] = acc[:, :LANES] / acc[:, LANES:]
        pairs = []
        for pr in range(NSA_HPG // 2):
            def merged(per_head):
                return jnp.where(lo_half, per_head(2 * pr), per_head(2 * pr + 1))
            head_rows = lambda a, j: a[j * tq:(j + 1) * tq, :]
            g = lambda i: gates[:, (i * 2 + pr) * LANES:(i * 2 + pr + 1) * LANES]
            pairs.append(g(0) * merged(lambda j: st["o_c"][j])
                         + g(1) * merged(lambda j: head_rows(ratio["sel"], j))
                         + g(2) * merged(lambda j: head_rows(ratio["win"], j)))
        o = jnp.concatenate(pairs, axis=1)
        z = z_ref[0, rows, :].astype(F32)
        o_ref[0, rows, :] = (o * (z * _sigmoid(z))).astype(o_ref.dtype)

    _pipeline_tiles(_tile_order(seq // tq), lambda x: sum(len(s) for s in steps_of(x)),
                    setup, score_steps, weight_steps, finish)


def _nsa_attention(p3, kt3, kvc, dn, bc, ovt, et, gsel, col, tq=2 * TQ):
    b, s, _ = p3.shape
    wide = 2 * LANES

    def slab(first_lane_block, per_group, width=LANES):
        return pl.BlockSpec((1, s, width), lambda g, bi: (bi, 0, first_lane_block + per_group * g))

    in_specs = [
        slab(col["q"] // wide, 1, wide),
        pl.BlockSpec((1, LANES, s), lambda g, bi: (bi, g, 0)),
        pl.BlockSpec((1, LANES, s), lambda g, bi: (bi, NSA_GROUPS + g, 0)),
        slab(col["vsel"] // LANES, 1), slab(col["vwin"] // LANES, 1),
        pl.BlockSpec((1, 1, kvc.shape[2], wide), lambda g, bi: (bi, g, 0, 0)),
        slab(col["gate"] // LANES, 1),
        slab(col["z"] // wide, 1, wide),
        _resident((NSA_HPG, N_BIAS_KINDS, TQ, LANES), lambda g, bi: (g, 0, 0, 0)),
        _resident((NSA_HPG, s, LANES), lambda g, bi: (g, 0, 0)),
        _resident(ovt.shape, lambda g, bi: (0, 0)),
        _resident(et.shape, lambda g, bi: (0, 0)),
        _resident(gsel.shape, lambda g, bi: (0, 0)),
    ]
    return pl.pallas_call(
        functools.partial(_nsa_kernel, tq=tq),
        grid=(NSA_GROUPS, b),
        in_specs=in_specs,
        out_specs=pl.BlockSpec((1, s, wide), lambda g, bi: (bi, 0, g)),
        out_shape=jax.ShapeDtypeStruct((b, s, NSA_GROUPS * wide), BF16),
        compiler_params=pltpu.CompilerParams(
            dimension_semantics=("arbitrary", "arbitrary"), vmem_limit_bytes=VMEM_LIMIT),
        name="nsa_attention",
    )(p3, kt3, kt3, p3, p3, kvc, p3, p3, dn, bc, ovt, et, gsel)


def _diff_kernel(q_ref, kt_ref, v_ref, z_ref, dn_ref, lq1_ref, lk1_ref, lq2_ref, lk2_ref, sub_ref,
                 o_ref, *, lambda_init, tq):
    seq = q_ref.shape[1]
    n_rb = tq // TQ
    lane = lax.broadcasted_iota(jnp.int32, (tq, LANES), 1)
    lam = (jnp.exp(jnp.sum(lq1_ref[...] * lk1_ref[...], axis=-1, keepdims=True))
           - jnp.exp(jnp.sum(lq2_ref[...] * lk2_ref[...], axis=-1, keepdims=True)) + lambda_init)
    keys = lambda kp: slice(kp * TK, (kp + 1) * TK)
    steps_of = lambda x: range((x * tq + tq - 1) // TK + 1)

    def setup(x):
        q = q_ref[0, x * tq:(x + 1) * tq, :]
        zero = jnp.zeros_like(q)
        return dict(q=jnp.concatenate([jnp.where(lane < HEAD_DIM, q, zero),
                                       jnp.where(lane >= HEAD_DIM, q, zero)], axis=0))

    def score_steps(x, st):
        row_blocks = [(m, x * n_rb + r) for m in range(2) for r in range(n_rb)]
        return _score_steps(st["q"], [(lambda kp=kp: kt_ref[0, :, keys(kp)]) for kp in steps_of(x)],
                            _tile_kinds(row_blocks, steps_of(x)), dn_ref, st)

    def weight_steps(x, st):
        return _weight_steps([(lambda kp=kp: _with_ones(v_ref[0, keys(kp), :]))
                              for kp in steps_of(x)], st)

    def finish(x, st):
        rows = slice(x * tq, (x + 1) * tq)
        a = st["acc"][:, :LANES] / st["acc"][:, LANES:]
        o = a[:tq, :] - lam * a[tq:, :]
        o = o * lax.rsqrt(jnp.mean(o * o, axis=-1, keepdims=True) + EPS) * sub_ref[...]
        o = o * (1.0 - lambda_init)
        z = z_ref[0, rows, :].astype(F32)
        o_ref[0, rows, :] = (o * (z * _sigmoid(z))).astype(o_ref.dtype)

    _pipeline_tiles(_tile_order(seq // tq), lambda x: len(steps_of(x)),
                    setup, score_steps, weight_steps, finish)


def _diff_attention(p3, kt3, dn, lq1, lk1, lq2, lk2, subln, lambda_init, tq=2 * TQ):
    b, s, _ = p3.shape
    h = DIFF_HEADS
    vec = lambda a: a.reshape(1, -1)
    small = pl.BlockSpec((1, HEAD_DIM), lambda bi, hi: (0, 0))
    slab = lambda first: pl.BlockSpec((1, s, LANES), lambda bi, hi: (bi, 0, first + hi))
    in_specs = [
        slab(0),
        pl.BlockSpec((1, LANES, s), lambda bi, hi: (bi, hi, 0)),
        slab(h), slab(2 * h),
        pl.BlockSpec((2, N_BIAS_KINDS, TQ, LANES), lambda bi, hi: (hi, 0, 0, 0)),
        small, small, small, small,
        pl.BlockSpec((1, DIFF_VDIM), lambda bi, hi: (0, hi)),
    ]
    return pl.pallas_call(
        functools.partial(_diff_kernel, lambda_init=lambda_init, tq=tq),
        grid=(b, h),
        in_specs=in_specs,
        out_specs=slab(0),
        out_shape=jax.ShapeDtypeStruct((b, s, h * DIFF_VDIM), BF16),
        compiler_params=pltpu.CompilerParams(
            dimension_semantics=("arbitrary", "arbitrary"), vmem_limit_bytes=VMEM_LIMIT),
        name="diff_attention",
    )(p3, kt3, p3, p3, dn, vec(lq1), vec(lk1), vec(lq2), vec(lk2), vec(subln))


def _nsa_weight_layout(w_in):
    d = HEAD_DIM
    kv = NSA_GROUPS * d
    width = NSA_HEADS * d
    pts = np.cumsum([width] + [kv] * 6 + [3 * NSA_HEADS, width])
    q, kc, vc, ks, vs, kw, vw, gate, z = [w_in[:, a:b] for a, b in zip([0] + list(pts[:-1]), pts)]
    grp = lambda w, g: w[:, g * d:(g + 1) * d]
    dup = lambda w: jnp.concatenate(
        [jnp.concatenate([grp(w, g), grp(w, g)], axis=1) for g in range(NSA_GROUPS)], axis=1)
    n_gate = 3 * NSA_HPG
    gate_pad = jnp.concatenate(
        [jnp.pad(gate[:, g * n_gate:(g + 1) * n_gate], ((0, 0), (0, LANES - n_gate)))
         for g in range(NSA_GROUPS)], axis=1)
    parts = [("q", q * (d ** -0.5)), ("vsel", dup(vs)), ("vwin", dup(vw)),
             ("gate", gate_pad), ("z", z)]
    col, off = {}, 0
    for name, w in parts:
        col[name] = off
        off += w.shape[1]
    w_main = jnp.concatenate([w for _, w in parts], axis=1).astype(BF16)
    w_keys_t = jnp.concatenate([dup(ks), dup(kw)], axis=1).T.astype(BF16)
    w_cmp = jnp.concatenate(
        [jnp.concatenate([grp(kc, g), grp(vc, g)], axis=1) for g in range(NSA_GROUPS)],
        axis=1).astype(BF16)
    return w_main, w_cmp, w_keys_t, col


def _compress_weight_layout(pe_k, w1_k, w2_k, pe_v, w1_v, w2_v):
    d, hdn = HEAD_DIM, CMP_HIDDEN
    pe_kv = jnp.concatenate([pe_k, pe_v], axis=1)
    w1k = w1_k.reshape(CMP_BLOCK, d, hdn)
    w1v = w1_v.reshape(CMP_BLOCK, d, hdn)
    zero = jnp.zeros_like(w1k)
    w1_kv = jnp.concatenate([jnp.concatenate([w1k, zero], axis=2),
                             jnp.concatenate([zero, w1v], axis=2)], axis=1).astype(BF16)
    z2 = jnp.zeros((hdn, 2 * d), w2_k.dtype)
    w2_kv = jnp.concatenate([jnp.concatenate([w2_k, w2_k, z2], axis=1),
                             jnp.concatenate([z2, w2_v, w2_v], axis=1)], axis=0).astype(BF16)
    return pe_kv, w1_kv, w2_kv


def _selection_constants(seq):
    n_cmp_pad = seq // CMP_STRIDE
    n_cmp = (seq - CMP_BLOCK) // CMP_STRIDE + 1
    n_sel = seq // SEL_BLOCK
    cmp_lo = np.arange(n_cmp_pad) * CMP_STRIDE
    sel_lo = np.arange(n_sel) * SEL_BLOCK
    overlap = np.maximum(np.minimum(cmp_lo[:, None] + CMP_BLOCK, sel_lo[None, :] + SEL_BLOCK)
                         - np.maximum(cmp_lo[:, None], sel_lo[None, :]), 0).astype(np.float32) / CMP_BLOCK
    overlap[n_cmp:] = 0.0
    ovt = np.zeros((LANES, n_cmp_pad), np.float32)
    ovt[:n_sel] = overlap.T
    et = np.zeros((LANES, seq), np.float32)
    et[np.arange(seq) // SEL_BLOCK, np.arange(seq)] = 1.0
    n_pairs = NSA_HPG // 2
    gsel = np.zeros((LANES, 3 * n_pairs * LANES), np.float32)
    for i in range(3):
        for pr in range(n_pairs):
            for n in range(LANES):
                gsel[3 * (2 * pr + n // HEAD_DIM) + i, (i * n_pairs + pr) * LANES + n] = 1.0
    return jnp.asarray(ovt), jnp.asarray(et, dtype=BF16), jnp.asarray(gsel, dtype=BF16)


def kernel(x, rel_bias_table, norm_pre, norm_post, nsa_w_in, nsa_cmp_pe_k, nsa_cmp_w1_k, nsa_cmp_w2_k,
           nsa_cmp_pe_v, nsa_cmp_w1_v, nsa_cmp_w2_v, nsa_w_out, diff_w_in, diff_lambda_q1,
           diff_lambda_k1, diff_lambda_q2, diff_lambda_k2, diff_subln, diff_w_out):
    b, s, d = x.shape
    n = b * s
    assert d == D_MODEL and s % TK == 0 and s // SEL_BLOCK <= LANES and s // CMP_STRIDE == LANES
    dn, bc = _bias_tiles(rel_bias_table, s)
    ovt, et, gsel = _selection_constants(s)
    x2 = x.reshape(n, d)
    for i in range(DEPTH):
        j = i // 2
        if i % 2 == 0:
            w_main, w_cmp, w_keys_t, col = _nsa_weight_layout(nsa_w_in[j])
            p_main, p_cmp, keys_t = _norm_proj(x2, norm_pre[i], s, [w_main, w_cmp], [BF16, F32],
                                               [w_keys_t])
            kvc = _compress(p_cmp.reshape(b, s, -1),
                            *_compress_weight_layout(nsa_cmp_pe_k[j], nsa_cmp_w1_k[j], nsa_cmp_w2_k[j],
                                                     nsa_cmp_pe_v[j], nsa_cmp_w1_v[j], nsa_cmp_w2_v[j]))
            o = _nsa_attention(p_main.reshape(b, s, -1), keys_t, kvc, dn, bc, ovt, et, gsel, col)
            w_out = nsa_w_out[j]
        else:
            lambda_init = 0.8 - 0.6 * math.exp(-0.3 * i)
            w = diff_w_in[j]
            w_main = jnp.concatenate([w[:, :d] * (HEAD_DIM ** -0.5), w[:, 2 * d:]], axis=1).astype(BF16)
            p_main, keys_t = _norm_proj(x2, norm_pre[i], s, [w_main], [BF16],
                                        [w[:, d:2 * d].T.astype(BF16)])
            o = _diff_attention(p_main.reshape(b, s, -1), keys_t, dn, diff_lambda_q1[j],
                                diff_lambda_k1[j], diff_lambda_q2[j], diff_lambda_k2[j],
                                diff_subln[j], lambda_init)
            w_out = diff_w_out[j]
        x2 = _out_proj(o.reshape(n, -1), w_out.astype(BF16), x2, norm_post[i])
    return x2.reshape(b, s, d)
```

```python
import functools
import math

import jax
import jax.numpy as jnp
import numpy as np
from jax import lax
from jax.experimental import pallas as pl
from jax.experimental.pallas import tpu as pltpu

F32 = jnp.float32
BF16 = jnp.bfloat16

D_MODEL = 1024
DEPTH = 2
REL_BUCKETS = 32
REL_MAX_DIST = 128
NSA_HEADS = 16
NSA_GROUPS = 4
NSA_HPG = NSA_HEADS // NSA_GROUPS
HEAD_DIM = 64
CMP_BLOCK = 32
CMP_STRIDE = 16
CMP_HIDDEN = 128
SEL_BLOCK = 64
SEL_TOPK = 8
WINDOW = 512
DIFF_HEADS = 8
DIFF_VDIM = 128
NEG = -1e30
BIG = 1e9
EPS = 1e-6

LANES = 128
TQ = 128
TK = 2 * LANES
N_BIAS_KINDS = 5
VMEM_LIMIT = 52 * 1024 * 1024


def _sigmoid(x):
    return 1.0 / (1.0 + jnp.exp(-x))


def _nt_dot(a, b, precision=None):
    return lax.dot_general(a, b, (((1,), (1,)), ((), ())), precision=precision,
                           preferred_element_type=F32)


def _resident(shape, index_map):
    return pl.BlockSpec(shape, index_map, pipeline_mode=pl.Buffered(1))


def _norm_proj_kernel(x_ref, g_ref, *refs, n_w, n_wt):
    w_refs, wt_refs, o_refs = refs[:n_w], refs[n_w:n_w + n_wt], refs[n_w + n_wt:]
    x = x_ref[...]
    u = x * lax.rsqrt(jnp.mean(x * x, axis=-1, keepdims=True) + EPS) * g_ref[...]
    ub = u.astype(BF16)
    for w_ref, o_ref in zip(w_refs, o_refs[:n_w]):
        o_ref[...] = jnp.dot(ub, w_ref[...], preferred_element_type=F32).astype(o_ref.dtype)
    for wt_ref, o_ref in zip(wt_refs, o_refs[n_w:]):
        o_ref[0] = _nt_dot(wt_ref[...], ub).astype(o_ref.dtype)


def _norm_proj(x2, g, seq, weights, out_dtypes, weights_t=(), tm=256):
    n, d = x2.shape
    n_st = seq // tm
    in_specs = [pl.BlockSpec((tm, d), lambda i: (i, 0)), pl.BlockSpec((1, d), lambda i: (0, 0))]
    out_specs, out_shape = [], []
    for w, dt in zip(weights, out_dtypes):
        c = w.shape[1]
        in_specs.append(_resident((d, c), lambda i: (0, 0)))
        out_specs.append(pl.BlockSpec((tm, c), lambda i: (i, 0)))
        out_shape.append(jax.ShapeDtypeStruct((n, c), dt))
    for wt in weights_t:
        c = wt.shape[0]
        in_specs.append(_resident((c, d), lambda i: (0, 0)))
        out_specs.append(pl.BlockSpec((1, c, tm), lambda i: (i // n_st, 0, i % n_st)))
        out_shape.append(jax.ShapeDtypeStruct((n // seq, c, seq), BF16))
    return pl.pallas_call(
        functools.partial(_norm_proj_kernel, n_w=len(weights), n_wt=len(weights_t)),
        grid=(n // tm,),
        in_specs=in_specs,
        out_specs=out_specs,
        out_shape=out_shape,
        compiler_params=pltpu.CompilerParams(
            dimension_semantics=("arbitrary",), vmem_limit_bytes=VMEM_LIMIT),
        name="norm_proj",
    )(x2, g.reshape(1, d), *weights, *weights_t)


def _out_proj_kernel(o_ref, w_ref, x_ref, g_ref, y_ref):
    y = jnp.dot(o_ref[...], w_ref[...], preferred_element_type=F32)
    r = y * lax.rsqrt(jnp.mean(y * y, axis=-1, keepdims=True) + EPS) * g_ref[...]
    y_ref[...] = x_ref[...] + r


def _out_proj(o2, w, x2, g, tm=512):
    n, d = x2.shape
    return pl.pallas_call(
        _out_proj_kernel,
        grid=(n // tm,),
        in_specs=[pl.BlockSpec((tm, o2.shape[1]), lambda i: (i, 0)),
                  _resident(w.shape, lambda i: (0, 0)),
                  pl.BlockSpec((tm, d), lambda i: (i, 0)),
                  pl.BlockSpec((1, d), lambda i: (0, 0))],
        out_specs=pl.BlockSpec((tm, d), lambda i: (i, 0)),
        out_shape=jax.ShapeDtypeStruct((n, d), F32),
        compiler_params=pltpu.CompilerParams(
            dimension_semantics=("arbitrary",), vmem_limit_bytes=VMEM_LIMIT),
        name="out_proj",
    )(o2, w, x2, g.reshape(1, d))


def _bucket_starts(max_dist):
    max_exact = REL_BUCKETS // 2
    n = np.arange(max_dist)
    nf = np.maximum(n, 1).astype(np.float32)
    large = max_exact + (np.log(nf / max_exact) / np.float32(math.log(REL_MAX_DIST / max_exact))
                         * (REL_BUCKETS - max_exact)).astype(np.int32)
    bucket = np.where(n < max_exact, n, np.minimum(large, REL_BUCKETS - 1))
    assert (np.diff(bucket) >= 0).all()
    return [int(np.argmax(bucket >= b)) if (bucket >= b).any() else None
            for b in range(REL_BUCKETS)]


def _bias_kernel(tbl_ref, dn_ref, bc_ref):
    h = pl.program_id(0)
    c31 = tbl_ref[REL_BUCKETS - 1, h]
    starts = _bucket_starts(bc_ref.shape[1])

    def rel_bias(dist):
        n = jnp.maximum(dist, 0)
        out = jnp.full(dist.shape, tbl_ref[0, h], F32)
        for b in range(1, REL_BUCKETS):
            if starts[b] is not None:
                out = jnp.where(n >= starts[b], tbl_ref[b, h], out)
        return out - c31

    i = lax.broadcasted_iota(jnp.int32, (TQ, LANES), 0)
    j = lax.broadcasted_iota(jnp.int32, (TQ, LANES), 1)
    dn_ref[0, 0] = jnp.where(j <= i, rel_bias(i - j), NEG)
    dn_ref[0, 1] = rel_bias(LANES + i - j)
    dn_ref[0, 2] = jnp.zeros((TQ, LANES), F32)
    dn_ref[0, 3] = jnp.where(j > i, 0.0, NEG)
    dn_ref[0, 4] = jnp.full((TQ, LANES), NEG, F32)

    n_cmp = (bc_ref.shape[1] - CMP_BLOCK) // CMP_STRIDE + 1

    def body(r, carry):
        t = r * TQ + i
        d = t - (j * CMP_STRIDE + CMP_BLOCK - 1)
        valid = (d >= 0) & (j < n_cmp)
        bc_ref[0, pl.ds(pl.multiple_of(r * TQ, TQ), TQ), :] = jnp.where(valid, rel_bias(d), NEG)
        return carry

    lax.fori_loop(0, bc_ref.shape[1] // TQ, body, 0)


def _bias_tiles(table, seq):
    n_maps = table.shape[1]
    return pl.pallas_call(
        _bias_kernel,
        grid=(n_maps,),
        in_specs=[pl.BlockSpec(memory_space=pltpu.SMEM)],
        out_specs=[pl.BlockSpec((1, N_BIAS_KINDS, TQ, LANES), lambda h: (h, 0, 0, 0)),
                   pl.BlockSpec((1, seq, LANES), lambda h: (h, 0, 0))],
        out_shape=[jax.ShapeDtypeStruct((n_maps, N_BIAS_KINDS, TQ, LANES), F32),
                   jax.ShapeDtypeStruct((n_maps, seq, LANES), F32)],
        compiler_params=pltpu.CompilerParams(dimension_semantics=("arbitrary",)),
        name="bias_tiles",
    )(table)


def _compress_kernel(x_ref, pe_ref, w1_ref, w2_ref, o_ref, xp_ref):
    seq = x_ref.shape[1]
    n_chunks = seq // CMP_STRIDE
    xp_ref[0:seq, :] = x_ref[0]
    xp_ref[seq:, :] = jnp.zeros((xp_ref.shape[0] - seq, xp_ref.shape[1]), F32)
    hid = jnp.zeros((n_chunks, 2 * CMP_HIDDEN), F32)
    for l in range(CMP_BLOCK):
        a = xp_ref[pl.ds(l, n_chunks, stride=CMP_STRIDE), :]
        hid = hid + jnp.dot((a + pe_ref[l:l + 1, :]).astype(BF16), w1_ref[l],
                            preferred_element_type=F32)
    hid = hid * _sigmoid(hid)
    o_ref[0, 0] = jnp.dot(hid.astype(BF16), w2_ref[...], preferred_element_type=F32).astype(BF16)


def _compress(cmp_in, pe_kv, w1_kv, w2_kv):
    b, s, _ = cmp_in.shape
    n_chunks = s // CMP_STRIDE
    return pl.pallas_call(
        _compress_kernel,
        grid=(b, NSA_GROUPS),
        in_specs=[pl.BlockSpec((1, s, LANES), lambda bi, g: (bi, 0, g)),
                  pl.BlockSpec(pe_kv.shape, lambda bi, g: (0, 0)),
                  pl.BlockSpec(w1_kv.shape, lambda bi, g: (0, 0, 0)),
                  pl.BlockSpec(w2_kv.shape, lambda bi, g: (0, 0))],
        out_specs=pl.BlockSpec((1, 1, n_chunks, 2 * LANES), lambda bi, g: (bi, g, 0, 0)),
        out_shape=jax.ShapeDtypeStruct((b, NSA_GROUPS, n_chunks, 2 * LANES), BF16),
        scratch_shapes=[pltpu.VMEM((s + CMP_BLOCK, LANES), F32)],
        compiler_params=pltpu.CompilerParams(
            dimension_semantics=("arbitrary", "arbitrary"), vmem_limit_bytes=VMEM_LIMIT),
        name="nsa_compress",
    )(cmp_in, pe_kv, w1_kv, w2_kv)


def _tile_kind(delta, window=False):
    if delta < 0 or (window and delta > WINDOW // LANES):
        return 4
    if window and delta == WINDOW // LANES:
        return 3
    return min(delta, 2)


def _tile_kinds(row_blocks, steps, window=False):
    n_hf = TK // LANES
    return [[[(slot, _tile_kind(q_tile - (kp * n_hf + hf), window)) for hf in range(n_hf)]
             for slot, q_tile in row_blocks] for kp in steps]


def _score_steps(qx, kt_tiles, kinds, bias_ref, st):
    n_rb = qx.shape[0] // TQ
    n_hf = TK // LANES
    st["mx"] = [None] * n_rb
    st["sc"] = []
    for i, load_kt in enumerate(kt_tiles):
        s = jnp.dot(qx, load_kt(), preferred_element_type=F32)
        step = []
        for rb in range(n_rb):
            tiles = []
            for hf in range(n_hf):
                slot, kind = kinds[i][rb][hf]
                if kind == 4:
                    tiles.append(None)
                    continue
                t = s[rb * TQ:(rb + 1) * TQ, hf * LANES:(hf + 1) * LANES]
                if kind != 2:
                    t = t + bias_ref[slot, kind]
                tiles.append(t)
                st["mx"][rb] = t if st["mx"][rb] is None else jnp.maximum(st["mx"][rb], t)
            step.append(tiles)
        st["sc"].append(step)
        yield


def _weight_steps(v_tiles, st):
    n_rb = len(st["mx"])
    mb = [jnp.broadcast_to(jnp.max(v, axis=-1, keepdims=True), (TQ, LANES)) for v in st["mx"]]
    st["acc"] = jnp.zeros((n_rb * TQ, 2 * LANES), F32)
    for i, load_v in enumerate(v_tiles):
        p = jnp.concatenate(
            [jnp.concatenate([jnp.zeros((TQ, LANES), F32) if t is None else jnp.exp(t - mb[rb])
                              for t in st["sc"][i][rb]], axis=1) for rb in range(n_rb)], axis=0)
        st["acc"] = st["acc"] + jnp.dot(p.astype(BF16), load_v(), preferred_element_type=F32)
        yield


def _chain(*gens):
    for g in gens:
        yield from g


def _pipeline_tiles(order, n_steps, setup, score_steps, weight_steps, finish):
    n = len(order)
    state = {}
    qk_at = [0, 0]

    def qk_stream():
        state[order[0]] = setup(order[0])
        for i, x in enumerate(order):
            if i + 1 < n:
                state[order[i + 1]] = setup(order[i + 1])
            qk_at[:] = [i, 0]
            for _ in score_steps(x, state[x]):
                qk_at[1] += 1
                yield
        qk_at[:] = [n, 0]

    def pv_stream():
        for i, x in enumerate(order):
            while not (qk_at[0] > i + 1 or qk_at[0] >= n
                       or (qk_at[0] == i + 1 and 2 * qk_at[1] >= n_steps(order[i + 1]))):
                yield
            yield from weight_steps(x, state[x])
            finish(x, state.pop(x))

    live = [qk_stream(), pv_stream()]
    while live:
        for g in list(live):
            try:
                next(g)
            except StopIteration:
                live.remove(g)


def _tile_order(n):
    return list(range(0, n, 2)) + list(range(n - 1 - n % 2, 0, -2))


def _with_ones(v_tile):
    return jnp.concatenate([v_tile, jnp.ones(v_tile.shape, v_tile.dtype)], axis=1)


def _nsa_kernel(q_ref, kst_ref, kwt_ref, vs_ref, vw_ref, kvc_ref, gate_ref, z_ref, dn_ref, bc_ref,
                ovt_ref, et_ref, gsel_ref, o_ref, *, tq):
    seq = q_ref.shape[1]
    n_rb = tq // TQ
    n_sel = seq // SEL_BLOCK

    keys = lambda kp: slice(kp * TK, (kp + 1) * TK)

    n_pairs = NSA_HPG // 2
    tiles = {}

    def tile_setup(x):
        lane = lax.broadcasted_iota(jnp.int32, (tq, LANES), 1)
        lo_half = lane < HEAD_DIM
        blk = lax.broadcasted_iota(jnp.int32, (n_sel, tq), 0)
        tok = lax.broadcasted_iota(jnp.int32, (n_sel, tq), 1)
        row = lax.broadcasted_iota(jnp.int32, (tq, 1), 0)
        kcc = kvc_ref[0, 0, :, 0:LANES]
        vcc = kvc_ref[0, 0, :, LANES:2 * LANES]
        rows = slice(x * tq, (x + 1) * tq)
        q = q_ref[0, rows, :]
        qx = []
        for j in range(NSA_HPG):
            slab = q[:, (j // 2) * LANES:(j // 2 + 1) * LANES]
            keep = lo_half if j % 2 == 0 else jnp.logical_not(lo_half)
            qx.append(jnp.where(keep, slab, jnp.zeros_like(slab)))

        any_valid = (x * tq + row >= CMP_BLOCK - 1).astype(F32)
        p_sum = jnp.zeros((tq, LANES), F32)
        o_c = []
        for j in range(NSA_HPG):
            sj = _nt_dot(qx[j], kcc) + bc_ref[j, rows, :]
            pj = jnp.exp(sj - jnp.max(sj, axis=-1, keepdims=True))
            pv = jnp.dot(pj.astype(BF16), _with_ones(vcc), preferred_element_type=F32)
            inv = any_valid / pv[:, LANES:]
            p_sum = p_sum + pj * inv
            o_c.append(pv[:, :LANES] * inv)

        imp = _nt_dot(ovt_ref[...], p_sum, precision=lax.Precision.HIGHEST)[0:n_sel, :]
        cur = jnp.right_shift(x * tq + tok, int(math.log2(SEL_BLOCK)))
        forced = (blk == 0) | (blk == cur) | (blk == cur - 1)
        val = jnp.where(forced, BIG, jnp.where(blk > cur, -BIG, imp))
        rank = jnp.zeros((n_sel, tq), F32)
        for other in range(min(n_sel, (x * tq + tq - 1) // SEL_BLOCK + 1)):
            r = val[other:other + 1, :]
            ahead = (r > val) | ((r == val) & (blk > other))
            rank = rank + jnp.where(ahead, 1.0, 0.0)
        block_mask_t = jnp.where(rank < float(min(SEL_TOPK, n_sel)), 0.0, NEG)
        block_mask = jnp.concatenate(
            [block_mask_t, jnp.zeros((LANES - n_sel, tq), F32)], axis=0).T.astype(BF16)

        return dict(qx=qx, block_mask=block_mask, o_c=o_c)

    order = _tile_order(seq // tq)

    def setup(unit):
        x, pr = unit
        ahead = order[order.index(x):][:2] if pr == 0 else [x]
        for y in ahead:
            if y not in tiles:
                tiles[y] = tile_setup(y)
        t = tiles[x] if pr + 1 < n_pairs else tiles.pop(x)
        q_win = jnp.concatenate(t["qx"][2 * pr:2 * pr + 2], axis=0)
        q_sel = jnp.concatenate([q_win, jnp.concatenate([t["block_mask"]] * 2, axis=0)], axis=1)
        return dict(q_win=q_win, q_sel=q_sel, o_c=t["o_c"][2 * pr:2 * pr + 2], win={}, sel={})

    def steps_of(unit):
        x = unit[0]
        sel_steps = range((x * tq + tq - 1) // TK + 1)
        win_steps = range(max(0, (x * tq - (WINDOW - 1)) // TK), sel_steps[-1] + 1)
        return win_steps, sel_steps

    both = lambda t: jnp.concatenate([t, t], axis=0)

    def score_steps(unit, st):
        x, pr = unit
        win_steps, sel_steps = steps_of(unit)
        row_blocks = [(j, x * n_rb + r) for j in (2 * pr, 2 * pr + 1) for r in range(n_rb)]
        return _chain(
            _score_steps(st["q_win"],
                         [(lambda kp=kp: both(kwt_ref[0, :, keys(kp)])) for kp in win_steps],
                         _tile_kinds(row_blocks, win_steps, window=True), dn_ref, st["win"]),
            _score_steps(st["q_sel"],
                         [(lambda kp=kp: jnp.concatenate(
                             [both(kst_ref[0, :, keys(kp)]), et_ref[:, keys(kp)]], axis=0))
                          for kp in sel_steps],
                         _tile_kinds(row_blocks, sel_steps), dn_ref, st["sel"]))

    def weight_steps(unit, st):
        win_steps, sel_steps = steps_of(unit)
        return _chain(
            _weight_steps([(lambda kp=kp: _with_ones(vw_ref[0, keys(kp), :])) for kp in win_steps],
                          st["win"]),
            _weight_steps([(lambda kp=kp: _with_ones(vs_ref[0, keys(kp), :])) for kp in sel_steps],
                          st["sel"]))

    def finish(unit, st):
        x, pr = unit
        rows = slice(x * tq, (x + 1) * tq)
        cols = slice(pr * LANES, (pr + 1) * LANES)
        lo_half = lax.broadcasted_iota(jnp.int32, (tq, LANES), 1) < HEAD_DIM
        gates = _sigmoid(jnp.dot(gate_ref[0, rows, :], gsel_ref[0, :, 3 * pr * LANES:3 * (pr + 1) * LANES],
                                 preferred_element_type=F32))
        merged = lambda a: jnp.where(lo_half, a[:tq, :], a[tq:, :])
        ratio = lambda acc: merged(acc[:, :LANES] / acc[:, LANES:])
        o = (gates[:, 0:LANES] * jnp.where(lo_half, st["o_c"][0], st["o_c"][1])
             + gates[:, LANES:2 * LANES] * ratio(st["sel"]["acc"])
             + gates[:, 2 * LANES:3 * LANES] * ratio(st["win"]["acc"]))
        z = z_ref[0, rows, cols].astype(F32)
        o_ref[0, rows, cols] = (o * (z * _sigmoid(z))).astype(o_ref.dtype)

    units = [(x, pr) for x in order for pr in range(n_pairs)]
    _pipeline_tiles(units, lambda u: sum(len(s) for s in steps_of(u)),
                    setup, score_steps, weight_steps, finish)


def _nsa_attention(p3, kt3, kvc, dn, bc, ovt, et, gsel, col, tq=2 * TQ):
    b, s, _ = p3.shape
    wide = 2 * LANES

    def slab(first_lane_block, per_group, width=LANES):
        return pl.BlockSpec((1, s, width), lambda g, bi: (bi, 0, first_lane_block + per_group * g))

    in_specs = [
        slab(col["q"] // wide, 1, wide),
        pl.BlockSpec((1, HEAD_DIM, s), lambda g, bi: (bi, g, 0)),
        pl.BlockSpec((1, HEAD_DIM, s), lambda g, bi: (bi, NSA_GROUPS + g, 0)),
        slab(col["vsel"] // LANES, 1), slab(col["vwin"] // LANES, 1),
        pl.BlockSpec((1, 1, kvc.shape[2], wide), lambda g, bi: (bi, g, 0, 0)),
        slab(col["gate"] // LANES, 0),
        slab(col["z"] // wide, 1, wide),
        _resident((NSA_HPG, N_BIAS_KINDS, TQ, LANES), lambda g, bi: (g, 0, 0, 0)),
        _resident((NSA_HPG, s, LANES), lambda g, bi: (g, 0, 0)),
        _resident(ovt.shape, lambda g, bi: (0, 0)),
        _resident(et.shape, lambda g, bi: (0, 0)),
        _resident((1,) + gsel.shape[1:], lambda g, bi: (g, 0, 0)),
    ]
    return pl.pallas_call(
        functools.partial(_nsa_kernel, tq=tq),
        grid=(NSA_GROUPS, b),
        in_specs=in_specs,
        out_specs=pl.BlockSpec((1, s, wide), lambda g, bi: (bi, 0, g)),
        out_shape=jax.ShapeDtypeStruct((b, s, NSA_GROUPS * wide), BF16),
        compiler_params=pltpu.CompilerParams(
            dimension_semantics=("arbitrary", "arbitrary"), vmem_limit_bytes=VMEM_LIMIT),
        name="nsa_attention",
    )(p3, kt3, kt3, p3, p3, kvc, p3, p3, dn, bc, ovt, et, gsel)


def _diff_kernel(q_ref, kt_ref, v_ref, z_ref, dn_ref, lq1_ref, lk1_ref, lq2_ref, lk2_ref, sub_ref,
                 o_ref, *, lambda_init, tq):
    seq = q_ref.shape[1]
    n_rb = tq // TQ
    lane = lax.broadcasted_iota(jnp.int32, (tq, LANES), 1)
    lam = (jnp.exp(jnp.sum(lq1_ref[...] * lk1_ref[...], axis=-1, keepdims=True))
           - jnp.exp(jnp.sum(lq2_ref[...] * lk2_ref[...], axis=-1, keepdims=True)) + lambda_init)
    keys = lambda kp: slice(kp * TK, (kp + 1) * TK)
    steps_of = lambda x: range((x * tq + tq - 1) // TK + 1)

    def setup(x):
        q = q_ref[0, x * tq:(x + 1) * tq, :]
        zero = jnp.zeros_like(q)
        return dict(q=jnp.concatenate([jnp.where(lane < HEAD_DIM, q, zero),
                                       jnp.where(lane >= HEAD_DIM, q, zero)], axis=0))

    def score_steps(x, st):
        row_blocks = [(m, x * n_rb + r) for m in range(2) for r in range(n_rb)]
        return _score_steps(st["q"], [(lambda kp=kp: kt_ref[0, :, keys(kp)]) for kp in steps_of(x)],
                            _tile_kinds(row_blocks, steps_of(x)), dn_ref, st)

    def weight_steps(x, st):
        return _weight_steps([(lambda kp=kp: _with_ones(v_ref[0, keys(kp), :]))
                              for kp in steps_of(x)], st)

    def finish(x, st):
        rows = slice(x * tq, (x + 1) * tq)
        a = st["acc"][:, :LANES] / st["acc"][:, LANES:]
        o = a[:tq, :] - lam * a[tq:, :]
        o = o * lax.rsqrt(jnp.mean(o * o, axis=-1, keepdims=True) + EPS) * sub_ref[...]
        o = o * (1.0 - lambda_init)
        z = z_ref[0, rows, :].astype(F32)
        o_ref[0, rows, :] = (o * (z * _sigmoid(z))).astype(o_ref.dtype)

    _pipeline_tiles(_tile_order(seq // tq), lambda x: len(steps_of(x)),
                    setup, score_steps, weight_steps, finish)


def _diff_attention(p3, kt3, dn, lq1, lk1, lq2, lk2, subln, lambda_init, tq=2 * TQ):
    b, s, _ = p3.shape
    h = DIFF_HEADS
    vec = lambda a: a.reshape(1, -1)
    small = pl.BlockSpec((1, HEAD_DIM), lambda bi, hi: (0, 0))
    slab = lambda first: pl.BlockSpec((1, s, LANES), lambda bi, hi: (bi, 0, first + hi))
    in_specs = [
        slab(0),
        pl.BlockSpec((1, LANES, s), lambda bi, hi: (bi, hi, 0)),
        slab(h), slab(2 * h),
        pl.BlockSpec((2, N_BIAS_KINDS, TQ, LANES), lambda bi, hi: (hi, 0, 0, 0)),
        small, small, small, small,
        pl.BlockSpec((1, DIFF_VDIM), lambda bi, hi: (0, hi)),
    ]
    return pl.pallas_call(
        functools.partial(_diff_kernel, lambda_init=lambda_init, tq=tq),
        grid=(b, h),
        in_specs=in_specs,
        out_specs=slab(0),
        out_shape=jax.ShapeDtypeStruct((b, s, h * DIFF_VDIM), BF16),
        compiler_params=pltpu.CompilerParams(
            dimension_semantics=("arbitrary", "arbitrary"), vmem_limit_bytes=VMEM_LIMIT),
        name="diff_attention",
    )(p3, kt3, p3, p3, dn, vec(lq1), vec(lk1), vec(lq2), vec(lk2), vec(subln))


def _nsa_weight_layout(w_in):
    d = HEAD_DIM
    kv = NSA_GROUPS * d
    width = NSA_HEADS * d
    pts = np.cumsum([width] + [kv] * 6 + [3 * NSA_HEADS, width])
    q, kc, vc, ks, vs, kw, vw, gate, z = [w_in[:, a:b] for a, b in zip([0] + list(pts[:-1]), pts)]
    grp = lambda w, g: w[:, g * d:(g + 1) * d]
    dup = lambda w: jnp.concatenate(
        [jnp.concatenate([grp(w, g), grp(w, g)], axis=1) for g in range(NSA_GROUPS)], axis=1)
    gate_pad = jnp.pad(gate, ((0, 0), (0, LANES - gate.shape[1])))
    parts = [("q", q * (d ** -0.5)), ("z", z), ("vsel", dup(vs)), ("vwin", dup(vw)),
             ("gate", gate_pad)]
    col, off = {}, 0
    for name, w in parts:
        col[name] = off
        off += w.shape[1]
    w_main = jnp.concatenate([w for _, w in parts], axis=1).astype(BF16)
    w_keys_t = jnp.concatenate([ks, kw], axis=1).T.astype(BF16)
    w_cmp = jnp.concatenate(
        [jnp.concatenate([grp(kc, g), grp(vc, g)], axis=1) for g in range(NSA_GROUPS)],
        axis=1).astype(BF16)
    return w_main, w_cmp, w_keys_t, col


def _compress_weight_layout(pe_k, w1_k, w2_k, pe_v, w1_v, w2_v):
    d, hdn = HEAD_DIM, CMP_HIDDEN
    pe_kv = jnp.concatenate([pe_k, pe_v], axis=1)
    w1k = w1_k.reshape(CMP_BLOCK, d, hdn)
    w1v = w1_v.reshape(CMP_BLOCK, d, hdn)
    zero = jnp.zeros_like(w1k)
    w1_kv = jnp.concatenate([jnp.concatenate([w1k, zero], axis=2),
                             jnp.concatenate([zero, w1v], axis=2)], axis=1).astype(BF16)
    z2 = jnp.zeros((hdn, 2 * d), w2_k.dtype)
    w2_kv = jnp.concatenate([jnp.concatenate([w2_k, w2_k, z2], axis=1),
                             jnp.concatenate([z2, w2_v, w2_v], axis=1)], axis=0).astype(BF16)
    return pe_kv, w1_kv, w2_kv


def _selection_constants(seq):
    n_cmp_pad = seq // CMP_STRIDE
    n_cmp = (seq - CMP_BLOCK) // CMP_STRIDE + 1
    n_sel = seq // SEL_BLOCK
    cmp_lo = np.arange(n_cmp_pad) * CMP_STRIDE
    sel_lo = np.arange(n_sel) * SEL_BLOCK
    overlap = np.maximum(np.minimum(cmp_lo[:, None] + CMP_BLOCK, sel_lo[None, :] + SEL_BLOCK)
                         - np.maximum(cmp_lo[:, None], sel_lo[None, :]), 0).astype(np.float32) / CMP_BLOCK
    overlap[n_cmp:] = 0.0
    ovt = np.zeros((LANES, n_cmp_pad), np.float32)
    ovt[:n_sel] = overlap.T
    et = np.zeros((LANES, seq), np.float32)
    et[np.arange(seq) // SEL_BLOCK, np.arange(seq)] = 1.0
    n_pairs = NSA_HPG // 2
    gsel = np.zeros((NSA_GROUPS, LANES, 3 * n_pairs * LANES), np.float32)
    for g in range(NSA_GROUPS):
        for pr in range(n_pairs):
            for i in range(3):
                for n in range(LANES):
                    head = g * NSA_HPG + 2 * pr + n // HEAD_DIM
                    gsel[g, 3 * head + i, (pr * 3 + i) * LANES + n] = 1.0
    return jnp.asarray(ovt), jnp.asarray(et, dtype=BF16), jnp.asarray(gsel, dtype=BF16)


def kernel(x, rel_bias_table, norm_pre, norm_post, nsa_w_in, nsa_cmp_pe_k, nsa_cmp_w1_k, nsa_cmp_w2_k,
           nsa_cmp_pe_v, nsa_cmp_w1_v, nsa_cmp_w2_v, nsa_w_out, diff_w_in, diff_lambda_q1,
           diff_lambda_k1, diff_lambda_q2, diff_lambda_k2, diff_subln, diff_w_out):
    b, s, d = x.shape
    n = b * s
    assert d == D_MODEL and s % TK == 0 and s // SEL_BLOCK <= LANES and s // CMP_STRIDE == LANES
    dn, bc = _bias_tiles(rel_bias_table, s)
    ovt, et, gsel = _selection_constants(s)
    x2 = x.reshape(n, d)
    for i in range(DEPTH):
        j = i // 2
        if i % 2 == 0:
            w_main, w_cmp, w_keys_t, col = _nsa_weight_layout(nsa_w_in[j])
            p_main, p_cmp, keys_t = _norm_proj(x2, norm_pre[i], s, [w_main, w_cmp], [BF16, F32],
                                               [w_keys_t])
            kvc = _compress(p_cmp.reshape(b, s, -1),
                            *_compress_weight_layout(nsa_cmp_pe_k[j], nsa_cmp_w1_k[j], nsa_cmp_w2_k[j],
                                                     nsa_cmp_pe_v[j], nsa_cmp_w1_v[j], nsa_cmp_w2_v[j]))
            o = _nsa_attention(p_main.reshape(b, s, -1), keys_t, kvc, dn, bc, ovt, et, gsel, col)
            w_out = nsa_w_out[j]
        else:
            lambda_init = 0.8 - 0.6 * math.exp(-0.3 * i)
            w = diff_w_in[j]
            w_main = jnp.concatenate([w[:, :d] * (HEAD_DIM ** -0.5), w[:, 2 * d:]], axis=1).astype(BF16)
            p_main, keys_t = _norm_proj(x2, norm_pre[i], s, [w_main], [BF16],
                                        [w[:, d:2 * d].T.astype(BF16)])
            o = _diff_attention(p_main.reshape(b, s, -1), keys_t, dn, diff_lambda_q1[j],
                                diff_lambda_k1[j], diff_lambda_q2[j], diff_lambda_k2[j],
                                diff_subln[j], lambda_init)
            w_out = diff_w_out[j]
        x2 = _out_proj(o.reshape(n, -1), w_out.astype(BF16), x2, norm_post[i])
    return x2.reshape(b, s, d)
```

```python
import functools
import math

import jax
import jax.numpy as jnp
import numpy as np
from jax import lax
from jax.experimental import pallas as pl
from jax.experimental.pallas import tpu as pltpu

F32 = jnp.float32
BF16 = jnp.bfloat16

D_MODEL = 1024
DEPTH = 2
REL_BUCKETS = 32
REL_MAX_DIST = 128
NSA_HEADS = 16
NSA_GROUPS = 4
NSA_HPG = NSA_HEADS // NSA_GROUPS
HEAD_DIM = 64
CMP_BLOCK = 32
CMP_STRIDE = 16
CMP_HIDDEN = 128
SEL_BLOCK = 64
SEL_TOPK = 8
WINDOW = 512
DIFF_HEADS = 8
DIFF_VDIM = 128
NEG = -1e30
BIG = 1e9
EPS = 1e-6

LANES = 128
TQ = 128
TK = 2 * LANES
N_BIAS_KINDS = 5
VMEM_LIMIT = 52 * 1024 * 1024


def _sigmoid(x):
    return 1.0 / (1.0 + jnp.exp(-x))


def _nt_dot(a, b, precision=None):
    return lax.dot_general(a, b, (((1,), (1,)), ((), ())), precision=precision,
                           preferred_element_type=F32)


def _resident(shape, index_map):
    return pl.BlockSpec(shape, index_map, pipeline_mode=pl.Buffered(1))


def _norm_proj_kernel(x_ref, g_ref, *refs, n_w, n_wt):
    w_refs, wt_refs, o_refs = refs[:n_w], refs[n_w:n_w + n_wt], refs[n_w + n_wt:]
    x = x_ref[...]
    u = x * lax.rsqrt(jnp.mean(x * x, axis=-1, keepdims=True) + EPS) * g_ref[...]
    ub = u.astype(BF16)
    for w_ref, o_ref in zip(w_refs, o_refs[:n_w]):
        o_ref[...] = jnp.dot(ub, w_ref[...], preferred_element_type=F32).astype(o_ref.dtype)
    for wt_ref, o_ref in zip(wt_refs, o_refs[n_w:]):
        o_ref[0] = _nt_dot(wt_ref[...], ub).astype(o_ref.dtype)


def _norm_proj(x2, g, seq, weights, out_dtypes, weights_t=(), tm=256):
    n, d = x2.shape
    n_st = seq // tm
    in_specs = [pl.BlockSpec((tm, d), lambda i: (i, 0)), pl.BlockSpec((1, d), lambda i: (0, 0))]
    out_specs, out_shape = [], []
    for w, dt in zip(weights, out_dtypes):
        c = w.shape[1]
        in_specs.append(_resident((d, c), lambda i: (0, 0)))
        out_specs.append(pl.BlockSpec((tm, c), lambda i: (i, 0)))
        out_shape.append(jax.ShapeDtypeStruct((n, c), dt))
    for wt in weights_t:
        c = wt.shape[0]
        in_specs.append(_resident((c, d), lambda i: (0, 0)))
        out_specs.append(pl.BlockSpec((1, c, tm), lambda i: (i // n_st, 0, i % n_st)))
        out_shape.append(jax.ShapeDtypeStruct((n // seq, c, seq), BF16))
    return pl.pallas_call(
        functools.partial(_norm_proj_kernel, n_w=len(weights), n_wt=len(weights_t)),
        grid=(n // tm,),
        in_specs=in_specs,
        out_specs=out_specs,
        out_shape=out_shape,
        compiler_params=pltpu.CompilerParams(
            dimension_semantics=("arbitrary",), vmem_limit_bytes=VMEM_LIMIT),
        name="norm_proj",
    )(x2, g.reshape(1, d), *weights, *weights_t)


def _out_proj_kernel(o_ref, w_ref, x_ref, g_ref, y_ref):
    y = jnp.dot(o_ref[...], w_ref[...], preferred_element_type=F32)
    r = y * lax.rsqrt(jnp.mean(y * y, axis=-1, keepdims=True) + EPS) * g_ref[...]
    y_ref[...] = x_ref[...] + r


def _out_proj(o2, w, x2, g, tm=512):
    n, d = x2.shape
    return pl.pallas_call(
        _out_proj_kernel,
        grid=(n // tm,),
        in_specs=[pl.BlockSpec((tm, o2.shape[1]), lambda i: (i, 0)),
                  _resident(w.shape, lambda i: (0, 0)),
                  pl.BlockSpec((tm, d), lambda i: (i, 0)),
                  pl.BlockSpec((1, d), lambda i: (0, 0))],
        out_specs=pl.BlockSpec((tm, d), lambda i: (i, 0)),
        out_shape=jax.ShapeDtypeStruct((n, d), F32),
        compiler_params=pltpu.CompilerParams(
            dimension_semantics=("arbitrary",), vmem_limit_bytes=VMEM_LIMIT),
        name="out_proj",
    )(o2, w, x2, g.reshape(1, d))


def _bucket_starts(max_dist):
    max_exact = REL_BUCKETS // 2
    n = np.arange(max_dist)
    nf = np.maximum(n, 1).astype(np.float32)
    large = max_exact + (np.log(nf / max_exact) / np.float32(math.log(REL_MAX_DIST / max_exact))
                         * (REL_BUCKETS - max_exact)).astype(np.int32)
    bucket = np.where(n < max_exact, n, np.minimum(large, REL_BUCKETS - 1))
    assert (np.diff(bucket) >= 0).all()
    return [int(np.argmax(bucket >= b)) if (bucket >= b).any() else None
            for b in range(REL_BUCKETS)]


def _bias_kernel(tbl_ref, dn_ref, bc_ref):
    h = pl.program_id(0)
    c31 = tbl_ref[REL_BUCKETS - 1, h]
    starts = _bucket_starts(bc_ref.shape[1])

    def rel_bias(dist):
        n = jnp.maximum(dist, 0)
        out = jnp.full(dist.shape, tbl_ref[0, h], F32)
        for b in range(1, REL_BUCKETS):
            if starts[b] is not None:
                out = jnp.where(n >= starts[b], tbl_ref[b, h], out)
        return out - c31

    i = lax.broadcasted_iota(jnp.int32, (TQ, LANES), 0)
    j = lax.broadcasted_iota(jnp.int32, (TQ, LANES), 1)
    dn_ref[0, 0] = jnp.where(j <= i, rel_bias(i - j), NEG)
    dn_ref[0, 1] = rel_bias(LANES + i - j)
    dn_ref[0, 2] = jnp.zeros((TQ, LANES), F32)
    dn_ref[0, 3] = jnp.where(j > i, 0.0, NEG)
    dn_ref[0, 4] = jnp.full((TQ, LANES), NEG, F32)

    n_cmp = (bc_ref.shape[1] - CMP_BLOCK) // CMP_STRIDE + 1

    def body(r, carry):
        t = r * TQ + i
        d = t - (j * CMP_STRIDE + CMP_BLOCK - 1)
        valid = (d >= 0) & (j < n_cmp)
        bc_ref[0, pl.ds(pl.multiple_of(r * TQ, TQ), TQ), :] = jnp.where(valid, rel_bias(d), NEG)
        return carry

    lax.fori_loop(0, bc_ref.shape[1] // TQ, body, 0)


def _bias_tiles(table, seq):
    n_maps = table.shape[1]
    return pl.pallas_call(
        _bias_kernel,
        grid=(n_maps,),
        in_specs=[pl.BlockSpec(memory_space=pltpu.SMEM)],
        out_specs=[pl.BlockSpec((1, N_BIAS_KINDS, TQ, LANES), lambda h: (h, 0, 0, 0)),
                   pl.BlockSpec((1, seq, LANES), lambda h: (h, 0, 0))],
        out_shape=[jax.ShapeDtypeStruct((n_maps, N_BIAS_KINDS, TQ, LANES), F32),
                   jax.ShapeDtypeStruct((n_maps, seq, LANES), F32)],
        compiler_params=pltpu.CompilerParams(dimension_semantics=("arbitrary",)),
        name="bias_tiles",
    )(table)


def _compress_kernel(x_ref, pe_ref, w1_ref, w2_ref, o_ref, xp_ref):
    seq = x_ref.shape[1]
    n_chunks = seq // CMP_STRIDE
    xp_ref[0:seq, :] = x_ref[0]
    xp_ref[seq:, :] = jnp.zeros((xp_ref.shape[0] - seq, xp_ref.shape[1]), F32)
    hid = jnp.zeros((n_chunks, 2 * CMP_HIDDEN), F32)
    for l in range(CMP_BLOCK):
        a = xp_ref[pl.ds(l, n_chunks, stride=CMP_STRIDE), :]
        hid = hid + jnp.dot((a + pe_ref[l:l + 1, :]).astype(BF16), w1_ref[l],
                            preferred_element_type=F32)
    hid = hid * _sigmoid(hid)
    o_ref[0, 0] = jnp.dot(hid.astype(BF16), w2_ref[...], preferred_element_type=F32).astype(BF16)


def _compress(cmp_in, pe_kv, w1_kv, w2_kv):
    b, s, _ = cmp_in.shape
    n_chunks = s // CMP_STRIDE
    return pl.pallas_call(
        _compress_kernel,
        grid=(b, NSA_GROUPS),
        in_specs=[pl.BlockSpec((1, s, LANES), lambda bi, g: (bi, 0, g)),
                  pl.BlockSpec(pe_kv.shape, lambda bi, g: (0, 0)),
                  pl.BlockSpec(w1_kv.shape, lambda bi, g: (0, 0, 0)),
                  pl.BlockSpec(w2_kv.shape, lambda bi, g: (0, 0))],
        out_specs=pl.BlockSpec((1, 1, n_chunks, 2 * LANES), lambda bi, g: (bi, g, 0, 0)),
        out_shape=jax.ShapeDtypeStruct((b, NSA_GROUPS, n_chunks, 2 * LANES), BF16),
        scratch_shapes=[pltpu.VMEM((s + CMP_BLOCK, LANES), F32)],
        compiler_params=pltpu.CompilerParams(
            dimension_semantics=("arbitrary", "arbitrary"), vmem_limit_bytes=VMEM_LIMIT),
        name="nsa_compress",
    )(cmp_in, pe_kv, w1_kv, w2_kv)


def _tile_kind(delta, window=False):
    if delta < 0 or (window and delta > WINDOW // LANES):
        return 4
    if window and delta == WINDOW // LANES:
        return 3
    return min(delta, 2)


def _tile_kinds(row_blocks, steps, window=False):
    n_hf = TK // LANES
    return [[[(slot, _tile_kind(q_tile - (kp * n_hf + hf), window)) for hf in range(n_hf)]
             for slot, q_tile in row_blocks] for kp in steps]


def _score_steps(qx, kt_tiles, kinds, bias_ref, st):
    n_rb = qx.shape[0] // TQ
    n_hf = TK // LANES
    st["mx"] = [None] * n_rb
    st["sc"] = []
    for i, load_kt in enumerate(kt_tiles):
        s = jnp.dot(qx, load_kt(), preferred_element_type=F32)
        step = []
        for rb in range(n_rb):
            tiles = []
            for hf in range(n_hf):
                slot, kind = kinds[i][rb][hf]
                if kind == 4:
                    tiles.append(None)
                    continue
                t = s[rb * TQ:(rb + 1) * TQ, hf * LANES:(hf + 1) * LANES]
                if kind != 2:
                    t = t + bias_ref[slot, kind]
                tiles.append(t)
                st["mx"][rb] = t if st["mx"][rb] is None else jnp.maximum(st["mx"][rb], t)
            step.append(tiles)
        st["sc"].append(step)
        yield


def _weight_steps(v_tiles, st):
    n_rb = len(st["mx"])
    mb = [jnp.broadcast_to(jnp.max(v, axis=-1, keepdims=True), (TQ, LANES)) for v in st["mx"]]
    st["acc"] = jnp.zeros((n_rb * TQ, 2 * LANES), F32)
    for i, load_v in enumerate(v_tiles):
        p = jnp.concatenate(
            [jnp.concatenate([jnp.zeros((TQ, LANES), F32) if t is None else jnp.exp(t - mb[rb])
                              for t in st["sc"][i][rb]], axis=1) for rb in range(n_rb)], axis=0)
        st["acc"] = st["acc"] + jnp.dot(p.astype(BF16), load_v(), preferred_element_type=F32)
        yield


def _chain(*gens):
    for g in gens:
        yield from g


def _pipeline_tiles(order, n_steps, setup, score_steps, weight_steps, finish):
    n = len(order)
    state = {}
    qk_at = [0, 0]

    def qk_stream():
        state[order[0]] = setup(order[0])
        for i, x in enumerate(order):
            if i + 1 < n:
                state[order[i + 1]] = setup(order[i + 1])
            qk_at[:] = [i, 0]
            for _ in score_steps(x, state[x]):
                qk_at[1] += 1
                yield
        qk_at[:] = [n, 0]

    def pv_stream():
        for i, x in enumerate(order):
            while not (qk_at[0] > i + 1 or qk_at[0] >= n
                       or (qk_at[0] == i + 1 and 2 * qk_at[1] >= n_steps(order[i + 1]))):
                yield
            yield from weight_steps(x, state[x])
            finish(x, state.pop(x))

    live = [qk_stream(), pv_stream()]
    while live:
        for g in list(live):
            try:
                next(g)
            except StopIteration:
                live.remove(g)


def _tile_order(n):
    return list(range(0, n, 2)) + list(range(n - 1 - n % 2, 0, -2))


def _with_ones(v_tile):
    return jnp.concatenate([v_tile, jnp.ones(v_tile.shape, v_tile.dtype)], axis=1)


def _nsa_kernel(q_ref, kst_ref, kwt_ref, vs_ref, vw_ref, kvc_ref, gate_ref, z_ref, dn_ref, bc_ref,
                ovt_ref, et_ref, gsel_ref, o_ref, *, tq):
    seq = q_ref.shape[1]
    n_rb = tq // TQ
    n_sel = seq // SEL_BLOCK

    keys = lambda kp: slice(kp * TK, (kp + 1) * TK)

    n_pairs = NSA_HPG // 2
    tiles = {}

    def tile_setup(x):
        lane = lax.broadcasted_iota(jnp.int32, (tq, LANES), 1)
        lo_half = lane < HEAD_DIM
        blk = lax.broadcasted_iota(jnp.int32, (n_sel, tq), 0)
        tok = lax.broadcasted_iota(jnp.int32, (n_sel, tq), 1)
        row = lax.broadcasted_iota(jnp.int32, (tq, 1), 0)
        kcc = kvc_ref[0, 0, :, 0:LANES]
        vcc = kvc_ref[0, 0, :, LANES:2 * LANES]
        rows = slice(x * tq, (x + 1) * tq)
        q = q_ref[0, rows, :]
        qx = []
        for j in range(NSA_HPG):
            slab = q[:, (j // 2) * LANES:(j // 2 + 1) * LANES]
            keep = lo_half if j % 2 == 0 else jnp.logical_not(lo_half)
            qx.append(jnp.where(keep, slab, jnp.zeros_like(slab)))

        any_valid = (x * tq + row >= CMP_BLOCK - 1).astype(F32)
        p_sum = jnp.zeros((tq, LANES), F32)
        o_c = []
        for j in range(NSA_HPG):
            sj = _nt_dot(qx[j], kcc) + bc_ref[j, rows, :]
            pj = jnp.exp(sj - jnp.max(sj, axis=-1, keepdims=True))
            pv = jnp.dot(pj.astype(BF16), _with_ones(vcc), preferred_element_type=F32)
            inv = any_valid / pv[:, LANES:]
            p_sum = p_sum + pj * inv
            o_c.append(pv[:, :LANES] * inv)

        imp = _nt_dot(ovt_ref[...], p_sum, precision=lax.Precision.HIGHEST)[0:n_sel, :]
        cur = jnp.right_shift(x * tq + tok, int(math.log2(SEL_BLOCK)))
        forced = (blk == 0) | (blk == cur) | (blk == cur - 1)
        val = jnp.where(forced, BIG, jnp.where(blk > cur, -BIG, imp))
        rank = jnp.zeros((n_sel, tq), F32)
        for other in range(min(n_sel, (x * tq + tq - 1) // SEL_BLOCK + 1)):
            r = val[other:other + 1, :]
            ahead = (r > val) | ((r == val) & (blk > other))
            rank = rank + jnp.where(ahead, 1.0, 0.0)
        block_mask_t = jnp.where(rank < float(min(SEL_TOPK, n_sel)), 0.0, NEG)
        block_mask = jnp.concatenate(
            [block_mask_t, jnp.zeros((LANES - n_sel, tq), F32)], axis=0).T.astype(BF16)

        return dict(qx=qx, block_mask=block_mask, o_c=o_c)

    order = _tile_order(seq // tq)
    started = []

    def setup(unit):
        x, pr = unit
        if not started:
            started.append(True)
            for y in order:
                tiles[y] = tile_setup(y)
        t = tiles[x] if pr + 1 < n_pairs else tiles.pop(x)
        q_win = jnp.concatenate(t["qx"][2 * pr:2 * pr + 2], axis=0)
        q_sel = jnp.concatenate([q_win, jnp.concatenate([t["block_mask"]] * 2, axis=0)], axis=1)
        return dict(q_win=q_win, q_sel=q_sel, o_c=t["o_c"][2 * pr:2 * pr + 2], win={}, sel={})

    def steps_of(unit):
        x = unit[0]
        sel_steps = range((x * tq + tq - 1) // TK + 1)
        win_steps = range(max(0, (x * tq - (WINDOW - 1)) // TK), sel_steps[-1] + 1)
        return win_steps, sel_steps

    both = lambda t: jnp.concatenate([t, t], axis=0)

    def score_steps(unit, st):
        x, pr = unit
        win_steps, sel_steps = steps_of(unit)
        row_blocks = [(j, x * n_rb + r) for j in (2 * pr, 2 * pr + 1) for r in range(n_rb)]
        return _chain(
            _score_steps(st["q_win"],
                         [(lambda kp=kp: both(kwt_ref[0, :, keys(kp)])) for kp in win_steps],
                         _tile_kinds(row_blocks, win_steps, window=True), dn_ref, st["win"]),
            _score_steps(st["q_sel"],
                         [(lambda kp=kp: jnp.concatenate(
                             [both(kst_ref[0, :, keys(kp)]), et_ref[:, keys(kp)]], axis=0))
                          for kp in sel_steps],
                         _tile_kinds(row_blocks, sel_steps), dn_ref, st["sel"]))

    def weight_steps(unit, st):
        win_steps, sel_steps = steps_of(unit)
        return _chain(
            _weight_steps([(lambda kp=kp: _with_ones(vw_ref[0, keys(kp), :])) for kp in win_steps],
                          st["win"]),
            _weight_steps([(lambda kp=kp: _with_ones(vs_ref[0, keys(kp), :])) for kp in sel_steps],
                          st["sel"]))

    def finish(unit, st):
        x, pr = unit
        rows = slice(x * tq, (x + 1) * tq)
        cols = slice(pr * LANES, (pr + 1) * LANES)
        lo_half = lax.broadcasted_iota(jnp.int32, (tq, LANES), 1) < HEAD_DIM
        gates = _sigmoid(jnp.dot(gate_ref[0, rows, :], gsel_ref[0, :, 3 * pr * LANES:3 * (pr + 1) * LANES],
                                 preferred_element_type=F32))
        merged = lambda a: jnp.where(lo_half, a[:tq, :], a[tq:, :])
        ratio = lambda acc: merged(acc[:, :LANES] / acc[:, LANES:])
        o = (gates[:, 0:LANES] * jnp.where(lo_half, st["o_c"][0], st["o_c"][1])
             + gates[:, LANES:2 * LANES] * ratio(st["sel"]["acc"])
             + gates[:, 2 * LANES:3 * LANES] * ratio(st["win"]["acc"]))
        z = z_ref[0, rows, cols].astype(F32)
        o_ref[0, rows, cols] = (o * (z * _sigmoid(z))).astype(o_ref.dtype)

    units = [(x, pr) for x in order for pr in range(n_pairs)]
    _pipeline_tiles(units, lambda u: sum(len(s) for s in steps_of(u)),
                    setup, score_steps, weight_steps, finish)


def _nsa_attention(p3, kt3, kvc, dn, bc, ovt, et, gsel, col, tq=2 * TQ):
    b, s, _ = p3.shape
    wide = 2 * LANES

    def slab(first_lane_block, per_group, width=LANES):
        return pl.BlockSpec((1, s, width), lambda g, bi: (bi, 0, first_lane_block + per_group * g))

    in_specs = [
        slab(col["q"] // wide, 1, wide),
        pl.BlockSpec((1, HEAD_DIM, s), lambda g, bi: (bi, g, 0)),
        pl.BlockSpec((1, HEAD_DIM, s), lambda g, bi: (bi, NSA_GROUPS + g, 0)),
        slab(col["vsel"] // LANES, 1), slab(col["vwin"] // LANES, 1),
        pl.BlockSpec((1, 1, kvc.shape[2], wide), lambda g, bi: (bi, g, 0, 0)),
        slab(col["gate"] // LANES, 0),
        slab(col["z"] // wide, 1, wide),
        _resident((NSA_HPG, N_BIAS_KINDS, TQ, LANES), lambda g, bi: (g, 0, 0, 0)),
        _resident((NSA_HPG, s, LANES), lambda g, bi: (g, 0, 0)),
        _resident(ovt.shape, lambda g, bi: (0, 0)),
        _resident(et.shape, lambda g, bi: (0, 0)),
        _resident((1,) + gsel.shape[1:], lambda g, bi: (g, 0, 0)),
    ]
    return pl.pallas_call(
        functools.partial(_nsa_kernel, tq=tq),
        grid=(NSA_GROUPS, b),
        in_specs=in_specs,
        out_specs=pl.BlockSpec((1, s, wide), lambda g, bi: (bi, 0, g)),
        out_shape=jax.ShapeDtypeStruct((b, s, NSA_GROUPS * wide), BF16),
        compiler_params=pltpu.CompilerParams(
            dimension_semantics=("arbitrary", "arbitrary"), vmem_limit_bytes=VMEM_LIMIT),
        name="nsa_attention",
    )(p3, kt3, kt3, p3, p3, kvc, p3, p3, dn, bc, ovt, et, gsel)


def _diff_kernel(q_ref, kt_ref, v_ref, z_ref, dn_ref, lq1_ref, lk1_ref, lq2_ref, lk2_ref, sub_ref,
                 o_ref, *, lambda_init, tq):
    seq = q_ref.shape[1]
    n_rb = tq // TQ
    lane = lax.broadcasted_iota(jnp.int32, (tq, LANES), 1)
    lam = (jnp.exp(jnp.sum(lq1_ref[...] * lk1_ref[...], axis=-1, keepdims=True))
           - jnp.exp(jnp.sum(lq2_ref[...] * lk2_ref[...], axis=-1, keepdims=True)) + lambda_init)
    keys = lambda kp: slice(kp * TK, (kp + 1) * TK)
    steps_of = lambda x: range((x * tq + tq - 1) // TK + 1)

    def setup(x):
        q = q_ref[0, x * tq:(x + 1) * tq, :]
        zero = jnp.zeros_like(q)
        return dict(q=jnp.concatenate([jnp.where(lane < HEAD_DIM, q, zero),
                                       jnp.where(lane >= HEAD_DIM, q, zero)], axis=0))

    def score_steps(x, st):
        row_blocks = [(m, x * n_rb + r) for m in range(2) for r in range(n_rb)]
        return _score_steps(st["q"], [(lambda kp=kp: kt_ref[0, :, keys(kp)]) for kp in steps_of(x)],
                            _tile_kinds(row_blocks, steps_of(x)), dn_ref, st)

    def weight_steps(x, st):
        return _weight_steps([(lambda kp=kp: _with_ones(v_ref[0, keys(kp), :]))
                              for kp in steps_of(x)], st)

    def finish(x, st):
        rows = slice(x * tq, (x + 1) * tq)
        a = st["acc"][:, :LANES] / st["acc"][:, LANES:]
        o = a[:tq, :] - lam * a[tq:, :]
        o = o * lax.rsqrt(jnp.mean(o * o, axis=-1, keepdims=True) + EPS) * sub_ref[...]
        o = o * (1.0 - lambda_init)
        z = z_ref[0, rows, :].astype(F32)
        o_ref[0, rows, :] = (o * (z * _sigmoid(z))).astype(o_ref.dtype)

    _pipeline_tiles(_tile_order(seq // tq), lambda x: len(steps_of(x)),
                    setup, score_steps, weight_steps, finish)


def _diff_attention(p3, kt3, dn, lq1, lk1, lq2, lk2, subln, lambda_init, tq=2 * TQ):
    b, s, _ = p3.shape
    h = DIFF_HEADS
    vec = lambda a: a.reshape(1, -1)
    small = pl.BlockSpec((1, HEAD_DIM), lambda bi, hi: (0, 0))
    slab = lambda first: pl.BlockSpec((1, s, LANES), lambda bi, hi: (bi, 0, first + hi))
    in_specs = [
        slab(0),
        pl.BlockSpec((1, LANES, s), lambda bi, hi: (bi, hi, 0)),
        slab(h), slab(2 * h),
        pl.BlockSpec((2, N_BIAS_KINDS, TQ, LANES), lambda bi, hi: (hi, 0, 0, 0)),
        small, small, small, small,
        pl.BlockSpec((1, DIFF_VDIM), lambda bi, hi: (0, hi)),
    ]
    return pl.pallas_call(
        functools.partial(_diff_kernel, lambda_init=lambda_init, tq=tq),
        grid=(b, h),
        in_specs=in_specs,
        out_specs=slab(0),
        out_shape=jax.ShapeDtypeStruct((b, s, h * DIFF_VDIM), BF16),
        compiler_params=pltpu.CompilerParams(
            dimension_semantics=("arbitrary", "arbitrary"), vmem_limit_bytes=VMEM_LIMIT),
        name="diff_attention",
    )(p3, kt3, p3, p3, dn, vec(lq1), vec(lk1), vec(lq2), vec(lk2), vec(subln))


def _nsa_weight_layout(w_in):
    d = HEAD_DIM
    kv = NSA_GROUPS * d
    width = NSA_HEADS * d
    pts = np.cumsum([width] + [kv] * 6 + [3 * NSA_HEADS, width])
    q, kc, vc, ks, vs, kw, vw, gate, z = [w_in[:, a:b] for a, b in zip([0] + list(pts[:-1]), pts)]
    grp = lambda w, g: w[:, g * d:(g + 1) * d]
    dup = lambda w: jnp.concatenate(
        [jnp.concatenate([grp(w, g), grp(w, g)], axis=1) for g in range(NSA_GROUPS)], axis=1)
    gate_pad = jnp.pad(gate, ((0, 0), (0, LANES - gate.shape[1])))
    parts = [("q", q * (d ** -0.5)), ("z", z), ("vsel", dup(vs)), ("vwin", dup(vw)),
             ("gate", gate_pad)]
    col, off = {}, 0
    for name, w in parts:
        col[name] = off
        off += w.shape[1]
    w_main = jnp.concatenate([w for _, w in parts], axis=1).astype(BF16)
    w_keys_t = jnp.concatenate([ks, kw], axis=1).T.astype(BF16)
    w_cmp = jnp.concatenate(
        [jnp.concatenate([grp(kc, g), grp(vc, g)], axis=1) for g in range(NSA_GROUPS)],
        axis=1).astype(BF16)
    return w_main, w_cmp, w_keys_t, col


def _compress_weight_layout(pe_k, w1_k, w2_k, pe_v, w1_v, w2_v):
    d, hdn = HEAD_DIM, CMP_HIDDEN
    pe_kv = jnp.concatenate([pe_k, pe_v], axis=1)
    w1k = w1_k.reshape(CMP_BLOCK, d, hdn)
    w1v = w1_v.reshape(CMP_BLOCK, d, hdn)
    zero = jnp.zeros_like(w1k)
    w1_kv = jnp.concatenate([jnp.concatenate([w1k, zero], axis=2),
                             jnp.concatenate([zero, w1v], axis=2)], axis=1).astype(BF16)
    z2 = jnp.zeros((hdn, 2 * d), w2_k.dtype)
    w2_kv = jnp.concatenate([jnp.concatenate([w2_k, w2_k, z2], axis=1),
                             jnp.concatenate([z2, w2_v, w2_v], axis=1)], axis=0).astype(BF16)
    return pe_kv, w1_kv, w2_kv


def _selection_constants(seq):
    n_cmp_pad = seq // CMP_STRIDE
    n_cmp = (seq - CMP_BLOCK) // CMP_STRIDE + 1
    n_sel = seq // SEL_BLOCK
    cmp_lo = np.arange(n_cmp_pad) * CMP_STRIDE
    sel_lo = np.arange(n_sel) * SEL_BLOCK
    overlap = np.maximum(np.minimum(cmp_lo[:, None] + CMP_BLOCK, sel_lo[None, :] + SEL_BLOCK)
                         - np.maximum(cmp_lo[:, None], sel_lo[None, :]), 0).astype(np.float32) / CMP_BLOCK
    overlap[n_cmp:] = 0.0
    ovt = np.zeros((LANES, n_cmp_pad), np.float32)
    ovt[:n_sel] = overlap.T
    et = np.zeros((LANES, seq), np.float32)
    et[np.arange(seq) // SEL_BLOCK, np.arange(seq)] = 1.0
    n_pairs = NSA_HPG // 2
    gsel = np.zeros((NSA_GROUPS, LANES, 3 * n_pairs * LANES), np.float32)
    for g in range(NSA_GROUPS):
        for pr in range(n_pairs):
            for i in range(3):
                for n in range(LANES):
                    head = g * NSA_HPG + 2 * pr + n // HEAD_DIM
                    gsel[g, 3 * head + i, (pr * 3 + i) * LANES + n] = 1.0
    return jnp.asarray(ovt), jnp.asarray(et, dtype=BF16), jnp.asarray(gsel, dtype=BF16)


def kernel(x, rel_bias_table, norm_pre, norm_post, nsa_w_in, nsa_cmp_pe_k, nsa_cmp_w1_k, nsa_cmp_w2_k,
           nsa_cmp_pe_v, nsa_cmp_w1_v, nsa_cmp_w2_v, nsa_w_out, diff_w_in, diff_lambda_q1,
           diff_lambda_k1, diff_lambda_q2, diff_lambda_k2, diff_subln, diff_w_out):
    b, s, d = x.shape
    n = b * s
    assert d == D_MODEL and s % TK == 0 and s // SEL_BLOCK <= LANES and s // CMP_STRIDE == LANES
    dn, bc = _bias_tiles(rel_bias_table, s)
    ovt, et, gsel = _selection_constants(s)
    x2 = x.reshape(n, d)
    for i in range(DEPTH):
        j = i // 2
        if i % 2 == 0:
            w_main, w_cmp, w_keys_t, col = _nsa_weight_layout(nsa_w_in[j])
            p_main, p_cmp, keys_t = _norm_proj(x2, norm_pre[i], s, [w_main, w_cmp], [BF16, F32],
                                               [w_keys_t])
            kvc = _compress(p_cmp.reshape(b, s, -1),
                            *_compress_weight_layout(nsa_cmp_pe_k[j], nsa_cmp_w1_k[j], nsa_cmp_w2_k[j],
                                                     nsa_cmp_pe_v[j], nsa_cmp_w1_v[j], nsa_cmp_w2_v[j]))
            o = _nsa_attention(p_main.reshape(b, s, -1), keys_t, kvc, dn, bc, ovt, et, gsel, col)
            w_out = nsa_w_out[j]
        else:
            lambda_init = 0.8 - 0.6 * math.exp(-0.3 * i)
            w = diff_w_in[j]
            w_main = jnp.concatenate([w[:, :d] * (HEAD_DIM ** -0.5), w[:, 2 * d:]], axis=1).astype(BF16)
            p_main, keys_t = _norm_proj(x2, norm_pre[i], s, [w_main], [BF16],
                                        [w[:, d:2 * d].T.astype(BF16)])
            o = _diff_attention(p_main.reshape(b, s, -1), keys_t, dn, diff_lambda_q1[j],
                                diff_lambda_k1[j], diff_lambda_q2[j], diff_lambda_k2[j],
                                diff_subln[j], lambda_init)
            w_out = diff_w_out[j]
        x2 = _out_proj(o.reshape(n, -1), w_out.astype(BF16), x2, norm_post[i])
    return x2.reshape(b, s, d)
```

```python
import functools
import math

import jax
import jax.numpy as jnp
import numpy as np
from jax import lax
from jax.experimental import pallas as pl
from jax.experimental.pallas import tpu as pltpu

F32 = jnp.float32
BF16 = jnp.bfloat16

D_MODEL = 1024
DEPTH = 2
REL_BUCKETS = 32
REL_MAX_DIST = 128
NSA_HEADS = 16
NSA_GROUPS = 4
NSA_HPG = NSA_HEADS // NSA_GROUPS
HEAD_DIM = 64
CMP_BLOCK = 32
CMP_STRIDE = 16
CMP_HIDDEN = 128
SEL_BLOCK = 64
SEL_TOPK = 8
WINDOW = 512
DIFF_HEADS = 8
DIFF_VDIM = 128
NEG = -1e30
BIG = 1e9
EPS = 1e-6

LANES = 128
TQ = 128
TK = 2 * LANES
N_BIAS_KINDS = 5
VMEM_LIMIT = 52 * 1024 * 1024


def _sigmoid(x):
    return 1.0 / (1.0 + jnp.exp(-x))


def _nt_dot(a, b, precision=None):
    return lax.dot_general(a, b, (((1,), (1,)), ((), ())), precision=precision,
                           preferred_element_type=F32)


def _resident(shape, index_map):
    return pl.BlockSpec(shape, index_map, pipeline_mode=pl.Buffered(1))


def _norm_proj_kernel(x_ref, g_ref, *refs, n_w, n_wt):
    w_refs, wt_refs, o_refs = refs[:n_w], refs[n_w:n_w + n_wt], refs[n_w + n_wt:]
    x = x_ref[...]
    u = x * lax.rsqrt(jnp.mean(x * x, axis=-1, keepdims=True) + EPS) * g_ref[...]
    ub = u.astype(BF16)
    for w_ref, o_ref in zip(w_refs, o_refs[:n_w]):
        o_ref[...] = jnp.dot(ub, w_ref[...], preferred_element_type=F32).astype(o_ref.dtype)
    for wt_ref, o_ref in zip(wt_refs, o_refs[n_w:]):
        o_ref[0] = _nt_dot(wt_ref[...], ub).astype(o_ref.dtype)


def _norm_proj(x2, g, seq, weights, out_dtypes, weights_t=(), tm=256):
    n, d = x2.shape
    n_st = seq // tm
    in_specs = [pl.BlockSpec((tm, d), lambda i: (i, 0)), pl.BlockSpec((1, d), lambda i: (0, 0))]
    out_specs, out_shape = [], []
    for w, dt in zip(weights, out_dtypes):
        c = w.shape[1]
        in_specs.append(_resident((d, c), lambda i: (0, 0)))
        out_specs.append(pl.BlockSpec((tm, c), lambda i: (i, 0)))
        out_shape.append(jax.ShapeDtypeStruct((n, c), dt))
    for wt in weights_t:
        c = wt.shape[0]
        in_specs.append(_resident((c, d), lambda i: (0, 0)))
        out_specs.append(pl.BlockSpec((1, c, tm), lambda i: (i // n_st, 0, i % n_st)))
        out_shape.append(jax.ShapeDtypeStruct((n // seq, c, seq), BF16))
    return pl.pallas_call(
        functools.partial(_norm_proj_kernel, n_w=len(weights), n_wt=len(weights_t)),
        grid=(n // tm,),
        in_specs=in_specs,
        out_specs=out_specs,
        out_shape=out_shape,
        compiler_params=pltpu.CompilerParams(
            dimension_semantics=("arbitrary",), vmem_limit_bytes=VMEM_LIMIT),
        name="norm_proj",
    )(x2, g.reshape(1, d), *weights, *weights_t)


def _out_proj_kernel(o_ref, w_ref, x_ref, g_ref, y_ref):
    y = jnp.dot(o_ref[...], w_ref[...], preferred_element_type=F32)
    r = y * lax.rsqrt(jnp.mean(y * y, axis=-1, keepdims=True) + EPS) * g_ref[...]
    y_ref[...] = x_ref[...] + r


def _out_proj(o2, w, x2, g, tm=512):
    n, d = x2.shape
    return pl.pallas_call(
        _out_proj_kernel,
        grid=(n // tm,),
        in_specs=[pl.BlockSpec((tm, o2.shape[1]), lambda i: (i, 0)),
                  _resident(w.shape, lambda i: (0, 0)),
                  pl.BlockSpec((tm, d), lambda i: (i, 0)),
                  pl.BlockSpec((1, d), lambda i: (0, 0))],
        out_specs=pl.BlockSpec((tm, d), lambda i: (i, 0)),
        out_shape=jax.ShapeDtypeStruct((n, d), F32),
        compiler_params=pltpu.CompilerParams(
            dimension_semantics=("arbitrary",), vmem_limit_bytes=VMEM_LIMIT),
        name="out_proj",
    )(o2, w, x2, g.reshape(1, d))


def _bucket_starts(max_dist):
    max_exact = REL_BUCKETS // 2
    n = np.arange(max_dist)
    nf = np.maximum(n, 1).astype(np.float32)
    large = max_exact + (np.log(nf / max_exact) / np.float32(math.log(REL_MAX_DIST / max_exact))
                         * (REL_BUCKETS - max_exact)).astype(np.int32)
    bucket = np.where(n < max_exact, n, np.minimum(large, REL_BUCKETS - 1))
    assert (np.diff(bucket) >= 0).all()
    return [int(np.argmax(bucket >= b)) if (bucket >= b).any() else None
            for b in range(REL_BUCKETS)]


def _bias_kernel(tbl_ref, dn_ref, bc_ref):
    h = pl.program_id(0)
    c31 = tbl_ref[REL_BUCKETS - 1, h]
    starts = _bucket_starts(bc_ref.shape[1])

    def rel_bias(dist):
        n = jnp.maximum(dist, 0)
        out = jnp.full(dist.shape, tbl_ref[0, h], F32)
        for b in range(1, REL_BUCKETS):
            if starts[b] is not None:
                out = jnp.where(n >= starts[b], tbl_ref[b, h], out)
        return out - c31

    i = lax.broadcasted_iota(jnp.int32, (TQ, LANES), 0)
    j = lax.broadcasted_iota(jnp.int32, (TQ, LANES), 1)
    dn_ref[0, 0] = jnp.where(j <= i, rel_bias(i - j), NEG)
    dn_ref[0, 1] = rel_bias(LANES + i - j)
    dn_ref[0, 2] = jnp.zeros((TQ, LANES), F32)
    dn_ref[0, 3] = jnp.where(j > i, 0.0, NEG)
    dn_ref[0, 4] = jnp.full((TQ, LANES), NEG, F32)

    n_cmp = (bc_ref.shape[1] - CMP_BLOCK) // CMP_STRIDE + 1

    def body(r, carry):
        t = r * TQ + i
        d = t - (j * CMP_STRIDE + CMP_BLOCK - 1)
        valid = (d >= 0) & (j < n_cmp)
        bc_ref[0, pl.ds(pl.multiple_of(r * TQ, TQ), TQ), :] = jnp.where(valid, rel_bias(d), NEG)
        return carry

    lax.fori_loop(0, bc_ref.shape[1] // TQ, body, 0)


def _bias_tiles(table, seq):
    n_maps = table.shape[1]
    return pl.pallas_call(
        _bias_kernel,
        grid=(n_maps,),
        in_specs=[pl.BlockSpec(memory_space=pltpu.SMEM)],
        out_specs=[pl.BlockSpec((1, N_BIAS_KINDS, TQ, LANES), lambda h: (h, 0, 0, 0)),
                   pl.BlockSpec((1, seq, LANES), lambda h: (h, 0, 0))],
        out_shape=[jax.ShapeDtypeStruct((n_maps, N_BIAS_KINDS, TQ, LANES), F32),
                   jax.ShapeDtypeStruct((n_maps, seq, LANES), F32)],
        compiler_params=pltpu.CompilerParams(dimension_semantics=("arbitrary",)),
        name="bias_tiles",
    )(table)


def _compress_kernel(x_ref, pe_ref, w1_ref, w2_ref, o_ref, xp_ref):
    seq = x_ref.shape[1]
    n_chunks = seq // CMP_STRIDE
    xp_ref[0:seq, :] = x_ref[0]
    xp_ref[seq:, :] = jnp.zeros((xp_ref.shape[0] - seq, xp_ref.shape[1]), F32)
    hid = jnp.zeros((n_chunks, 2 * CMP_HIDDEN), F32)
    for l in range(CMP_BLOCK):
        a = xp_ref[pl.ds(l, n_chunks, stride=CMP_STRIDE), :]
        hid = hid + jnp.dot((a + pe_ref[l:l + 1, :]).astype(BF16), w1_ref[l],
                            preferred_element_type=F32)
    hid = hid * _sigmoid(hid)
    o_ref[0, 0] = jnp.dot(hid.astype(BF16), w2_ref[...], preferred_element_type=F32).astype(BF16)


def _compress(cmp_in, pe_kv, w1_kv, w2_kv):
    b, s, _ = cmp_in.shape
    n_chunks = s // CMP_STRIDE
    return pl.pallas_call(
        _compress_kernel,
        grid=(b, NSA_GROUPS),
        in_specs=[pl.BlockSpec((1, s, LANES), lambda bi, g: (bi, 0, g)),
                  pl.BlockSpec(pe_kv.shape, lambda bi, g: (0, 0)),
                  pl.BlockSpec(w1_kv.shape, lambda bi, g: (0, 0, 0)),
                  pl.BlockSpec(w2_kv.shape, lambda bi, g: (0, 0))],
        out_specs=pl.BlockSpec((1, 1, n_chunks, 2 * LANES), lambda bi, g: (bi, g, 0, 0)),
        out_shape=jax.ShapeDtypeStruct((b, NSA_GROUPS, n_chunks, 2 * LANES), BF16),
        scratch_shapes=[pltpu.VMEM((s + CMP_BLOCK, LANES), F32)],
        compiler_params=pltpu.CompilerParams(
            dimension_semantics=("arbitrary", "arbitrary"), vmem_limit_bytes=VMEM_LIMIT),
        name="nsa_compress",
    )(cmp_in, pe_kv, w1_kv, w2_kv)


def _tile_kind(delta, window=False):
    if delta < 0 or (window and delta > WINDOW // LANES):
        return 4
    if window and delta == WINDOW // LANES:
        return 3
    return min(delta, 2)


def _tile_kinds(row_blocks, steps, window=False):
    n_hf = TK // LANES
    return [[[(slot, _tile_kind(q_tile - (kp * n_hf + hf), window)) for hf in range(n_hf)]
             for slot, q_tile in row_blocks] for kp in steps]


def _score_steps(qx, kt_tiles, kinds, bias_ref, st):
    n_rb = qx.shape[0] // TQ
    n_hf = TK // LANES
    st["mx"] = [None] * n_rb
    st["sc"] = []
    for i, load_kt in enumerate(kt_tiles):
        s = jnp.dot(qx, load_kt(), preferred_element_type=F32)
        step = []
        for rb in range(n_rb):
            tiles = []
            for hf in range(n_hf):
                slot, kind = kinds[i][rb][hf]
                if kind == 4:
                    tiles.append(None)
                    continue
                t = s[rb * TQ:(rb + 1) * TQ, hf * LANES:(hf + 1) * LANES]
                if kind != 2:
                    t = t + bias_ref[slot, kind]
                tiles.append(t)
                st["mx"][rb] = t if st["mx"][rb] is None else jnp.maximum(st["mx"][rb], t)
            step.append(tiles)
        st["sc"].append(step)
        yield


def _weight_steps(v_tiles, st):
    n_rb = len(st["mx"])
    mb = [jnp.broadcast_to(jnp.max(v, axis=-1, keepdims=True), (TQ, LANES)) for v in st["mx"]]
    st["acc"] = jnp.zeros((n_rb * TQ, 2 * LANES), F32)
    for i, load_v in enumerate(v_tiles):
        p = jnp.concatenate(
            [jnp.concatenate([jnp.zeros((TQ, LANES), F32) if t is None else jnp.exp(t - mb[rb])
                              for t in st["sc"][i][rb]], axis=1) for rb in range(n_rb)], axis=0)
        st["acc"] = st["acc"] + jnp.dot(p.astype(BF16), load_v(), preferred_element_type=F32)
        yield


def _chain(*gens):
    for g in gens:
        yield from g


def _pipeline_tiles(order, n_steps, setup, score_steps, weight_steps, finish):
    n = len(order)
    state = {}
    qk_at = [0, 0]

    def qk_stream():
        state[order[0]] = setup(order[0])
        for i, x in enumerate(order):
            if i + 1 < n:
                state[order[i + 1]] = setup(order[i + 1])
            qk_at[:] = [i, 0]
            for _ in score_steps(x, state[x]):
                qk_at[1] += 1
                yield
        qk_at[:] = [n, 0]

    def pv_stream():
        for i, x in enumerate(order):
            while not (qk_at[0] > i + 1 or qk_at[0] >= n
                       or (qk_at[0] == i + 1 and 2 * qk_at[1] >= n_steps(order[i + 1]))):
                yield
            yield from weight_steps(x, state[x])
            finish(x, state.pop(x))

    live = [qk_stream(), pv_stream()]
    while live:
        for g in list(live):
            try:
                next(g)
            except StopIteration:
                live.remove(g)


def _tile_order(n):
    return list(range(0, n, 2)) + list(range(n - 1 - n % 2, 0, -2))


def _pad_contraction(a, axis):
    shape = list(a.shape)
    shape[axis] = TK - a.shape[axis]
    return jnp.concatenate([a, jnp.zeros(shape, a.dtype)], axis=axis)


def _with_ones(v_tile):
    return jnp.concatenate([v_tile, jnp.ones(v_tile.shape, v_tile.dtype)], axis=1)


def _nsa_kernel(q_ref, kst_ref, kwt_ref, vs_ref, vw_ref, kvc_ref, gate_ref, z_ref, dn_ref, bc_ref,
                ovt_ref, et_ref, gsel_ref, o_ref, *, tq):
    seq = q_ref.shape[1]
    n_rb = tq // TQ
    n_sel = seq // SEL_BLOCK

    keys = lambda kp: slice(kp * TK, (kp + 1) * TK)

    n_pairs = NSA_HPG // 2
    tiles = {}

    def tile_setup(x):
        lane = lax.broadcasted_iota(jnp.int32, (tq, LANES), 1)
        lo_half = lane < HEAD_DIM
        blk = lax.broadcasted_iota(jnp.int32, (n_sel, tq), 0)
        tok = lax.broadcasted_iota(jnp.int32, (n_sel, tq), 1)
        row = lax.broadcasted_iota(jnp.int32, (tq, 1), 0)
        kcc = kvc_ref[0, 0, :, 0:LANES]
        vcc = kvc_ref[0, 0, :, LANES:2 * LANES]
        rows = slice(x * tq, (x + 1) * tq)
        q = q_ref[0, rows, :]
        qx = []
        for j in range(NSA_HPG):
            slab = q[:, (j // 2) * LANES:(j // 2 + 1) * LANES]
            keep = lo_half if j % 2 == 0 else jnp.logical_not(lo_half)
            qx.append(jnp.where(keep, slab, jnp.zeros_like(slab)))

        any_valid = (x * tq + row >= CMP_BLOCK - 1).astype(F32)
        p_sum = jnp.zeros((tq, LANES), F32)
        o_c = []
        for j in range(NSA_HPG):
            sj = _nt_dot(qx[j], kcc) + bc_ref[j, rows, :]
            pj = jnp.exp(sj - jnp.max(sj, axis=-1, keepdims=True))
            pv = jnp.dot(pj.astype(BF16), _with_ones(vcc), preferred_element_type=F32)
            inv = any_valid / pv[:, LANES:]
            p_sum = p_sum + pj * inv
            o_c.append(pv[:, :LANES] * inv)

        imp = _nt_dot(ovt_ref[...], p_sum, precision=lax.Precision.HIGHEST)[0:n_sel, :]
        cur = jnp.right_shift(x * tq + tok, int(math.log2(SEL_BLOCK)))
        forced = (blk == 0) | (blk == cur) | (blk == cur - 1)
        val = jnp.where(forced, BIG, jnp.where(blk > cur, -BIG, imp))
        rank = jnp.zeros((n_sel, tq), F32)
        for other in range(min(n_sel, (x * tq + tq - 1) // SEL_BLOCK + 1)):
            r = val[other:other + 1, :]
            ahead = (r > val) | ((r == val) & (blk > other))
            rank = rank + jnp.where(ahead, 1.0, 0.0)
        block_mask_t = jnp.where(rank < float(min(SEL_TOPK, n_sel)), 0.0, NEG)
        block_mask = jnp.concatenate(
            [block_mask_t, jnp.zeros((LANES - n_sel, tq), F32)], axis=0).T.astype(BF16)

        return dict(qx=qx, block_mask=block_mask, o_c=o_c)

    order = _tile_order(seq // tq)
    started = []

    def setup(unit):
        x, pr = unit
        if not started:
            started.append(True)
            for y in order:
                tiles[y] = tile_setup(y)
        t = tiles[x] if pr + 1 < n_pairs else tiles.pop(x)
        q_pair = jnp.concatenate(t["qx"][2 * pr:2 * pr + 2], axis=0)
        q_sel = jnp.concatenate([q_pair, jnp.concatenate([t["block_mask"]] * 2, axis=0)], axis=1)
        return dict(q_win=_pad_contraction(q_pair, 1), q_sel=q_sel,
                    o_c=t["o_c"][2 * pr:2 * pr + 2], win={}, sel={})

    def steps_of(unit):
        x = unit[0]
        sel_steps = range((x * tq + tq - 1) // TK + 1)
        win_steps = range(max(0, (x * tq - (WINDOW - 1)) // TK), sel_steps[-1] + 1)
        return win_steps, sel_steps

    both = lambda t: jnp.concatenate([t, t], axis=0)

    def score_steps(unit, st):
        x, pr = unit
        win_steps, sel_steps = steps_of(unit)
        row_blocks = [(j, x * n_rb + r) for j in (2 * pr, 2 * pr + 1) for r in range(n_rb)]
        return _chain(
            _score_steps(st["q_win"],
                         [(lambda kp=kp: _pad_contraction(both(kwt_ref[0, :, keys(kp)]), 0))
                          for kp in win_steps],
                         _tile_kinds(row_blocks, win_steps, window=True), dn_ref, st["win"]),
            _score_steps(st["q_sel"],
                         [(lambda kp=kp: jnp.concatenate(
                             [both(kst_ref[0, :, keys(kp)]), et_ref[:, keys(kp)]], axis=0))
                          for kp in sel_steps],
                         _tile_kinds(row_blocks, sel_steps), dn_ref, st["sel"]))

    def weight_steps(unit, st):
        win_steps, sel_steps = steps_of(unit)
        return _chain(
            _weight_steps([(lambda kp=kp: _with_ones(vw_ref[0, keys(kp), :])) for kp in win_steps],
                          st["win"]),
            _weight_steps([(lambda kp=kp: _with_ones(vs_ref[0, keys(kp), :])) for kp in sel_steps],
                          st["sel"]))

    def finish(unit, st):
        x, pr = unit
        rows = slice(x * tq, (x + 1) * tq)
        cols = slice(pr * LANES, (pr + 1) * LANES)
        lo_half = lax.broadcasted_iota(jnp.int32, (tq, LANES), 1) < HEAD_DIM
        gates = _sigmoid(jnp.dot(gate_ref[0, rows, :], gsel_ref[0, :, 3 * pr * LANES:3 * (pr + 1) * LANES],
                                 preferred_element_type=F32))
        merged = lambda a: jnp.where(lo_half, a[:tq, :], a[tq:, :])
        ratio = lambda acc: merged(acc[:, :LANES] / acc[:, LANES:])
        o = (gates[:, 0:LANES] * jnp.where(lo_half, st["o_c"][0], st["o_c"][1])
             + gates[:, LANES:2 * LANES] * ratio(st["sel"]["acc"])
             + gates[:, 2 * LANES:3 * LANES] * ratio(st["win"]["acc"]))
        z = z_ref[0, rows, cols].astype(F32)
        o_ref[0, rows, cols] = (o * (z * _sigmoid(z))).astype(o_ref.dtype)

    units = [(x, pr) for x in order for pr in range(n_pairs)]
    _pipeline_tiles(units, lambda u: sum(len(s) for s in steps_of(u)),
                    setup, score_steps, weight_steps, finish)


def _nsa_attention(p3, kt3, kvc, dn, bc, ovt, et, gsel, col, tq=2 * TQ):
    b, s, _ = p3.shape
    wide = 2 * LANES

    def slab(first_lane_block, per_group, width=LANES):
        return pl.BlockSpec((1, s, width), lambda g, bi: (bi, 0, first_lane_block + per_group * g))

    in_specs = [
        slab(col["q"] // wide, 1, wide),
        pl.BlockSpec((1, HEAD_DIM, s), lambda g, bi: (bi, g, 0)),
        pl.BlockSpec((1, HEAD_DIM, s), lambda g, bi: (bi, NSA_GROUPS + g, 0)),
        slab(col["vsel"] // LANES, 1), slab(col["vwin"] // LANES, 1),
        pl.BlockSpec((1, 1, kvc.shape[2], wide), lambda g, bi: (bi, g, 0, 0)),
        slab(col["gate"] // LANES, 0),
        slab(col["z"] // wide, 1, wide),
        _resident((NSA_HPG, N_BIAS_KINDS, TQ, LANES), lambda g, bi: (g, 0, 0, 0)),
        _resident((NSA_HPG, s, LANES), lambda g, bi: (g, 0, 0)),
        _resident(ovt.shape, lambda g, bi: (0, 0)),
        _resident(et.shape, lambda g, bi: (0, 0)),
        _resident((1,) + gsel.shape[1:], lambda g, bi: (g, 0, 0)),
    ]
    return pl.pallas_call(
        functools.partial(_nsa_kernel, tq=tq),
        grid=(NSA_GROUPS, b),
        in_specs=in_specs,
        out_specs=pl.BlockSpec((1, s, wide), lambda g, bi: (bi, 0, g)),
        out_shape=jax.ShapeDtypeStruct((b, s, NSA_GROUPS * wide), BF16),
        compiler_params=pltpu.CompilerParams(
            dimension_semantics=("arbitrary", "arbitrary"), vmem_limit_bytes=VMEM_LIMIT),
        name="nsa_attention",
    )(p3, kt3, kt3, p3, p3, kvc, p3, p3, dn, bc, ovt, et, gsel)


def _diff_kernel(q_ref, kt_ref, v_ref, z_ref, dn_ref, lq1_ref, lk1_ref, lq2_ref, lk2_ref, sub_ref,
                 o_ref, *, lambda_init, tq):
    seq = q_ref.shape[1]
    n_rb = tq // TQ
    lane = lax.broadcasted_iota(jnp.int32, (tq, LANES), 1)
    lam = (jnp.exp(jnp.sum(lq1_ref[...] * lk1_ref[...], axis=-1, keepdims=True))
           - jnp.exp(jnp.sum(lq2_ref[...] * lk2_ref[...], axis=-1, keepdims=True)) + lambda_init)
    keys = lambda kp: slice(kp * TK, (kp + 1) * TK)
    steps_of = lambda x: range((x * tq + tq - 1) // TK + 1)

    def setup(x):
        q = q_ref[0, x * tq:(x + 1) * tq, :]
        zero = jnp.zeros_like(q)
        return dict(q=_pad_contraction(
            jnp.concatenate([jnp.where(lane < HEAD_DIM, q, zero),
                             jnp.where(lane >= HEAD_DIM, q, zero)], axis=0), 1))

    def score_steps(x, st):
        row_blocks = [(m, x * n_rb + r) for m in range(2) for r in range(n_rb)]
        return _score_steps(st["q"], [(lambda kp=kp: _pad_contraction(kt_ref[0, :, keys(kp)], 0))
                                      for kp in steps_of(x)],
                            _tile_kinds(row_blocks, steps_of(x)), dn_ref, st)

    def weight_steps(x, st):
        return _weight_steps([(lambda kp=kp: _with_ones(v_ref[0, keys(kp), :]))
                              for kp in steps_of(x)], st)

    def finish(x, st):
        rows = slice(x * tq, (x + 1) * tq)
        a = st["acc"][:, :LANES] / st["acc"][:, LANES:]
        o = a[:tq, :] - lam * a[tq:, :]
        o = o * lax.rsqrt(jnp.mean(o * o, axis=-1, keepdims=True) + EPS) * sub_ref[...]
        o = o * (1.0 - lambda_init)
        z = z_ref[0, rows, :].astype(F32)
        o_ref[0, rows, :] = (o * (z * _sigmoid(z))).astype(o_ref.dtype)

    _pipeline_tiles(_tile_order(seq // tq), lambda x: len(steps_of(x)),
                    setup, score_steps, weight_steps, finish)


def _diff_attention(p3, kt3, dn, lq1, lk1, lq2, lk2, subln, lambda_init, tq=2 * TQ):
    b, s, _ = p3.shape
    h = DIFF_HEADS
    vec = lambda a: a.reshape(1, -1)
    small = pl.BlockSpec((1, HEAD_DIM), lambda bi, hi: (0, 0))
    slab = lambda first: pl.BlockSpec((1, s, LANES), lambda bi, hi: (bi, 0, first + hi))
    in_specs = [
        slab(0),
        pl.BlockSpec((1, LANES, s), lambda bi, hi: (bi, hi, 0)),
        slab(h), slab(2 * h),
        pl.BlockSpec((2, N_BIAS_KINDS, TQ, LANES), lambda bi, hi: (hi, 0, 0, 0)),
        small, small, small, small,
        pl.BlockSpec((1, DIFF_VDIM), lambda bi, hi: (0, hi)),
    ]
    return pl.pallas_call(
        functools.partial(_diff_kernel, lambda_init=lambda_init, tq=tq),
        grid=(b, h),
        in_specs=in_specs,
        out_specs=slab(0),
        out_shape=jax.ShapeDtypeStruct((b, s, h * DIFF_VDIM), BF16),
        compiler_params=pltpu.CompilerParams(
            dimension_semantics=("arbitrary", "arbitrary"), vmem_limit_bytes=VMEM_LIMIT),
        name="diff_attention",
    )(p3, kt3, p3, p3, dn, vec(lq1), vec(lk1), vec(lq2), vec(lk2), vec(subln))


def _nsa_weight_layout(w_in):
    d = HEAD_DIM
    kv = NSA_GROUPS * d
    width = NSA_HEADS * d
    pts = np.cumsum([width] + [kv] * 6 + [3 * NSA_HEADS, width])
    q, kc, vc, ks, vs, kw, vw, gate, z = [w_in[:, a:b] for a, b in zip([0] + list(pts[:-1]), pts)]
    grp = lambda w, g: w[:, g * d:(g + 1) * d]
    dup = lambda w: jnp.concatenate(
        [jnp.concatenate([grp(w, g), grp(w, g)], axis=1) for g in range(NSA_GROUPS)], axis=1)
    gate_pad = jnp.pad(gate, ((0, 0), (0, LANES - gate.shape[1])))
    parts = [("q", q * (d ** -0.5)), ("z", z), ("vsel", dup(vs)), ("vwin", dup(vw)),
             ("gate", gate_pad)]
    col, off = {}, 0
    for name, w in parts:
        col[name] = off
        off += w.shape[1]
    w_main = jnp.concatenate([w for _, w in parts], axis=1).astype(BF16)
    w_keys_t = jnp.concatenate([ks, kw], axis=1).T.astype(BF16)
    w_cmp = jnp.concatenate(
        [jnp.concatenate([grp(kc, g), grp(vc, g)], axis=1) for g in range(NSA_GROUPS)],
        axis=1).astype(BF16)
    return w_main, w_cmp, w_keys_t, col


def _compress_weight_layout(pe_k, w1_k, w2_k, pe_v, w1_v, w2_v):
    d, hdn = HEAD_DIM, CMP_HIDDEN
    pe_kv = jnp.concatenate([pe_k, pe_v], axis=1)
    w1k = w1_k.reshape(CMP_BLOCK, d, hdn)
    w1v = w1_v.reshape(CMP_BLOCK, d, hdn)
    zero = jnp.zeros_like(w1k)
    w1_kv = jnp.concatenate([jnp.concatenate([w1k, zero], axis=2),
                             jnp.concatenate([zero, w1v], axis=2)], axis=1).astype(BF16)
    z2 = jnp.zeros((hdn, 2 * d), w2_k.dtype)
    w2_kv = jnp.concatenate([jnp.concatenate([w2_k, w2_k, z2], axis=1),
                             jnp.concatenate([z2, w2_v, w2_v], axis=1)], axis=0).astype(BF16)
    return pe_kv, w1_kv, w2_kv


def _selection_constants(seq):
    n_cmp_pad = seq // CMP_STRIDE
    n_cmp = (seq - CMP_BLOCK) // CMP_STRIDE + 1
    n_sel = seq // SEL_BLOCK
    cmp_lo = np.arange(n_cmp_pad) * CMP_STRIDE
    sel_lo = np.arange(n_sel) * SEL_BLOCK
    overlap = np.maximum(np.minimum(cmp_lo[:, None] + CMP_BLOCK, sel_lo[None, :] + SEL_BLOCK)
                         - np.maximum(cmp_lo[:, None], sel_lo[None, :]), 0).astype(np.float32) / CMP_BLOCK
    overlap[n_cmp:] = 0.0
    ovt = np.zeros((LANES, n_cmp_pad), np.float32)
    ovt[:n_sel] = overlap.T
    et = np.zeros((LANES, seq), np.float32)
    et[np.arange(seq) // SEL_BLOCK, np.arange(seq)] = 1.0
    n_pairs = NSA_HPG // 2
    gsel = np.zeros((NSA_GROUPS, LANES, 3 * n_pairs * LANES), np.float32)
    for g in range(NSA_GROUPS):
        for pr in range(n_pairs):
            for i in range(3):
                for n in range(LANES):
                    head = g * NSA_HPG + 2 * pr + n // HEAD_DIM
                    gsel[g, 3 * head + i, (pr * 3 + i) * LANES + n] = 1.0
    return jnp.asarray(ovt), jnp.asarray(et, dtype=BF16), jnp.asarray(gsel, dtype=BF16)


def kernel(x, rel_bias_table, norm_pre, norm_post, nsa_w_in, nsa_cmp_pe_k, nsa_cmp_w1_k, nsa_cmp_w2_k,
           nsa_cmp_pe_v, nsa_cmp_w1_v, nsa_cmp_w2_v, nsa_w_out, diff_w_in, diff_lambda_q1,
           diff_lambda_k1, diff_lambda_q2, diff_lambda_k2, diff_subln, diff_w_out):
    b, s, d = x.shape
    n = b * s
    assert d == D_MODEL and s % TK == 0 and s // SEL_BLOCK <= LANES and s // CMP_STRIDE == LANES
    dn, bc = _bias_tiles(rel_bias_table, s)
    ovt, et, gsel = _selection_constants(s)
    x2 = x.reshape(n, d)
    for i in range(DEPTH):
        j = i // 2
        if i % 2 == 0:
            w_main, w_cmp, w_keys_t, col = _nsa_weight_layout(nsa_w_in[j])
            p_main, p_cmp, keys_t = _norm_proj(x2, norm_pre[i], s, [w_main, w_cmp], [BF16, F32],
                                               [w_keys_t])
            kvc = _compress(p_cmp.reshape(b, s, -1),
                            *_compress_weight_layout(nsa_cmp_pe_k[j], nsa_cmp_w1_k[j], nsa_cmp_w2_k[j],
                                                     nsa_cmp_pe_v[j], nsa_cmp_w1_v[j], nsa_cmp_w2_v[j]))
            o = _nsa_attention(p_main.reshape(b, s, -1), keys_t, kvc, dn, bc, ovt, et, gsel, col)
            w_out = nsa_w_out[j]
        else:
            lambda_init = 0.8 - 0.6 * math.exp(-0.3 * i)
            w = diff_w_in[j]
            w_main = jnp.concatenate([w[:, :d] * (HEAD_DIM ** -0.5), w[:, 2 * d:]], axis=1).astype(BF16)
            p_main, keys_t = _norm_proj(x2, norm_pre[i], s, [w_main], [BF16],
                                        [w[:, d:2 * d].T.astype(BF16)])
            o = _diff_attention(p_main.reshape(b, s, -1), keys_t, dn, diff_lambda_q1[j],
                                diff_lambda_k1[j], diff_lambda_q2[j], diff_lambda_k2[j],
                                diff_subln[j], lambda_init)
            w_out = diff_w_out[j]
        x2 = _out_proj(o.reshape(n, -1), w_out.astype(BF16), x2, norm_post[i])
    return x2.reshape(b, s, d)
```

```python
import functools
import math

import jax
import jax.numpy as jnp
import numpy as np
from jax import lax
from jax.experimental import pallas as pl
from jax.experimental.pallas import tpu as pltpu

F32 = jnp.float32
BF16 = jnp.bfloat16

D_MODEL = 1024
DEPTH = 2
REL_BUCKETS = 32
REL_MAX_DIST = 128
NSA_HEADS = 16
NSA_GROUPS = 4
NSA_HPG = NSA_HEADS // NSA_GROUPS
HEAD_DIM = 64
CMP_BLOCK = 32
CMP_STRIDE = 16
CMP_HIDDEN = 128
SEL_BLOCK = 64
SEL_TOPK = 8
WINDOW = 512
DIFF_HEADS = 8
DIFF_VDIM = 128
NEG = -1e30
BIG = 1e9
EPS = 1e-6

LANES = 128
TQ = 128
TK = 2 * LANES
N_BIAS_KINDS = 5
VMEM_LIMIT = 52 * 1024 * 1024
LOG2E = math.log2(math.e)


def _sigmoid(x):
    return 1.0 / (1.0 + jnp.exp(-x))


def _nt_dot(a, b, precision=None):
    return lax.dot_general(a, b, (((1,), (1,)), ((), ())), precision=precision,
                           preferred_element_type=F32)


def _resident(shape, index_map):
    return pl.BlockSpec(shape, index_map, pipeline_mode=pl.Buffered(1))


def _norm_proj_kernel(x_ref, g_ref, *refs, n_w, n_wt):
    w_refs, wt_refs, o_refs = refs[:n_w], refs[n_w:n_w + n_wt], refs[n_w + n_wt:]
    x = x_ref[...]
    u = x * lax.rsqrt(jnp.mean(x * x, axis=-1, keepdims=True) + EPS) * g_ref[...]
    ub = u.astype(BF16)
    for w_ref, o_ref in zip(w_refs, o_refs[:n_w]):
        o_ref[...] = jnp.dot(ub, w_ref[...], preferred_element_type=F32).astype(o_ref.dtype)
    for wt_ref, o_ref in zip(wt_refs, o_refs[n_w:]):
        o_ref[0] = _nt_dot(wt_ref[...], ub).astype(o_ref.dtype)


def _norm_proj(x2, g, seq, weights, out_dtypes, weights_t=(), tm=256):
    n, d = x2.shape
    n_st = seq // tm
    in_specs = [pl.BlockSpec((tm, d), lambda i: (i, 0)), pl.BlockSpec((1, d), lambda i: (0, 0))]
    out_specs, out_shape = [], []
    for w, dt in zip(weights, out_dtypes):
        c = w.shape[1]
        in_specs.append(_resident((d, c), lambda i: (0, 0)))
        out_specs.append(pl.BlockSpec((tm, c), lambda i: (i, 0)))
        out_shape.append(jax.ShapeDtypeStruct((n, c), dt))
    for wt in weights_t:
        c = wt.shape[0]
        in_specs.append(_resident((c, d), lambda i: (0, 0)))
        out_specs.append(pl.BlockSpec((1, c, tm), lambda i: (i // n_st, 0, i % n_st)))
        out_shape.append(jax.ShapeDtypeStruct((n // seq, c, seq), BF16))
    return pl.pallas_call(
        functools.partial(_norm_proj_kernel, n_w=len(weights), n_wt=len(weights_t)),
        grid=(n // tm,),
        in_specs=in_specs,
        out_specs=out_specs,
        out_shape=out_shape,
        compiler_params=pltpu.CompilerParams(
            dimension_semantics=("arbitrary",), vmem_limit_bytes=VMEM_LIMIT),
        name="norm_proj",
    )(x2, g.reshape(1, d), *weights, *weights_t)


def _out_proj_kernel(o_ref, w_ref, x_ref, g_ref, y_ref):
    y = jnp.dot(o_ref[...], w_ref[...], preferred_element_type=F32)
    r = y * lax.rsqrt(jnp.mean(y * y, axis=-1, keepdims=True) + EPS) * g_ref[...]
    y_ref[...] = x_ref[...] + r


def _out_proj(o2, w, x2, g, tm=512):
    n, d = x2.shape
    return pl.pallas_call(
        _out_proj_kernel,
        grid=(n // tm,),
        in_specs=[pl.BlockSpec((tm, o2.shape[1]), lambda i: (i, 0)),
                  _resident(w.shape, lambda i: (0, 0)),
                  pl.BlockSpec((tm, d), lambda i: (i, 0)),
                  pl.BlockSpec((1, d), lambda i: (0, 0))],
        out_specs=pl.BlockSpec((tm, d), lambda i: (i, 0)),
        out_shape=jax.ShapeDtypeStruct((n, d), F32),
        compiler_params=pltpu.CompilerParams(
            dimension_semantics=("arbitrary",), vmem_limit_bytes=VMEM_LIMIT),
        name="out_proj",
    )(o2, w, x2, g.reshape(1, d))


def _bucket_starts(max_dist):
    max_exact = REL_BUCKETS // 2
    n = np.arange(max_dist)
    nf = np.maximum(n, 1).astype(np.float32)
    large = max_exact + (np.log(nf / max_exact) / np.float32(math.log(REL_MAX_DIST / max_exact))
                         * (REL_BUCKETS - max_exact)).astype(np.int32)
    bucket = np.where(n < max_exact, n, np.minimum(large, REL_BUCKETS - 1))
    assert (np.diff(bucket) >= 0).all()
    return [int(np.argmax(bucket >= b)) if (bucket >= b).any() else None
            for b in range(REL_BUCKETS)]


def _bias_kernel(tbl_ref, dn_ref, bc_ref):
    h = pl.program_id(0)
    c31 = tbl_ref[REL_BUCKETS - 1, h]
    starts = _bucket_starts(bc_ref.shape[1])

    def rel_bias(dist):
        n = jnp.maximum(dist, 0)
        out = jnp.full(dist.shape, tbl_ref[0, h], F32)
        for b in range(1, REL_BUCKETS):
            if starts[b] is not None:
                out = jnp.where(n >= starts[b], tbl_ref[b, h], out)
        return (out - c31) * LOG2E

    i = lax.broadcasted_iota(jnp.int32, (TQ, LANES), 0)
    j = lax.broadcasted_iota(jnp.int32, (TQ, LANES), 1)
    dn_ref[0, 0] = jnp.where(j <= i, rel_bias(i - j), NEG)
    dn_ref[0, 1] = rel_bias(LANES + i - j)
    dn_ref[0, 2] = jnp.zeros((TQ, LANES), F32)
    dn_ref[0, 3] = jnp.where(j > i, 0.0, NEG)
    dn_ref[0, 4] = jnp.full((TQ, LANES), NEG, F32)

    n_cmp = (bc_ref.shape[1] - CMP_BLOCK) // CMP_STRIDE + 1

    def body(r, carry):
        t = r * TQ + i
        d = t - (j * CMP_STRIDE + CMP_BLOCK - 1)
        valid = (d >= 0) & (j < n_cmp)
        bc_ref[0, pl.ds(pl.multiple_of(r * TQ, TQ), TQ), :] = jnp.where(valid, rel_bias(d), NEG)
        return carry

    lax.fori_loop(0, bc_ref.shape[1] // TQ, body, 0)


def _bias_tiles(table, seq):
    n_maps = table.shape[1]
    return pl.pallas_call(
        _bias_kernel,
        grid=(n_maps,),
        in_specs=[pl.BlockSpec(memory_space=pltpu.SMEM)],
        out_specs=[pl.BlockSpec((1, N_BIAS_KINDS, TQ, LANES), lambda h: (h, 0, 0, 0)),
                   pl.BlockSpec((1, seq, LANES), lambda h: (h, 0, 0))],
        out_shape=[jax.ShapeDtypeStruct((n_maps, N_BIAS_KINDS, TQ, LANES), F32),
                   jax.ShapeDtypeStruct((n_maps, seq, LANES), F32)],
        compiler_params=pltpu.CompilerParams(dimension_semantics=("arbitrary",)),
        name="bias_tiles",
    )(table)


def _compress_kernel(x_ref, pe_ref, w1_ref, w2_ref, o_ref, xp_ref):
    seq = x_ref.shape[1]
    n_chunks = seq // CMP_STRIDE
    xp_ref[0:seq, :] = x_ref[0]
    xp_ref[seq:, :] = jnp.zeros((xp_ref.shape[0] - seq, xp_ref.shape[1]), F32)
    hid = jnp.zeros((n_chunks, 2 * CMP_HIDDEN), F32)
    for l in range(CMP_BLOCK):
        a = xp_ref[pl.ds(l, n_chunks, stride=CMP_STRIDE), :]
        hid = hid + jnp.dot((a + pe_ref[l:l + 1, :]).astype(BF16), w1_ref[l],
                            preferred_element_type=F32)
    hid = hid * _sigmoid(hid)
    o_ref[0, 0] = jnp.dot(hid.astype(BF16), w2_ref[...], preferred_element_type=F32).astype(BF16)


def _compress(cmp_in, pe_kv, w1_kv, w2_kv):
    b, s, _ = cmp_in.shape
    n_chunks = s // CMP_STRIDE
    return pl.pallas_call(
        _compress_kernel,
        grid=(b, NSA_GROUPS),
        in_specs=[pl.BlockSpec((1, s, LANES), lambda bi, g: (bi, 0, g)),
                  pl.BlockSpec(pe_kv.shape, lambda bi, g: (0, 0)),
                  pl.BlockSpec(w1_kv.shape, lambda bi, g: (0, 0, 0)),
                  pl.BlockSpec(w2_kv.shape, lambda bi, g: (0, 0))],
        out_specs=pl.BlockSpec((1, 1, n_chunks, 2 * LANES), lambda bi, g: (bi, g, 0, 0)),
        out_shape=jax.ShapeDtypeStruct((b, NSA_GROUPS, n_chunks, 2 * LANES), BF16),
        scratch_shapes=[pltpu.VMEM((s + CMP_BLOCK, LANES), F32)],
        compiler_params=pltpu.CompilerParams(
            dimension_semantics=("arbitrary", "arbitrary"), vmem_limit_bytes=VMEM_LIMIT),
        name="nsa_compress",
    )(cmp_in, pe_kv, w1_kv, w2_kv)


def _tile_kind(delta, window=False):
    if delta < 0 or (window and delta > WINDOW // LANES):
        return 4
    if window and delta == WINDOW // LANES:
        return 3
    return min(delta, 2)


def _tile_kinds(row_blocks, steps, window=False):
    n_hf = TK // LANES
    return [[[(slot, _tile_kind(q_tile - (kp * n_hf + hf), window)) for hf in range(n_hf)]
             for slot, q_tile in row_blocks] for kp in steps]


def _score_steps(qx, kt_tiles, kinds, bias_ref, st):
    n_rb = qx.shape[0] // TQ
    n_hf = TK // LANES
    st["mx"] = [None] * n_rb
    st["sc"] = []
    for i, load_kt in enumerate(kt_tiles):
        s = jnp.dot(qx, load_kt(), preferred_element_type=F32)
        step = []
        for rb in range(n_rb):
            tiles = []
            for hf in range(n_hf):
                slot, kind = kinds[i][rb][hf]
                if kind == 4:
                    tiles.append(None)
                    continue
                t = s[rb * TQ:(rb + 1) * TQ, hf * LANES:(hf + 1) * LANES]
                if kind != 2:
                    t = t + bias_ref[slot, kind]
                tiles.append(t)
                st["mx"][rb] = t if st["mx"][rb] is None else jnp.maximum(st["mx"][rb], t)
            step.append(tiles)
        st["sc"].append(step)
        yield


def _weight_steps(v_tiles, st):
    n_rb = len(st["mx"])
    mb = [jnp.broadcast_to(jnp.max(v, axis=-1, keepdims=True), (TQ, LANES)) for v in st["mx"]]
    st["acc"] = jnp.zeros((n_rb * TQ, 2 * LANES), F32)
    for i, load_v in enumerate(v_tiles):
        p = jnp.concatenate(
            [jnp.concatenate([jnp.zeros((TQ, LANES), F32) if t is None else jnp.exp2(t - mb[rb])
                              for t in st["sc"][i][rb]], axis=1) for rb in range(n_rb)], axis=0)
        st["acc"] = st["acc"] + jnp.dot(p.astype(BF16), load_v(), preferred_element_type=F32)
        yield


def _chain(*gens):
    for g in gens:
        yield from g


def _pipeline_tiles(order, n_steps, setup, score_steps, weight_steps, finish):
    n = len(order)
    state = {}
    qk_at = [0, 0]

    def qk_stream():
        state[order[0]] = setup(order[0])
        for i, x in enumerate(order):
            if i + 1 < n:
                state[order[i + 1]] = setup(order[i + 1])
            qk_at[:] = [i, 0]
            for _ in score_steps(x, state[x]):
                qk_at[1] += 1
                yield
        qk_at[:] = [n, 0]

    def pv_stream():
        for i, x in enumerate(order):
            while not (qk_at[0] > i + 1 or qk_at[0] >= n
                       or (qk_at[0] == i + 1 and 2 * qk_at[1] >= n_steps(order[i + 1]))):
                yield
            yield from weight_steps(x, state[x])
            finish(x, state.pop(x))

    live = [qk_stream(), pv_stream()]
    while live:
        for g in list(live):
            try:
                next(g)
            except StopIteration:
                live.remove(g)


def _tile_order(n):
    return list(range(0, n, 2)) + list(range(n - 1 - n % 2, 0, -2))


def _with_ones(v_tile):
    return jnp.concatenate([v_tile, jnp.ones(v_tile.shape, v_tile.dtype)], axis=1)


def _nsa_kernel(q_ref, kst_ref, kwt_ref, vs_ref, vw_ref, kvc_ref, gate_ref, z_ref, dn_ref, bc_ref,
                ovt_ref, et_ref, gsel_ref, o_ref, *, tq):
    seq = q_ref.shape[1]
    n_rb = tq // TQ
    n_sel = seq // SEL_BLOCK

    keys = lambda kp: slice(kp * TK, (kp + 1) * TK)

    n_pairs = NSA_HPG // 2
    tiles = {}

    def tile_setup(x):
        lane = lax.broadcasted_iota(jnp.int32, (tq, LANES), 1)
        lo_half = lane < HEAD_DIM
        blk = lax.broadcasted_iota(jnp.int32, (n_sel, tq), 0)
        tok = lax.broadcasted_iota(jnp.int32, (n_sel, tq), 1)
        row = lax.broadcasted_iota(jnp.int32, (tq, 1), 0)
        kcc = kvc_ref[0, 0, :, 0:LANES]
        vcc = kvc_ref[0, 0, :, LANES:2 * LANES]
        rows = slice(x * tq, (x + 1) * tq)
        q = q_ref[0, rows, :]
        qx = []
        for j in range(NSA_HPG):
            slab = q[:, (j // 2) * LANES:(j // 2 + 1) * LANES]
            keep = lo_half if j % 2 == 0 else jnp.logical_not(lo_half)
            qx.append(jnp.where(keep, slab, jnp.zeros_like(slab)))

        any_valid = (x * tq + row >= CMP_BLOCK - 1).astype(F32)
        p_sum = jnp.zeros((tq, LANES), F32)
        o_c = []
        for j in range(NSA_HPG):
            sj = _nt_dot(qx[j], kcc) + bc_ref[j, rows, :]
            pj = jnp.exp2(sj - jnp.max(sj, axis=-1, keepdims=True))
            pv = jnp.dot(pj.astype(BF16), _with_ones(vcc), preferred_element_type=F32)
            inv = any_valid / pv[:, LANES:]
            p_sum = p_sum + pj * inv
            o_c.append(pv[:, :LANES] * inv)

        imp = _nt_dot(ovt_ref[...], p_sum, precision=lax.Precision.HIGHEST)[0:n_sel, :]
        cur = jnp.right_shift(x * tq + tok, int(math.log2(SEL_BLOCK)))
        forced = (blk == 0) | (blk == cur) | (blk == cur - 1)
        val = jnp.where(forced, BIG, jnp.where(blk > cur, -BIG, imp))
        rank = jnp.zeros((n_sel, tq), F32)
        for other in range(min(n_sel, (x * tq + tq - 1) // SEL_BLOCK + 1)):
            r = val[other:other + 1, :]
            ahead = (r > val) | ((r == val) & (blk > other))
            rank = rank + jnp.where(ahead, 1.0, 0.0)
        block_mask_t = jnp.where(rank < float(min(SEL_TOPK, n_sel)), 0.0, NEG)
        block_mask = jnp.concatenate(
            [block_mask_t, jnp.zeros((LANES - n_sel, tq), F32)], axis=0).T.astype(BF16)

        return dict(qx=qx, block_mask=block_mask, o_c=o_c)

    order = _tile_order(seq // tq)
    started = []

    def setup(unit):
        x, pr = unit
        if not started:
            started.append(True)
            for y in order:
                tiles[y] = tile_setup(y)
        t = tiles[x] if pr + 1 < n_pairs else tiles.pop(x)
        q_pair = jnp.concatenate(t["qx"][2 * pr:2 * pr + 2], axis=0)
        q_sel = jnp.concatenate([q_pair, jnp.concatenate([t["block_mask"]] * 2, axis=0)], axis=1)
        return dict(q_win=q_pair, q_sel=q_sel, o_c=t["o_c"][2 * pr:2 * pr + 2], win={}, sel={})

    def steps_of(unit):
        x = unit[0]
        sel_steps = range((x * tq + tq - 1) // TK + 1)
        win_steps = range(max(0, (x * tq - (WINDOW - 1)) // TK), sel_steps[-1] + 1)
        return win_steps, sel_steps

    both = lambda t: jnp.concatenate([t, t], axis=0)

    def score_steps(unit, st):
        x, pr = unit
        win_steps, sel_steps = steps_of(unit)
        row_blocks = [(j, x * n_rb + r) for j in (2 * pr, 2 * pr + 1) for r in range(n_rb)]
        return _chain(
            _score_steps(st["q_win"],
                         [(lambda kp=kp: both(kwt_ref[0, :, keys(kp)])) for kp in win_steps],
                         _tile_kinds(row_blocks, win_steps, window=True), dn_ref, st["win"]),
            _score_steps(st["q_sel"],
                         [(lambda kp=kp: jnp.concatenate(
                             [both(kst_ref[0, :, keys(kp)]), et_ref[:, keys(kp)]], axis=0))
                          for kp in sel_steps],
                         _tile_kinds(row_blocks, sel_steps), dn_ref, st["sel"]))

    def weight_steps(unit, st):
        win_steps, sel_steps = steps_of(unit)
        return _chain(
            _weight_steps([(lambda kp=kp: _with_ones(vw_ref[0, keys(kp), :])) for kp in win_steps],
                          st["win"]),
            _weight_steps([(lambda kp=kp: _with_ones(vs_ref[0, keys(kp), :])) for kp in sel_steps],
                          st["sel"]))

    def finish(unit, st):
        x, pr = unit
        rows = slice(x * tq, (x + 1) * tq)
        cols = slice(pr * LANES, (pr + 1) * LANES)
        lo_half = lax.broadcasted_iota(jnp.int32, (tq, LANES), 1) < HEAD_DIM
        gates = _sigmoid(jnp.dot(gate_ref[0, rows, :], gsel_ref[0, :, 3 * pr * LANES:3 * (pr + 1) * LANES],
                                 preferred_element_type=F32))
        merged = lambda a: jnp.where(lo_half, a[:tq, :], a[tq:, :])
        ratio = lambda acc: merged(acc[:, :LANES] / acc[:, LANES:])
        o = (gates[:, 0:LANES] * jnp.where(lo_half, st["o_c"][0], st["o_c"][1])
             + gates[:, LANES:2 * LANES] * ratio(st["sel"]["acc"])
             + gates[:, 2 * LANES:3 * LANES] * ratio(st["win"]["acc"]))
        z = z_ref[0, rows, cols].astype(F32)
        o_ref[0, rows, cols] = (o * (z * _sigmoid(z))).astype(o_ref.dtype)

    units = [(x, pr) for x in order for pr in range(n_pairs)]
    _pipeline_tiles(units, lambda u: sum(len(s) for s in steps_of(u)),
                    setup, score_steps, weight_steps, finish)


def _nsa_attention(p3, kt3, kvc, dn, bc, ovt, et, gsel, col, tq=2 * TQ):
    b, s, _ = p3.shape
    wide = 2 * LANES

    def slab(first_lane_block, per_group, width=LANES):
        return pl.BlockSpec((1, s, width), lambda g, bi: (bi, 0, first_lane_block + per_group * g))

    in_specs = [
        slab(col["q"] // wide, 1, wide),
        pl.BlockSpec((1, HEAD_DIM, s), lambda g, bi: (bi, g, 0)),
        pl.BlockSpec((1, HEAD_DIM, s), lambda g, bi: (bi, NSA_GROUPS + g, 0)),
        slab(col["vsel"] // LANES, 1), slab(col["vwin"] // LANES, 1),
        pl.BlockSpec((1, 1, kvc.shape[2], wide), lambda g, bi: (bi, g, 0, 0)),
        slab(col["gate"] // LANES, 0),
        slab(col["z"] // wide, 1, wide),
        _resident((NSA_HPG, N_BIAS_KINDS, TQ, LANES), lambda g, bi: (g, 0, 0, 0)),
        _resident((NSA_HPG, s, LANES), lambda g, bi: (g, 0, 0)),
        _resident(ovt.shape, lambda g, bi: (0, 0)),
        _resident(et.shape, lambda g, bi: (0, 0)),
        _resident((1,) + gsel.shape[1:], lambda g, bi: (g, 0, 0)),
    ]
    return pl.pallas_call(
        functools.partial(_nsa_kernel, tq=tq),
        grid=(NSA_GROUPS, b),
        in_specs=in_specs,
        out_specs=pl.BlockSpec((1, s, wide), lambda g, bi: (bi, 0, g)),
        out_shape=jax.ShapeDtypeStruct((b, s, NSA_GROUPS * wide), BF16),
        compiler_params=pltpu.CompilerParams(
            dimension_semantics=("arbitrary", "arbitrary"), vmem_limit_bytes=VMEM_LIMIT),
        name="nsa_attention",
    )(p3, kt3, kt3, p3, p3, kvc, p3, p3, dn, bc, ovt, et, gsel)


def _diff_kernel(q_ref, kt_ref, v_ref, z_ref, dn_ref, lq1_ref, lk1_ref, lq2_ref, lk2_ref, sub_ref,
                 o_ref, *, lambda_init, tq):
    seq = q_ref.shape[1]
    n_rb = tq // TQ
    lane = lax.broadcasted_iota(jnp.int32, (tq, LANES), 1)
    lam = (jnp.exp(jnp.sum(lq1_ref[...] * lk1_ref[...], axis=-1, keepdims=True))
           - jnp.exp(jnp.sum(lq2_ref[...] * lk2_ref[...], axis=-1, keepdims=True)) + lambda_init)
    keys = lambda kp: slice(kp * TK, (kp + 1) * TK)
    steps_of = lambda x: range((x * tq + tq - 1) // TK + 1)

    def setup(x):
        q = q_ref[0, x * tq:(x + 1) * tq, :]
        zero = jnp.zeros_like(q)
        return dict(q=jnp.concatenate([jnp.where(lane < HEAD_DIM, q, zero),
                                       jnp.where(lane >= HEAD_DIM, q, zero)], axis=0))

    def score_steps(x, st):
        row_blocks = [(m, x * n_rb + r) for m in range(2) for r in range(n_rb)]
        return _score_steps(st["q"], [(lambda kp=kp: kt_ref[0, :, keys(kp)]) for kp in steps_of(x)],
                            _tile_kinds(row_blocks, steps_of(x)), dn_ref, st)

    def weight_steps(x, st):
        return _weight_steps([(lambda kp=kp: _with_ones(v_ref[0, keys(kp), :]))
                              for kp in steps_of(x)], st)

    def finish(x, st):
        rows = slice(x * tq, (x + 1) * tq)
        a = st["acc"][:, :LANES] / st["acc"][:, LANES:]
        o = a[:tq, :] - lam * a[tq:, :]
        o = o * lax.rsqrt(jnp.mean(o * o, axis=-1, keepdims=True) + EPS) * sub_ref[...]
        o = o * (1.0 - lambda_init)
        z = z_ref[0, rows, :].astype(F32)
        o_ref[0, rows, :] = (o * (z * _sigmoid(z))).astype(o_ref.dtype)

    _pipeline_tiles(_tile_order(seq // tq), lambda x: len(steps_of(x)),
                    setup, score_steps, weight_steps, finish)


def _diff_attention(p3, kt3, dn, lq1, lk1, lq2, lk2, subln, lambda_init, tq=2 * TQ):
    b, s, _ = p3.shape
    h = DIFF_HEADS
    vec = lambda a: a.reshape(1, -1)
    small = pl.BlockSpec((1, HEAD_DIM), lambda bi, hi: (0, 0))
    slab = lambda first: pl.BlockSpec((1, s, LANES), lambda bi, hi: (bi, 0, first + hi))
    in_specs = [
        slab(0),
        pl.BlockSpec((1, LANES, s), lambda bi, hi: (bi, hi, 0)),
        slab(h), slab(2 * h),
        pl.BlockSpec((2, N_BIAS_KINDS, TQ, LANES), lambda bi, hi: (hi, 0, 0, 0)),
        small, small, small, small,
        pl.BlockSpec((1, DIFF_VDIM), lambda bi, hi: (0, hi)),
    ]
    return pl.pallas_call(
        functools.partial(_diff_kernel, lambda_init=lambda_init, tq=tq),
        grid=(b, h),
        in_specs=in_specs,
        out_specs=slab(0),
        out_shape=jax.ShapeDtypeStruct((b, s, h * DIFF_VDIM), BF16),
        compiler_params=pltpu.CompilerParams(
            dimension_semantics=("arbitrary", "arbitrary"), vmem_limit_bytes=VMEM_LIMIT),
        name="diff_attention",
    )(p3, kt3, p3, p3, dn, vec(lq1), vec(lk1), vec(lq2), vec(lk2), vec(subln))


def _nsa_weight_layout(w_in):
    d = HEAD_DIM
    kv = NSA_GROUPS * d
    width = NSA_HEADS * d
    pts = np.cumsum([width] + [kv] * 6 + [3 * NSA_HEADS, width])
    q, kc, vc, ks, vs, kw, vw, gate, z = [w_in[:, a:b] for a, b in zip([0] + list(pts[:-1]), pts)]
    grp = lambda w, g: w[:, g * d:(g + 1) * d]
    dup = lambda w: jnp.concatenate(
        [jnp.concatenate([grp(w, g), grp(w, g)], axis=1) for g in range(NSA_GROUPS)], axis=1)
    gate_pad = jnp.pad(gate, ((0, 0), (0, LANES - gate.shape[1])))
    parts = [("q", q * (d ** -0.5 * LOG2E)), ("z", z), ("vsel", dup(vs)), ("vwin", dup(vw)),
             ("gate", gate_pad)]
    col, off = {}, 0
    for name, w in parts:
        col[name] = off
        off += w.shape[1]
    w_main = jnp.concatenate([w for _, w in parts], axis=1).astype(BF16)
    w_keys_t = jnp.concatenate([ks, kw], axis=1).T.astype(BF16)
    w_cmp = jnp.concatenate(
        [jnp.concatenate([grp(kc, g), grp(vc, g)], axis=1) for g in range(NSA_GROUPS)],
        axis=1).astype(BF16)
    return w_main, w_cmp, w_keys_t, col


def _compress_weight_layout(pe_k, w1_k, w2_k, pe_v, w1_v, w2_v):
    d, hdn = HEAD_DIM, CMP_HIDDEN
    pe_kv = jnp.concatenate([pe_k, pe_v], axis=1)
    w1k = w1_k.reshape(CMP_BLOCK, d, hdn)
    w1v = w1_v.reshape(CMP_BLOCK, d, hdn)
    zero = jnp.zeros_like(w1k)
    w1_kv = jnp.concatenate([jnp.concatenate([w1k, zero], axis=2),
                             jnp.concatenate([zero, w1v], axis=2)], axis=1).astype(BF16)
    z2 = jnp.zeros((hdn, 2 * d), w2_k.dtype)
    w2_kv = jnp.concatenate([jnp.concatenate([w2_k, w2_k, z2], axis=1),
                             jnp.concatenate([z2, w2_v, w2_v], axis=1)], axis=0).astype(BF16)
    return pe_kv, w1_kv, w2_kv


def _selection_constants(seq):
    n_cmp_pad = seq // CMP_STRIDE
    n_cmp = (seq - CMP_BLOCK) // CMP_STRIDE + 1
    n_sel = seq // SEL_BLOCK
    cmp_lo = np.arange(n_cmp_pad) * CMP_STRIDE
    sel_lo = np.arange(n_sel) * SEL_BLOCK
    overlap = np.maximum(np.minimum(cmp_lo[:, None] + CMP_BLOCK, sel_lo[None, :] + SEL_BLOCK)
                         - np.maximum(cmp_lo[:, None], sel_lo[None, :]), 0).astype(np.float32) / CMP_BLOCK
    overlap[n_cmp:] = 0.0
    ovt = np.zeros((LANES, n_cmp_pad), np.float32)
    ovt[:n_sel] = overlap.T
    et = np.zeros((LANES, seq), np.float32)
    et[np.arange(seq) // SEL_BLOCK, np.arange(seq)] = 1.0
    n_pairs = NSA_HPG // 2
    gsel = np.zeros((NSA_GROUPS, LANES, 3 * n_pairs * LANES), np.float32)
    for g in range(NSA_GROUPS):
        for pr in range(n_pairs):
            for i in range(3):
                for n in range(LANES):
                    head = g * NSA_HPG + 2 * pr + n // HEAD_DIM
                    gsel[g, 3 * head + i, (pr * 3 + i) * LANES + n] = 1.0
    return jnp.asarray(ovt), jnp.asarray(et, dtype=BF16), jnp.asarray(gsel, dtype=BF16)


def kernel(x, rel_bias_table, norm_pre, norm_post, nsa_w_in, nsa_cmp_pe_k, nsa_cmp_w1_k, nsa_cmp_w2_k,
           nsa_cmp_pe_v, nsa_cmp_w1_v, nsa_cmp_w2_v, nsa_w_out, diff_w_in, diff_lambda_q1,
           diff_lambda_k1, diff_lambda_q2, diff_lambda_k2, diff_subln, diff_w_out):
    b, s, d = x.shape
    n = b * s
    assert d == D_MODEL and s % TK == 0 and s // SEL_BLOCK <= LANES and s // CMP_STRIDE == LANES
    dn, bc = _bias_tiles(rel_bias_table, s)
    ovt, et, gsel = _selection_constants(s)
    x2 = x.reshape(n, d)
    for i in range(DEPTH):
        j = i // 2
        if i % 2 == 0:
            w_main, w_cmp, w_keys_t, col = _nsa_weight_layout(nsa_w_in[j])
            p_main, p_cmp, keys_t = _norm_proj(x2, norm_pre[i], s, [w_main, w_cmp], [BF16, F32],
                                               [w_keys_t])
            kvc = _compress(p_cmp.reshape(b, s, -1),
                            *_compress_weight_layout(nsa_cmp_pe_k[j], nsa_cmp_w1_k[j], nsa_cmp_w2_k[j],
                                                     nsa_cmp_pe_v[j], nsa_cmp_w1_v[j], nsa_cmp_w2_v[j]))
            o = _nsa_attention(p_main.reshape(b, s, -1), keys_t, kvc, dn, bc, ovt, et, gsel, col)
            w_out = nsa_w_out[j]
        else:
            lambda_init = 0.8 - 0.6 * math.exp(-0.3 * i)
            w = diff_w_in[j]
            w_main = jnp.concatenate([w[:, :d] * (HEAD_DIM ** -0.5 * LOG2E), w[:, 2 * d:]],
                                     axis=1).astype(BF16)
            p_main, keys_t = _norm_proj(x2, norm_pre[i], s, [w_main], [BF16],
                                        [w[:, d:2 * d].T.astype(BF16)])
            o = _diff_attention(p_main.reshape(b, s, -1), keys_t, dn, diff_lambda_q1[j],
                                diff_lambda_k1[j], diff_lambda_q2[j], diff_lambda_k2[j],
                                diff_subln[j], lambda_init)
            w_out = diff_w_out[j]
        x2 = _out_proj(o.reshape(n, -1), w_out.astype(BF16), x2, norm_post[i])
    return x2.reshape(b, s, d)
```

```python
import functools
import math

import jax
import jax.numpy as jnp
import numpy as np
from jax import lax
from jax.experimental import pallas as pl
from jax.experimental.pallas import tpu as pltpu

F32 = jnp.float32
BF16 = jnp.bfloat16

D_MODEL = 1024
DEPTH = 2
REL_BUCKETS = 32
REL_MAX_DIST = 128
NSA_HEADS = 16
NSA_GROUPS = 4
NSA_HPG = NSA_HEADS // NSA_GROUPS
HEAD_DIM = 64
CMP_BLOCK = 32
CMP_STRIDE = 16
CMP_HIDDEN = 128
SEL_BLOCK = 64
SEL_TOPK = 8
WINDOW = 512
DIFF_HEADS = 8
DIFF_VDIM = 128
NEG = -1e30
BIG = 1e9
EPS = 1e-6

LANES = 128
TQ = 128
TK = 2 * LANES
N_BIAS_KINDS = 5
VMEM_LIMIT = 52 * 1024 * 1024
LOG2E = math.log2(math.e)


def _sigmoid(x):
    return 1.0 / (1.0 + jnp.exp(-x))


def _nt_dot(a, b, precision=None):
    return lax.dot_general(a, b, (((1,), (1,)), ((), ())), precision=precision,
                           preferred_element_type=F32)


def _resident(shape, index_map):
    return pl.BlockSpec(shape, index_map, pipeline_mode=pl.Buffered(1))


def _residual_update(o_ref, w_ref, x_ref, g_ref):
    y = jnp.dot(o_ref[...], w_ref[...], preferred_element_type=F32)
    return x_ref[...] + y * lax.rsqrt(jnp.mean(y * y, axis=-1, keepdims=True) + EPS) * g_ref[...]


def _norm_proj_kernel(*refs, n_w, n_wt, fused):
    if fused:
        x = _residual_update(refs[0], refs[1], refs[2], refs[3])
        refs[-1][...] = x
        refs = refs[4:-1]
    else:
        x = refs[0][...]
        refs = refs[1:]
    g_ref, refs = refs[0], refs[1:]
    w_refs, wt_refs, o_refs = refs[:n_w], refs[n_w:n_w + n_wt], refs[n_w + n_wt:]
    u = x * lax.rsqrt(jnp.mean(x * x, axis=-1, keepdims=True) + EPS) * g_ref[...]
    ub = u.astype(BF16)
    for w_ref, o_ref in zip(w_refs, o_refs[:n_w]):
        o_ref[...] = jnp.dot(ub, w_ref[...], preferred_element_type=F32).astype(o_ref.dtype)
    for wt_ref, o_ref in zip(wt_refs, o_refs[n_w:]):
        o_ref[0] = _nt_dot(wt_ref[...], ub).astype(o_ref.dtype)


def _norm_proj(x2, g, seq, weights, out_dtypes, weights_t=(), pending=None, tm=512):
    n, d = x2.shape
    n_st = seq // tm
    row_tile = lambda c: pl.BlockSpec((tm, c), lambda i: (i, 0))
    vector = pl.BlockSpec((1, d), lambda i: (0, 0))
    args, in_specs = [x2, g.reshape(1, d)], [row_tile(d), vector]
    if pending is not None:
        o2, w_out, g_post = pending
        args = [o2, w_out, x2, g_post.reshape(1, d), g.reshape(1, d)]
        in_specs = [row_tile(o2.shape[1]), _resident(w_out.shape, lambda i: (0, 0)), row_tile(d),
                    vector, vector]
    out_specs, out_shape = [], []
    for w, dt in zip(weights, out_dtypes):
        c = w.shape[1]
        in_specs.append(_resident((d, c), lambda i: (0, 0)))
        out_specs.append(pl.BlockSpec((tm, c), lambda i: (i, 0)))
        out_shape.append(jax.ShapeDtypeStruct((n, c), dt))
    for wt in weights_t:
        c = wt.shape[0]
        in_specs.append(_resident((c, d), lambda i: (0, 0)))
        out_specs.append(pl.BlockSpec((1, c, tm), lambda i: (i // n_st, 0, i % n_st)))
        out_shape.append(jax.ShapeDtypeStruct((n // seq, c, seq), BF16))
    if pending is not None:
        out_specs.append(row_tile(d))
        out_shape.append(jax.ShapeDtypeStruct((n, d), F32))
    return pl.pallas_call(
        functools.partial(_norm_proj_kernel, n_w=len(weights), n_wt=len(weights_t),
                          fused=pending is not None),
        grid=(n // tm,),
        in_specs=in_specs,
        out_specs=out_specs,
        out_shape=out_shape,
        compiler_params=pltpu.CompilerParams(
            dimension_semantics=("arbitrary",), vmem_limit_bytes=VMEM_LIMIT),
        name="norm_proj",
    )(*args, *weights, *weights_t)


def _out_proj_kernel(o_ref, w_ref, x_ref, g_ref, y_ref):
    y_ref[...] = _residual_update(o_ref, w_ref, x_ref, g_ref)


def _out_proj(o2, w, x2, g, tm=512):
    n, d = x2.shape
    return pl.pallas_call(
        _out_proj_kernel,
        grid=(n // tm,),
        in_specs=[pl.BlockSpec((tm, o2.shape[1]), lambda i: (i, 0)),
                  _resident(w.shape, lambda i: (0, 0)),
                  pl.BlockSpec((tm, d), lambda i: (i, 0)),
                  pl.BlockSpec((1, d), lambda i: (0, 0))],
        out_specs=pl.BlockSpec((tm, d), lambda i: (i, 0)),
        out_shape=jax.ShapeDtypeStruct((n, d), F32),
        compiler_params=pltpu.CompilerParams(
            dimension_semantics=("arbitrary",), vmem_limit_bytes=VMEM_LIMIT),
        name="out_proj",
    )(o2, w, x2, g.reshape(1, d))


def _bucket_starts(max_dist):
    max_exact = REL_BUCKETS // 2
    n = np.arange(max_dist)
    nf = np.maximum(n, 1).astype(np.float32)
    large = max_exact + (np.log(nf / max_exact) / np.float32(math.log(REL_MAX_DIST / max_exact))
                         * (REL_BUCKETS - max_exact)).astype(np.int32)
    bucket = np.where(n < max_exact, n, np.minimum(large, REL_BUCKETS - 1))
    assert (np.diff(bucket) >= 0).all()
    return [int(np.argmax(bucket >= b)) if (bucket >= b).any() else None
            for b in range(REL_BUCKETS)]


def _bias_kernel(tbl_ref, dn_ref, bc_ref):
    h = pl.program_id(0)
    c31 = tbl_ref[REL_BUCKETS - 1, h]
    starts = _bucket_starts(bc_ref.shape[1])

    def rel_bias(dist):
        n = jnp.maximum(dist, 0)
        out = jnp.full(dist.shape, tbl_ref[0, h], F32)
        for b in range(1, REL_BUCKETS):
            if starts[b] is not None:
                out = jnp.where(n >= starts[b], tbl_ref[b, h], out)
        return (out - c31) * LOG2E

    i = lax.broadcasted_iota(jnp.int32, (TQ, LANES), 0)
    j = lax.broadcasted_iota(jnp.int32, (TQ, LANES), 1)
    dn_ref[0, 0] = jnp.where(j <= i, rel_bias(i - j), NEG)
    dn_ref[0, 1] = rel_bias(LANES + i - j)
    dn_ref[0, 2] = jnp.zeros((TQ, LANES), F32)
    dn_ref[0, 3] = jnp.where(j > i, 0.0, NEG)
    dn_ref[0, 4] = jnp.full((TQ, LANES), NEG, F32)

    n_cmp = (bc_ref.shape[1] - CMP_BLOCK) // CMP_STRIDE + 1

    def body(r, carry):
        t = r * TQ + i
        d = t - (j * CMP_STRIDE + CMP_BLOCK - 1)
        valid = (d >= 0) & (j < n_cmp)
        bc_ref[0, pl.ds(pl.multiple_of(r * TQ, TQ), TQ), :] = jnp.where(valid, rel_bias(d), NEG)
        return carry

    lax.fori_loop(0, bc_ref.shape[1] // TQ, body, 0)


def _bias_tiles(table, seq):
    n_maps = table.shape[1]
    return pl.pallas_call(
        _bias_kernel,
        grid=(n_maps,),
        in_specs=[pl.BlockSpec(memory_space=pltpu.SMEM)],
        out_specs=[pl.BlockSpec((1, N_BIAS_KINDS, TQ, LANES), lambda h: (h, 0, 0, 0)),
                   pl.BlockSpec((1, seq, LANES), lambda h: (h, 0, 0))],
        out_shape=[jax.ShapeDtypeStruct((n_maps, N_BIAS_KINDS, TQ, LANES), F32),
                   jax.ShapeDtypeStruct((n_maps, seq, LANES), F32)],
        compiler_params=pltpu.CompilerParams(dimension_semantics=("arbitrary",)),
        name="bias_tiles",
    )(table)


def _compress_kernel(x_ref, pe_ref, w1_ref, w2_ref, o_ref, xp_ref):
    seq = x_ref.shape[1]
    n_chunks = seq // CMP_STRIDE
    xp_ref[0:seq, :] = x_ref[0]
    xp_ref[seq:, :] = jnp.zeros((xp_ref.shape[0] - seq, xp_ref.shape[1]), F32)
    hid = jnp.zeros((n_chunks, 2 * CMP_HIDDEN), F32)
    for l in range(CMP_BLOCK):
        a = xp_ref[pl.ds(l, n_chunks, stride=CMP_STRIDE), :]
        hid = hid + jnp.dot((a + pe_ref[l:l + 1, :]).astype(BF16), w1_ref[l],
                            preferred_element_type=F32)
    hid = hid * _sigmoid(hid)
    o_ref[0, 0] = jnp.dot(hid.astype(BF16), w2_ref[...], preferred_element_type=F32).astype(BF16)


def _compress(cmp_in, pe_kv, w1_kv, w2_kv):
    b, s, _ = cmp_in.shape
    n_chunks = s // CMP_STRIDE
    return pl.pallas_call(
        _compress_kernel,
        grid=(b, NSA_GROUPS),
        in_specs=[pl.BlockSpec((1, s, LANES), lambda bi, g: (bi, 0, g)),
                  pl.BlockSpec(pe_kv.shape, lambda bi, g: (0, 0)),
                  pl.BlockSpec(w1_kv.shape, lambda bi, g: (0, 0, 0)),
                  pl.BlockSpec(w2_kv.shape, lambda bi, g: (0, 0))],
        out_specs=pl.BlockSpec((1, 1, n_chunks, 2 * LANES), lambda bi, g: (bi, g, 0, 0)),
        out_shape=jax.ShapeDtypeStruct((b, NSA_GROUPS, n_chunks, 2 * LANES), BF16),
        scratch_shapes=[pltpu.VMEM((s + CMP_BLOCK, LANES), F32)],
        compiler_params=pltpu.CompilerParams(
            dimension_semantics=("arbitrary", "arbitrary"), vmem_limit_bytes=VMEM_LIMIT),
        name="nsa_compress",
    )(cmp_in, pe_kv, w1_kv, w2_kv)


def _tile_kind(delta, window=False):
    if delta < 0 or (window and delta > WINDOW // LANES):
        return 4
    if window and delta == WINDOW // LANES:
        return 3
    return min(delta, 2)


def _tile_kinds(row_blocks, steps, window=False):
    n_hf = TK // LANES
    return [[[(slot, _tile_kind(q_tile - (kp * n_hf + hf), window)) for hf in range(n_hf)]
             for slot, q_tile in row_blocks] for kp in steps]


def _score_steps(qx, kt_tiles, kinds, bias_ref, st):
    n_rb = qx.shape[0] // TQ
    n_hf = TK // LANES
    st["mx"] = [None] * n_rb
    st["sc"] = []
    for i, load_kt in enumerate(kt_tiles):
        s = jnp.dot(qx, load_kt(), preferred_element_type=F32)
        step = []
        for rb in range(n_rb):
            tiles = []
            for hf in range(n_hf):
                slot, kind = kinds[i][rb][hf]
                if kind == 4:
                    tiles.append(None)
                    continue
                t = s[rb * TQ:(rb + 1) * TQ, hf * LANES:(hf + 1) * LANES]
                if kind != 2:
                    t = t + bias_ref[slot, kind]
                tiles.append(t)
                st["mx"][rb] = t if st["mx"][rb] is None else jnp.maximum(st["mx"][rb], t)
            step.append(tiles)
        st["sc"].append(step)
        yield


def _weight_steps(v_tiles, st):
    n_rb = len(st["mx"])
    mb = [jnp.broadcast_to(jnp.max(v, axis=-1, keepdims=True), (TQ, LANES)) for v in st["mx"]]
    st["acc"] = jnp.zeros((n_rb * TQ, 2 * LANES), F32)
    for i, load_v in enumerate(v_tiles):
        p = jnp.concatenate(
            [jnp.concatenate([jnp.zeros((TQ, LANES), F32) if t is None else jnp.exp2(t - mb[rb])
                              for t in st["sc"][i][rb]], axis=1) for rb in range(n_rb)], axis=0)
        st["acc"] = st["acc"] + jnp.dot(p.astype(BF16), load_v(), preferred_element_type=F32)
        yield


def _chain(*gens):
    for g in gens:
        yield from g


def _pipeline_tiles(order, n_steps, setup, score_steps, weight_steps, finish):
    n = len(order)
    state = {}
    qk_at = [0, 0]

    def qk_stream():
        state[order[0]] = setup(order[0])
        for i, x in enumerate(order):
            if i + 1 < n:
                state[order[i + 1]] = setup(order[i + 1])
            qk_at[:] = [i, 0]
            for _ in score_steps(x, state[x]):
                qk_at[1] += 1
                yield
        qk_at[:] = [n, 0]

    def pv_stream():
        for i, x in enumerate(order):
            while not (qk_at[0] > i + 1 or qk_at[0] >= n
                       or (qk_at[0] == i + 1 and 2 * qk_at[1] >= n_steps(order[i + 1]))):
                yield
            yield from weight_steps(x, state[x])
            finish(x, state.pop(x))

    live = [qk_stream(), pv_stream()]
    while live:
        for g in list(live):
            try:
                next(g)
            except StopIteration:
                live.remove(g)


def _tile_order(n):
    return list(range(0, n, 2)) + list(range(n - 1 - n % 2, 0, -2))


def _with_ones(v_tile):
    return jnp.concatenate([v_tile, jnp.ones(v_tile.shape, v_tile.dtype)], axis=1)


def _nsa_kernel(q_ref, kst_ref, kwt_ref, vs_ref, vw_ref, kvc_ref, gate_ref, z_ref, dn_ref, bc_ref,
                ovt_ref, et_ref, gsel_ref, o_ref, *, tq):
    seq = q_ref.shape[1]
    n_rb = tq // TQ
    n_sel = seq // SEL_BLOCK

    keys = lambda kp: slice(kp * TK, (kp + 1) * TK)

    n_pairs = NSA_HPG // 2
    tiles = {}

    def tile_setup(x):
        lane = lax.broadcasted_iota(jnp.int32, (tq, LANES), 1)
        lo_half = lane < HEAD_DIM
        blk = lax.broadcasted_iota(jnp.int32, (n_sel, tq), 0)
        tok = lax.broadcasted_iota(jnp.int32, (n_sel, tq), 1)
        row = lax.broadcasted_iota(jnp.int32, (tq, 1), 0)
        kcc = kvc_ref[0, 0, :, 0:LANES]
        vcc = kvc_ref[0, 0, :, LANES:2 * LANES]
        rows = slice(x * tq, (x + 1) * tq)
        q = q_ref[0, rows, :]
        qx = []
        for j in range(NSA_HPG):
            slab = q[:, (j // 2) * LANES:(j // 2 + 1) * LANES]
            keep = lo_half if j % 2 == 0 else jnp.logical_not(lo_half)
            qx.append(jnp.where(keep, slab, jnp.zeros_like(slab)))

        any_valid = (x * tq + row >= CMP_BLOCK - 1).astype(F32)
        p_sum = jnp.zeros((tq, LANES), F32)
        o_c = []
        for j in range(NSA_HPG):
            sj = _nt_dot(qx[j], kcc) + bc_ref[j, rows, :]
            pj = jnp.exp2(sj - jnp.max(sj, axis=-1, keepdims=True))
            pv = jnp.dot(pj.astype(BF16), _with_ones(vcc), preferred_element_type=F32)
            inv = any_valid / pv[:, LANES:]
            p_sum = p_sum + pj * inv
            o_c.append(pv[:, :LANES] * inv)

        imp = _nt_dot(ovt_ref[...], p_sum, precision=lax.Precision.HIGHEST)[0:n_sel, :]
        cur = jnp.right_shift(x * tq + tok, int(math.log2(SEL_BLOCK)))
        forced = (blk == 0) | (blk == cur) | (blk == cur - 1)
        val = jnp.where(forced, BIG, jnp.where(blk > cur, -BIG, imp))
        rank = jnp.zeros((n_sel, tq), F32)
        for other in range(min(n_sel, (x * tq + tq - 1) // SEL_BLOCK + 1)):
            r = val[other:other + 1, :]
            ahead = (r > val) | ((r == val) & (blk > other))
            rank = rank + jnp.where(ahead, 1.0, 0.0)
        block_mask_t = jnp.where(rank < float(min(SEL_TOPK, n_sel)), 0.0, NEG)
        block_mask = jnp.concatenate(
            [block_mask_t, jnp.zeros((LANES - n_sel, tq), F32)], axis=0).T.astype(BF16)

        return dict(qx=qx, block_mask=block_mask, o_c=o_c)

    order = _tile_order(seq // tq)
    started = []

    def setup(unit):
        x, pr = unit
        if not started:
            started.append(True)
            for y in order:
                tiles[y] = tile_setup(y)
        t = tiles[x] if pr + 1 < n_pairs else tiles.pop(x)
        q_pair = jnp.concatenate(t["qx"][2 * pr:2 * pr + 2], axis=0)
        q_sel = jnp.concatenate([q_pair, jnp.concatenate([t["block_mask"]] * 2, axis=0)], axis=1)
        return dict(q_win=q_pair, q_sel=q_sel, o_c=t["o_c"][2 * pr:2 * pr + 2], win={}, sel={})

    def steps_of(unit):
        x = unit[0]
        sel_steps = range((x * tq + tq - 1) // TK + 1)
        win_steps = range(max(0, (x * tq - (WINDOW - 1)) // TK), sel_steps[-1] + 1)
        return win_steps, sel_steps

    both = lambda t: jnp.concatenate([t, t], axis=0)

    def score_steps(unit, st):
        x, pr = unit
        win_steps, sel_steps = steps_of(unit)
        row_blocks = [(j, x * n_rb + r) for j in (2 * pr, 2 * pr + 1) for r in range(n_rb)]
        return _chain(
            _score_steps(st["q_win"],
                         [(lambda kp=kp: both(kwt_ref[0, :, keys(kp)])) for kp in win_steps],
                         _tile_kinds(row_blocks, win_steps, window=True), dn_ref, st["win"]),
            _score_steps(st["q_sel"],
                         [(lambda kp=kp: jnp.concatenate(
                             [both(kst_ref[0, :, keys(kp)]), et_ref[:, keys(kp)]], axis=0))
                          for kp in sel_steps],
                         _tile_kinds(row_blocks, sel_steps), dn_ref, st["sel"]))

    def weight_steps(unit, st):
        win_steps, sel_steps = steps_of(unit)
        return _chain(
            _weight_steps([(lambda kp=kp: _with_ones(vw_ref[0, keys(kp), :])) for kp in win_steps],
                          st["win"]),
            _weight_steps([(lambda kp=kp: _with_ones(vs_ref[0, keys(kp), :])) for kp in sel_steps],
                          st["sel"]))

    def finish(unit, st):
        x, pr = unit
        rows = slice(x * tq, (x + 1) * tq)
        cols = slice(pr * LANES, (pr + 1) * LANES)
        lo_half = lax.broadcasted_iota(jnp.int32, (tq, LANES), 1) < HEAD_DIM
        gates = _sigmoid(jnp.dot(gate_ref[0, rows, :], gsel_ref[0, :, 3 * pr * LANES:3 * (pr + 1) * LANES],
                                 preferred_element_type=F32))
        merged = lambda a: jnp.where(lo_half, a[:tq, :], a[tq:, :])
        ratio = lambda acc: merged(acc[:, :LANES] / acc[:, LANES:])
        o = (gates[:, 0:LANES] * jnp.where(lo_half, st["o_c"][0], st["o_c"][1])
             + gates[:, LANES:2 * LANES] * ratio(st["sel"]["acc"])
             + gates[:, 2 * LANES:3 * LANES] * ratio(st["win"]["acc"]))
        z = z_ref[0, rows, cols].astype(F32)
        o_ref[0, rows, cols] = (o * (z * _sigmoid(z))).astype(o_ref.dtype)

    units = [(x, pr) for x in order for pr in range(n_pairs)]
    _pipeline_tiles(units, lambda u: sum(len(s) for s in steps_of(u)),
                    setup, score_steps, weight_steps, finish)


def _nsa_attention(p3, kt3, kvc, dn, bc, ovt, et, gsel, col, tq=2 * TQ):
    b, s, _ = p3.shape
    wide = 2 * LANES

    def slab(first_lane_block, per_group, width=LANES):
        return pl.BlockSpec((1, s, width), lambda g, bi: (bi, 0, first_lane_block + per_group * g))

    in_specs = [
        slab(col["q"] // wide, 1, wide),
        pl.BlockSpec((1, HEAD_DIM, s), lambda g, bi: (bi, g, 0)),
        pl.BlockSpec((1, HEAD_DIM, s), lambda g, bi: (bi, NSA_GROUPS + g, 0)),
        slab(col["vsel"] // LANES, 1), slab(col["vwin"] // LANES, 1),
        pl.BlockSpec((1, 1, kvc.shape[2], wide), lambda g, bi: (bi, g, 0, 0)),
        slab(col["gate"] // LANES, 0),
        slab(col["z"] // wide, 1, wide),
        _resident((NSA_HPG, N_BIAS_KINDS, TQ, LANES), lambda g, bi: (g, 0, 0, 0)),
        _resident((NSA_HPG, s, LANES), lambda g, bi: (g, 0, 0)),
        _resident(ovt.shape, lambda g, bi: (0, 0)),
        _resident(et.shape, lambda g, bi: (0, 0)),
        _resident((1,) + gsel.shape[1:], lambda g, bi: (g, 0, 0)),
    ]
    return pl.pallas_call(
        functools.partial(_nsa_kernel, tq=tq),
        grid=(NSA_GROUPS, b),
        in_specs=in_specs,
        out_specs=pl.BlockSpec((1, s, wide), lambda g, bi: (bi, 0, g)),
        out_shape=jax.ShapeDtypeStruct((b, s, NSA_GROUPS * wide), BF16),
        compiler_params=pltpu.CompilerParams(
            dimension_semantics=("arbitrary", "arbitrary"), vmem_limit_bytes=VMEM_LIMIT),
        name="nsa_attention",
    )(p3, kt3, kt3, p3, p3, kvc, p3, p3, dn, bc, ovt, et, gsel)


def _diff_kernel(q_ref, kt_ref, v_ref, z_ref, dn_ref, lq1_ref, lk1_ref, lq2_ref, lk2_ref, sub_ref,
                 o_ref, *, lambda_init, tq):
    seq = q_ref.shape[1]
    n_rb = tq // TQ
    lane = lax.broadcasted_iota(jnp.int32, (tq, LANES), 1)
    lam = (jnp.exp(jnp.sum(lq1_ref[...] * lk1_ref[...], axis=-1, keepdims=True))
           - jnp.exp(jnp.sum(lq2_ref[...] * lk2_ref[...], axis=-1, keepdims=True)) + lambda_init)
    keys = lambda kp: slice(kp * TK, (kp + 1) * TK)
    steps_of = lambda x: range((x * tq + tq - 1) // TK + 1)

    def setup(x):
        q = q_ref[0, x * tq:(x + 1) * tq, :]
        zero = jnp.zeros_like(q)
        return dict(q=jnp.concatenate([jnp.where(lane < HEAD_DIM, q, zero),
                                       jnp.where(lane >= HEAD_DIM, q, zero)], axis=0))

    def score_steps(x, st):
        row_blocks = [(m, x * n_rb + r) for m in range(2) for r in range(n_rb)]
        return _score_steps(st["q"], [(lambda kp=kp: kt_ref[0, :, keys(kp)]) for kp in steps_of(x)],
                            _tile_kinds(row_blocks, steps_of(x)), dn_ref, st)

    def weight_steps(x, st):
        return _weight_steps([(lambda kp=kp: _with_ones(v_ref[0, keys(kp), :]))
                              for kp in steps_of(x)], st)

    def finish(x, st):
        rows = slice(x * tq, (x + 1) * tq)
        a = st["acc"][:, :LANES] / st["acc"][:, LANES:]
        o = a[:tq, :] - lam * a[tq:, :]
        o = o * lax.rsqrt(jnp.mean(o * o, axis=-1, keepdims=True) + EPS) * sub_ref[...]
        o = o * (1.0 - lambda_init)
        z = z_ref[0, rows, :].astype(F32)
        o_ref[0, rows, :] = (o * (z * _sigmoid(z))).astype(o_ref.dtype)

    _pipeline_tiles(_tile_order(seq // tq), lambda x: len(steps_of(x)),
                    setup, score_steps, weight_steps, finish)


def _diff_attention(p3, kt3, dn, lq1, lk1, lq2, lk2, subln, lambda_init, tq=2 * TQ):
    b, s, _ = p3.shape
    h = DIFF_HEADS
    vec = lambda a: a.reshape(1, -1)
    small = pl.BlockSpec((1, HEAD_DIM), lambda bi, hi: (0, 0))
    slab = lambda first: pl.BlockSpec((1, s, LANES), lambda bi, hi: (bi, 0, first + hi))
    in_specs = [
        slab(0),
        pl.BlockSpec((1, LANES, s), lambda bi, hi: (bi, hi, 0)),
        slab(h), slab(2 * h),
        pl.BlockSpec((2, N_BIAS_KINDS, TQ, LANES), lambda bi, hi: (hi, 0, 0, 0)),
        small, small, small, small,
        pl.BlockSpec((1, DIFF_VDIM), lambda bi, hi: (0, hi)),
    ]
    return pl.pallas_call(
        functools.partial(_diff_kernel, lambda_init=lambda_init, tq=tq),
        grid=(b, h),
        in_specs=in_specs,
        out_specs=slab(0),
        out_shape=jax.ShapeDtypeStruct((b, s, h * DIFF_VDIM), BF16),
        compiler_params=pltpu.CompilerParams(
            dimension_semantics=("arbitrary", "arbitrary"), vmem_limit_bytes=VMEM_LIMIT),
        name="diff_attention",
    )(p3, kt3, p3, p3, dn, vec(lq1), vec(lk1), vec(lq2), vec(lk2), vec(subln))


def _nsa_weight_layout(w_in):
    d = HEAD_DIM
    kv = NSA_GROUPS * d
    width = NSA_HEADS * d
    pts = np.cumsum([width] + [kv] * 6 + [3 * NSA_HEADS, width])
    q, kc, vc, ks, vs, kw, vw, gate, z = [w_in[:, a:b] for a, b in zip([0] + list(pts[:-1]), pts)]
    grp = lambda w, g: w[:, g * d:(g + 1) * d]
    dup = lambda w: jnp.concatenate(
        [jnp.concatenate([grp(w, g), grp(w, g)], axis=1) for g in range(NSA_GROUPS)], axis=1)
    gate_pad = jnp.pad(gate, ((0, 0), (0, LANES - gate.shape[1])))
    parts = [("q", q * (d ** -0.5 * LOG2E)), ("z", z), ("vsel", dup(vs)), ("vwin", dup(vw)),
             ("gate", gate_pad)]
    col, off = {}, 0
    for name, w in parts:
        col[name] = off
        off += w.shape[1]
    w_main = jnp.concatenate([w for _, w in parts], axis=1).astype(BF16)
    w_keys_t = jnp.concatenate([ks, kw], axis=1).T.astype(BF16)
    w_cmp = jnp.concatenate(
        [jnp.concatenate([grp(kc, g), grp(vc, g)], axis=1) for g in range(NSA_GROUPS)],
        axis=1).astype(BF16)
    return w_main, w_cmp, w_keys_t, col


def _compress_weight_layout(pe_k, w1_k, w2_k, pe_v, w1_v, w2_v):
    d, hdn = HEAD_DIM, CMP_HIDDEN
    pe_kv = jnp.concatenate([pe_k, pe_v], axis=1)
    w1k = w1_k.reshape(CMP_BLOCK, d, hdn)
    w1v = w1_v.reshape(CMP_BLOCK, d, hdn)
    zero = jnp.zeros_like(w1k)
    w1_kv = jnp.concatenate([jnp.concatenate([w1k, zero], axis=2),
                             jnp.concatenate([zero, w1v], axis=2)], axis=1).astype(BF16)
    z2 = jnp.zeros((hdn, 2 * d), w2_k.dtype)
    w2_kv = jnp.concatenate([jnp.concatenate([w2_k, w2_k, z2], axis=1),
                             jnp.concatenate([z2, w2_v, w2_v], axis=1)], axis=0).astype(BF16)
    return pe_kv, w1_kv, w2_kv


def _selection_constants(seq):
    n_cmp_pad = seq // CMP_STRIDE
    n_cmp = (seq - CMP_BLOCK) // CMP_STRIDE + 1
    n_sel = seq // SEL_BLOCK
    cmp_lo = np.arange(n_cmp_pad) * CMP_STRIDE
    sel_lo = np.arange(n_sel) * SEL_BLOCK
    overlap = np.maximum(np.minimum(cmp_lo[:, None] + CMP_BLOCK, sel_lo[None, :] + SEL_BLOCK)
                         - np.maximum(cmp_lo[:, None], sel_lo[None, :]), 0).astype(np.float32) / CMP_BLOCK
    overlap[n_cmp:] = 0.0
    ovt = np.zeros((LANES, n_cmp_pad), np.float32)
    ovt[:n_sel] = overlap.T
    et = np.zeros((LANES, seq), np.float32)
    et[np.arange(seq) // SEL_BLOCK, np.arange(seq)] = 1.0
    n_pairs = NSA_HPG // 2
    gsel = np.zeros((NSA_GROUPS, LANES, 3 * n_pairs * LANES), np.float32)
    for g in range(NSA_GROUPS):
        for pr in range(n_pairs):
            for i in range(3):
                for n in range(LANES):
                    head = g * NSA_HPG + 2 * pr + n // HEAD_DIM
                    gsel[g, 3 * head + i, (pr * 3 + i) * LANES + n] = 1.0
    return jnp.asarray(ovt), jnp.asarray(et, dtype=BF16), jnp.asarray(gsel, dtype=BF16)


def kernel(x, rel_bias_table, norm_pre, norm_post, nsa_w_in, nsa_cmp_pe_k, nsa_cmp_w1_k, nsa_cmp_w2_k,
           nsa_cmp_pe_v, nsa_cmp_w1_v, nsa_cmp_w2_v, nsa_w_out, diff_w_in, diff_lambda_q1,
           diff_lambda_k1, diff_lambda_q2, diff_lambda_k2, diff_subln, diff_w_out):
    b, s, d = x.shape
    n = b * s
    assert d == D_MODEL and s % TK == 0 and s // SEL_BLOCK <= LANES and s // CMP_STRIDE == LANES
    dn, bc = _bias_tiles(rel_bias_table, s)
    ovt, et, gsel = _selection_constants(s)
    x2 = x.reshape(n, d)
    pending = None

    def project(i, weights, out_dtypes, weights_t):
        outs = _norm_proj(x2, norm_pre[i], s, weights, out_dtypes, weights_t, pending)
        return (outs[:-1], outs[-1]) if pending is not None else (outs, x2)

    for i in range(DEPTH):
        j = i // 2
        if i % 2 == 0:
            w_main, w_cmp, w_keys_t, col = _nsa_weight_layout(nsa_w_in[j])
            (p_main, p_cmp, keys_t), x2 = project(i, [w_main, w_cmp], [BF16, F32], [w_keys_t])
            kvc = _compress(p_cmp.reshape(b, s, -1),
                            *_compress_weight_layout(nsa_cmp_pe_k[j], nsa_cmp_w1_k[j], nsa_cmp_w2_k[j],
                                                     nsa_cmp_pe_v[j], nsa_cmp_w1_v[j], nsa_cmp_w2_v[j]))
            o = _nsa_attention(p_main.reshape(b, s, -1), keys_t, kvc, dn, bc, ovt, et, gsel, col)
            w_out = nsa_w_out[j]
        else:
            lambda_init = 0.8 - 0.6 * math.exp(-0.3 * i)
            w = diff_w_in[j]
            w_main = jnp.concatenate([w[:, :d] * (HEAD_DIM ** -0.5 * LOG2E), w[:, 2 * d:]],
                                     axis=1).astype(BF16)
            (p_main, keys_t), x2 = project(i, [w_main], [BF16], [w[:, d:2 * d].astype(BF16).T])
            o = _diff_attention(p_main.reshape(b, s, -1), keys_t, dn, diff_lambda_q1[j],
                                diff_lambda_k1[j], diff_lambda_q2[j], diff_lambda_k2[j],
                                diff_subln[j], lambda_init)
            w_out = diff_w_out[j]
        pending = (o.reshape(n, -1), w_out.astype(BF16), norm_post[i])
    return _out_proj(pending[0], pending[1], x2, pending[2]).reshape(b, s, d)
```

```python
import functools
import math

import jax
import jax.numpy as jnp
import numpy as np
from jax import lax
from jax.experimental import pallas as pl
from jax.experimental.pallas import tpu as pltpu

F32 = jnp.float32
BF16 = jnp.bfloat16

D_MODEL = 1024
DEPTH = 2
REL_BUCKETS = 32
REL_MAX_DIST = 128
NSA_HEADS = 16
NSA_GROUPS = 4
NSA_HPG = NSA_HEADS // NSA_GROUPS
HEAD_DIM = 64
CMP_BLOCK = 32
CMP_STRIDE = 16
CMP_HIDDEN = 128
SEL_BLOCK = 64
SEL_TOPK = 8
WINDOW = 512
DIFF_HEADS = 8
DIFF_VDIM = 128
NEG = -1e30
BIG = 1e9
EPS = 1e-6

LANES = 128
TQ = 128
TK = 2 * LANES
N_BIAS_KINDS = 5
VMEM_LIMIT = 52 * 1024 * 1024
LOG2E = math.log2(math.e)


def _sigmoid(x):
    return 1.0 / (1.0 + jnp.exp(-x))


def _nt_dot(a, b, precision=None):
    return lax.dot_general(a, b, (((1,), (1,)), ((), ())), precision=precision,
                           preferred_element_type=F32)


def _resident(shape, index_map):
    return pl.BlockSpec(shape, index_map, pipeline_mode=pl.Buffered(1))


def _residual_update(o_ref, w_ref, x_ref, g_ref):
    y = jnp.dot(o_ref[...], w_ref[...], preferred_element_type=F32)
    return x_ref[...] + y * lax.rsqrt(jnp.mean(y * y, axis=-1, keepdims=True) + EPS) * g_ref[...]


def _norm_proj_kernel(*refs, n_w, n_wt, fused):
    if fused:
        x = _residual_update(refs[0], refs[1], refs[2], refs[3])
        refs[-1][...] = x
        refs = refs[4:-1]
    else:
        x = refs[0][...]
        refs = refs[1:]
    g_ref, refs = refs[0], refs[1:]
    w_refs, wt_refs, o_refs = refs[:n_w], refs[n_w:n_w + n_wt], refs[n_w + n_wt:]
    u = x * lax.rsqrt(jnp.mean(x * x, axis=-1, keepdims=True) + EPS) * g_ref[...]
    ub = u.astype(BF16)
    for w_ref, o_ref in zip(w_refs, o_refs[:n_w]):
        o_ref[...] = jnp.dot(ub, w_ref[...], preferred_element_type=F32).astype(o_ref.dtype)
    for wt_ref, o_ref in zip(wt_refs, o_refs[n_w:]):
        o_ref[0] = _nt_dot(wt_ref[...], ub).astype(o_ref.dtype)


def _norm_proj(x2, g, seq, weights, out_dtypes, weights_t=(), pending=None, tm=512):
    n, d = x2.shape
    n_st = seq // tm
    row_tile = lambda c: pl.BlockSpec((tm, c), lambda i: (i, 0))
    vector = pl.BlockSpec((1, d), lambda i: (0, 0))
    args, in_specs = [x2, g.reshape(1, d)], [row_tile(d), vector]
    if pending is not None:
        o2, w_out, g_post = pending
        args = [o2, w_out, x2, g_post.reshape(1, d), g.reshape(1, d)]
        in_specs = [row_tile(o2.shape[1]), _resident(w_out.shape, lambda i: (0, 0)), row_tile(d),
                    vector, vector]
    out_specs, out_shape = [], []
    for w, dt in zip(weights, out_dtypes):
        c = w.shape[1]
        in_specs.append(_resident((d, c), lambda i: (0, 0)))
        out_specs.append(pl.BlockSpec((tm, c), lambda i: (i, 0)))
        out_shape.append(jax.ShapeDtypeStruct((n, c), dt))
    for wt in weights_t:
        c = wt.shape[0]
        in_specs.append(_resident((c, d), lambda i: (0, 0)))
        out_specs.append(pl.BlockSpec((1, c, tm), lambda i: (i // n_st, 0, i % n_st)))
        out_shape.append(jax.ShapeDtypeStruct((n // seq, c, seq), BF16))
    if pending is not None:
        out_specs.append(row_tile(d))
        out_shape.append(jax.ShapeDtypeStruct((n, d), F32))
    return pl.pallas_call(
        functools.partial(_norm_proj_kernel, n_w=len(weights), n_wt=len(weights_t),
                          fused=pending is not None),
        grid=(n // tm,),
        in_specs=in_specs,
        out_specs=out_specs,
        out_shape=out_shape,
        compiler_params=pltpu.CompilerParams(
            dimension_semantics=("arbitrary",), vmem_limit_bytes=VMEM_LIMIT),
        name="norm_proj",
    )(*args, *weights, *weights_t)


def _out_proj_kernel(o_ref, w_ref, x_ref, g_ref, y_ref):
    y_ref[...] = _residual_update(o_ref, w_ref, x_ref, g_ref)


def _out_proj(o2, w, x2, g, tm=512):
    n, d = x2.shape
    return pl.pallas_call(
        _out_proj_kernel,
        grid=(n // tm,),
        in_specs=[pl.BlockSpec((tm, o2.shape[1]), lambda i: (i, 0)),
                  _resident(w.shape, lambda i: (0, 0)),
                  pl.BlockSpec((tm, d), lambda i: (i, 0)),
                  pl.BlockSpec((1, d), lambda i: (0, 0))],
        out_specs=pl.BlockSpec((tm, d), lambda i: (i, 0)),
        out_shape=jax.ShapeDtypeStruct((n, d), F32),
        compiler_params=pltpu.CompilerParams(
            dimension_semantics=("arbitrary",), vmem_limit_bytes=VMEM_LIMIT),
        name="out_proj",
    )(o2, w, x2, g.reshape(1, d))


def _bucket_starts(max_dist):
    max_exact = REL_BUCKETS // 2
    n = np.arange(max_dist)
    nf = np.maximum(n, 1).astype(np.float32)
    large = max_exact + (np.log(nf / max_exact) / np.float32(math.log(REL_MAX_DIST / max_exact))
                         * (REL_BUCKETS - max_exact)).astype(np.int32)
    bucket = np.where(n < max_exact, n, np.minimum(large, REL_BUCKETS - 1))
    assert (np.diff(bucket) >= 0).all()
    return [int(np.argmax(bucket >= b)) if (bucket >= b).any() else None
            for b in range(REL_BUCKETS)]


def _bias_kernel(tbl_ref, dn_ref, bc_ref):
    h = pl.program_id(0)
    c31 = tbl_ref[REL_BUCKETS - 1, h]
    starts = _bucket_starts(bc_ref.shape[1])

    def rel_bias(dist):
        n = jnp.maximum(dist, 0)
        out = jnp.full(dist.shape, tbl_ref[0, h], F32)
        for b in range(1, REL_BUCKETS):
            if starts[b] is not None:
                out = jnp.where(n >= starts[b], tbl_ref[b, h], out)
        return (out - c31) * LOG2E

    i = lax.broadcasted_iota(jnp.int32, (TQ, LANES), 0)
    j = lax.broadcasted_iota(jnp.int32, (TQ, LANES), 1)
    dn_ref[0, 0] = jnp.where(j <= i, rel_bias(i - j), NEG)
    dn_ref[0, 1] = rel_bias(LANES + i - j)
    dn_ref[0, 2] = jnp.zeros((TQ, LANES), F32)
    dn_ref[0, 3] = jnp.where(j > i, 0.0, NEG)
    dn_ref[0, 4] = jnp.full((TQ, LANES), NEG, F32)

    n_cmp = (bc_ref.shape[1] - CMP_BLOCK) // CMP_STRIDE + 1

    def body(r, carry):
        t = r * TQ + i
        d = t - (j * CMP_STRIDE + CMP_BLOCK - 1)
        valid = (d >= 0) & (j < n_cmp)
        bc_ref[0, pl.ds(pl.multiple_of(r * TQ, TQ), TQ), :] = jnp.where(valid, rel_bias(d), NEG)
        return carry

    lax.fori_loop(0, bc_ref.shape[1] // TQ, body, 0)


def _bias_tiles(table, seq):
    n_maps = table.shape[1]
    return pl.pallas_call(
        _bias_kernel,
        grid=(n_maps,),
        in_specs=[pl.BlockSpec(memory_space=pltpu.SMEM)],
        out_specs=[pl.BlockSpec((1, N_BIAS_KINDS, TQ, LANES), lambda h: (h, 0, 0, 0)),
                   pl.BlockSpec((1, seq, LANES), lambda h: (h, 0, 0))],
        out_shape=[jax.ShapeDtypeStruct((n_maps, N_BIAS_KINDS, TQ, LANES), F32),
                   jax.ShapeDtypeStruct((n_maps, seq, LANES), F32)],
        compiler_params=pltpu.CompilerParams(dimension_semantics=("arbitrary",)),
        name="bias_tiles",
    )(table)


def _compress_kernel(x_ref, pe_ref, w1_ref, w2_ref, o_ref, xp_ref):
    seq = x_ref.shape[1]
    n_chunks = seq // CMP_STRIDE
    xp_ref[0:seq, :] = x_ref[0]
    xp_ref[seq:, :] = jnp.zeros((xp_ref.shape[0] - seq, xp_ref.shape[1]), F32)
    hid = jnp.zeros((n_chunks, 2 * CMP_HIDDEN), F32)
    for l in range(CMP_BLOCK):
        a = xp_ref[pl.ds(l, n_chunks, stride=CMP_STRIDE), :]
        hid = hid + jnp.dot((a + pe_ref[l:l + 1, :]).astype(BF16), w1_ref[l],
                            preferred_element_type=F32)
    hid = hid * _sigmoid(hid)
    o_ref[0, 0] = jnp.dot(hid.astype(BF16), w2_ref[...], preferred_element_type=F32).astype(BF16)


def _compress(cmp_in, pe_kv, w1_kv, w2_kv):
    b, s, _ = cmp_in.shape
    n_chunks = s // CMP_STRIDE
    return pl.pallas_call(
        _compress_kernel,
        grid=(b, NSA_GROUPS),
        in_specs=[pl.BlockSpec((1, s, LANES), lambda bi, g: (bi, 0, g)),
                  pl.BlockSpec(pe_kv.shape, lambda bi, g: (0, 0)),
                  pl.BlockSpec(w1_kv.shape, lambda bi, g: (0, 0, 0)),
                  pl.BlockSpec(w2_kv.shape, lambda bi, g: (0, 0))],
        out_specs=pl.BlockSpec((1, 1, n_chunks, 2 * LANES), lambda bi, g: (bi, g, 0, 0)),
        out_shape=jax.ShapeDtypeStruct((b, NSA_GROUPS, n_chunks, 2 * LANES), BF16),
        scratch_shapes=[pltpu.VMEM((s + CMP_BLOCK, LANES), F32)],
        compiler_params=pltpu.CompilerParams(
            dimension_semantics=("arbitrary", "arbitrary"), vmem_limit_bytes=VMEM_LIMIT),
        name="nsa_compress",
    )(cmp_in, pe_kv, w1_kv, w2_kv)


def _tile_kind(delta, window=False):
    if delta < 0 or (window and delta > WINDOW // LANES):
        return 4
    if window and delta == WINDOW // LANES:
        return 3
    return min(delta, 2)


def _tile_kinds(row_blocks, steps, window=False):
    n_hf = TK // LANES
    return [[[(slot, _tile_kind(q_tile - (kp * n_hf + hf), window)) for hf in range(n_hf)]
             for slot, q_tile in row_blocks] for kp in steps]


def _score_steps(qx, kt_tiles, kinds, bias_ref, st):
    n_rb = qx.shape[0] // TQ
    n_hf = TK // LANES
    st["mx"] = [None] * n_rb
    st["sc"] = []
    for i, load_kt in enumerate(kt_tiles):
        s = jnp.dot(qx, load_kt(), preferred_element_type=F32)
        step = []
        for rb in range(n_rb):
            tiles = []
            for hf in range(n_hf):
                slot, kind = kinds[i][rb][hf]
                if kind == 4:
                    tiles.append(None)
                    continue
                t = s[rb * TQ:(rb + 1) * TQ, hf * LANES:(hf + 1) * LANES]
                if kind != 2:
                    t = t + bias_ref[slot, kind]
                tiles.append(t)
                st["mx"][rb] = t if st["mx"][rb] is None else jnp.maximum(st["mx"][rb], t)
            step.append(tiles)
        st["sc"].append(step)
        yield


def _weight_steps(v_tiles, st):
    n_rb = len(st["mx"])
    mb = [jnp.broadcast_to(jnp.max(v, axis=-1, keepdims=True), (TQ, LANES)) for v in st["mx"]]
    st["mb"] = mb
    st["acc"] = jnp.zeros((n_rb * TQ, 2 * LANES), F32)
    for i, load_v in enumerate(v_tiles):
        p = jnp.concatenate(
            [jnp.concatenate([jnp.zeros((TQ, LANES), F32) if t is None else jnp.exp2(t - mb[rb])
                              for t in st["sc"][i][rb]], axis=1) for rb in range(n_rb)], axis=0)
        st["acc"] = st["acc"] + jnp.dot(p.astype(BF16), load_v(), preferred_element_type=F32)
        yield


def _chain(*gens):
    for g in gens:
        yield from g


def _pipeline_tiles(order, n_steps, setup, score_steps, weight_steps, finish):
    n = len(order)
    state = {}
    qk_at = [0, 0]

    def qk_stream():
        state[order[0]] = setup(order[0])
        for i, x in enumerate(order):
            if i + 1 < n:
                state[order[i + 1]] = setup(order[i + 1])
            qk_at[:] = [i, 0]
            for _ in score_steps(x, state[x]):
                qk_at[1] += 1
                yield
        qk_at[:] = [n, 0]

    def pv_stream():
        for i, x in enumerate(order):
            while not (qk_at[0] > i + 1 or qk_at[0] >= n
                       or (qk_at[0] == i + 1 and 2 * qk_at[1] >= n_steps(order[i + 1]))):
                yield
            yield from weight_steps(x, state[x])
            finish(x, state.pop(x))

    live = [qk_stream(), pv_stream()]
    while live:
        for g in list(live):
            try:
                next(g)
            except StopIteration:
                live.remove(g)


def _merge_segments(segs):
    if len(segs) == 1:
        return segs[0]["acc"]
    n_rb = len(segs[0]["mb"])
    top = [functools.reduce(jnp.maximum, [st["mb"][rb] for st in segs]) for rb in range(n_rb)]
    total = None
    for st in segs:
        w = jnp.concatenate([jnp.exp2(st["mb"][rb] - top[rb]) for rb in range(n_rb)], axis=0)
        part = st["acc"] * jnp.concatenate([w, w], axis=1)
        total = part if total is None else total + part
    return total


def _split_steps(steps, max_len=4):
    steps = list(steps)
    n_seg = -(-len(steps) // max_len)
    cuts = [round(i * len(steps) / n_seg) for i in range(n_seg + 1)]
    return [steps[a:b] for a, b in zip(cuts[:-1], cuts[1:])]


def _tile_order(n):
    return list(range(0, n, 2)) + list(range(n - 1 - n % 2, 0, -2))


def _with_ones(v_tile):
    return jnp.concatenate([v_tile, jnp.ones(v_tile.shape, v_tile.dtype)], axis=1)


def _nsa_kernel(q_ref, kst_ref, kwt_ref, vs_ref, vw_ref, kvc_ref, gate_ref, z_ref, dn_ref, bc_ref,
                ovt_ref, et_ref, gsel_ref, o_ref, *, tq):
    seq = q_ref.shape[1]
    n_rb = tq // TQ
    n_sel = seq // SEL_BLOCK

    keys = lambda kp: slice(kp * TK, (kp + 1) * TK)

    n_pairs = NSA_HPG // 2
    tiles = {}

    def tile_setup(x):
        lane = lax.broadcasted_iota(jnp.int32, (tq, LANES), 1)
        lo_half = lane < HEAD_DIM
        blk = lax.broadcasted_iota(jnp.int32, (n_sel, tq), 0)
        tok = lax.broadcasted_iota(jnp.int32, (n_sel, tq), 1)
        row = lax.broadcasted_iota(jnp.int32, (tq, 1), 0)
        kcc = kvc_ref[0, 0, :, 0:LANES]
        vcc = kvc_ref[0, 0, :, LANES:2 * LANES]
        rows = slice(x * tq, (x + 1) * tq)
        q = q_ref[0, rows, :]
        qx = []
        for j in range(NSA_HPG):
            slab = q[:, (j // 2) * LANES:(j // 2 + 1) * LANES]
            keep = lo_half if j % 2 == 0 else jnp.logical_not(lo_half)
            qx.append(jnp.where(keep, slab, jnp.zeros_like(slab)))

        any_valid = (x * tq + row >= CMP_BLOCK - 1).astype(F32)
        p_sum = jnp.zeros((tq, LANES), F32)
        o_c = []
        for j in range(NSA_HPG):
            sj = _nt_dot(qx[j], kcc) + bc_ref[j, rows, :]
            pj = jnp.exp2(sj - jnp.max(sj, axis=-1, keepdims=True))
            pv = jnp.dot(pj.astype(BF16), _with_ones(vcc), preferred_element_type=F32)
            inv = any_valid / pv[:, LANES:]
            p_sum = p_sum + pj * inv
            o_c.append(pv[:, :LANES] * inv)

        imp = _nt_dot(ovt_ref[...], p_sum, precision=lax.Precision.HIGHEST)[0:n_sel, :]
        cur = jnp.right_shift(x * tq + tok, int(math.log2(SEL_BLOCK)))
        forced = (blk == 0) | (blk == cur) | (blk == cur - 1)
        val = jnp.where(forced, BIG, jnp.where(blk > cur, -BIG, imp))
        rank = jnp.zeros((n_sel, tq), F32)
        for other in range(min(n_sel, (x * tq + tq - 1) // SEL_BLOCK + 1)):
            r = val[other:other + 1, :]
            ahead = (r > val) | ((r == val) & (blk > other))
            rank = rank + jnp.where(ahead, 1.0, 0.0)
        block_mask_t = jnp.where(rank < float(min(SEL_TOPK, n_sel)), 0.0, NEG)
        block_mask = jnp.concatenate(
            [block_mask_t, jnp.zeros((LANES - n_sel, tq), F32)], axis=0).T.astype(BF16)

        return dict(qx=qx, block_mask=block_mask, o_c=o_c)

    order = _tile_order(seq // tq)
    started = []

    pairs = {}

    def segments_of(x):
        sel_steps = range((x * tq + tq - 1) // TK + 1)
        win_steps = range(max(0, (x * tq - (WINDOW - 1)) // TK), sel_steps[-1] + 1)
        return [("win", list(win_steps))] + [("sel", seg) for seg in _split_steps(sel_steps)]

    def setup(unit):
        x, pr, _ = unit
        if not started:
            started.append(True)
            for y in order:
                tiles[y] = tile_setup(y)
        if (x, pr) not in pairs:
            t = tiles[x] if pr + 1 < n_pairs else tiles.pop(x)
            q_pair = jnp.concatenate(t["qx"][2 * pr:2 * pr + 2], axis=0)
            q_sel = jnp.concatenate([q_pair, jnp.concatenate([t["block_mask"]] * 2, axis=0)], axis=1)
            pairs[(x, pr)] = dict(win=q_pair, sel=q_sel, o_c=t["o_c"][2 * pr:2 * pr + 2],
                                  done=dict(win=[], sel=[]))
        return {}

    both = lambda t: jnp.concatenate([t, t], axis=0)

    def score_steps(unit, st):
        x, pr, seg = unit
        branch, steps = segments_of(x)[seg]
        row_blocks = [(j, x * n_rb + r) for j in (2 * pr, 2 * pr + 1) for r in range(n_rb)]
        kinds = _tile_kinds(row_blocks, steps, window=branch == "win")
        if branch == "win":
            kt_tiles = [(lambda kp=kp: both(kwt_ref[0, :, keys(kp)])) for kp in steps]
        else:
            kt_tiles = [(lambda kp=kp: jnp.concatenate(
                [both(kst_ref[0, :, keys(kp)]), et_ref[:, keys(kp)]], axis=0)) for kp in steps]
        return _score_steps(pairs[(x, pr)][branch], kt_tiles, kinds, dn_ref, st)

    def weight_steps(unit, st):
        x, pr, seg = unit
        branch, steps = segments_of(x)[seg]
        v_ref = vw_ref if branch == "win" else vs_ref
        return _weight_steps([(lambda kp=kp: _with_ones(v_ref[0, keys(kp), :])) for kp in steps], st)

    def finish(unit, st):
        x, pr, seg = unit
        pairs[(x, pr)]["done"][segments_of(x)[seg][0]].append(st)
        if seg + 1 < len(segments_of(x)):
            return
        pair = pairs.pop((x, pr))
        rows = slice(x * tq, (x + 1) * tq)
        cols = slice(pr * LANES, (pr + 1) * LANES)
        lo_half = lax.broadcasted_iota(jnp.int32, (tq, LANES), 1) < HEAD_DIM
        gates = _sigmoid(jnp.dot(gate_ref[0, rows, :], gsel_ref[0, :, 3 * pr * LANES:3 * (pr + 1) * LANES],
                                 preferred_element_type=F32))
        merged = lambda a: jnp.where(lo_half, a[:tq, :], a[tq:, :])

        def ratio(segs):
            acc = _merge_segments(segs)
            return merged(acc[:, :LANES] / acc[:, LANES:])

        o = (gates[:, 0:LANES] * jnp.where(lo_half, pair["o_c"][0], pair["o_c"][1])
             + gates[:, LANES:2 * LANES] * ratio(pair["done"]["sel"])
             + gates[:, 2 * LANES:3 * LANES] * ratio(pair["done"]["win"]))
        z = z_ref[0, rows, cols].astype(F32)
        o_ref[0, rows, cols] = (o * (z * _sigmoid(z))).astype(o_ref.dtype)

    units = [(x, pr, seg) for x in order for pr in range(n_pairs)
             for seg in range(len(segments_of(x)))]
    _pipeline_tiles(units, lambda u: len(segments_of(u[0])[u[2]][1]),
                    setup, score_steps, weight_steps, finish)


def _nsa_attention(p3, kt3, kvc, dn, bc, ovt, et, gsel, col, tq=2 * TQ):
    b, s, _ = p3.shape
    wide = 2 * LANES

    def slab(first_lane_block, per_group, width=LANES):
        return pl.BlockSpec((1, s, width), lambda g, bi: (bi, 0, first_lane_block + per_group * g))

    in_specs = [
        slab(col["q"] // wide, 1, wide),
        pl.BlockSpec((1, HEAD_DIM, s), lambda g, bi: (bi, g, 0)),
        pl.BlockSpec((1, HEAD_DIM, s), lambda g, bi: (bi, NSA_GROUPS + g, 0)),
        slab(col["vsel"] // LANES, 1), slab(col["vwin"] // LANES, 1),
        pl.BlockSpec((1, 1, kvc.shape[2], wide), lambda g, bi: (bi, g, 0, 0)),
        slab(col["gate"] // LANES, 0),
        slab(col["z"] // wide, 1, wide),
        _resident((NSA_HPG, N_BIAS_KINDS, TQ, LANES), lambda g, bi: (g, 0, 0, 0)),
        _resident((NSA_HPG, s, LANES), lambda g, bi: (g, 0, 0)),
        _resident(ovt.shape, lambda g, bi: (0, 0)),
        _resident(et.shape, lambda g, bi: (0, 0)),
        _resident((1,) + gsel.shape[1:], lambda g, bi: (g, 0, 0)),
    ]
    return pl.pallas_call(
        functools.partial(_nsa_kernel, tq=tq),
        grid=(NSA_GROUPS, b),
        in_specs=in_specs,
        out_specs=pl.BlockSpec((1, s, wide), lambda g, bi: (bi, 0, g)),
        out_shape=jax.ShapeDtypeStruct((b, s, NSA_GROUPS * wide), BF16),
        compiler_params=pltpu.CompilerParams(
            dimension_semantics=("arbitrary", "arbitrary"), vmem_limit_bytes=VMEM_LIMIT),
        name="nsa_attention",
    )(p3, kt3, kt3, p3, p3, kvc, p3, p3, dn, bc, ovt, et, gsel)


def _diff_kernel(q_ref, kt_ref, v_ref, z_ref, dn_ref, lq1_ref, lk1_ref, lq2_ref, lk2_ref, sub_ref,
                 o_ref, *, lambda_init, tq):
    seq = q_ref.shape[1]
    n_rb = tq // TQ
    lane = lax.broadcasted_iota(jnp.int32, (tq, LANES), 1)
    lam = (jnp.exp(jnp.sum(lq1_ref[...] * lk1_ref[...], axis=-1, keepdims=True))
           - jnp.exp(jnp.sum(lq2_ref[...] * lk2_ref[...], axis=-1, keepdims=True)) + lambda_init)
    keys = lambda kp: slice(kp * TK, (kp + 1) * TK)
    segments_of = lambda x: _split_steps(range((x * tq + tq - 1) // TK + 1))
    queries, done = {}, {}

    def setup(unit):
        x = unit[0]
        if x not in queries:
            q = q_ref[0, x * tq:(x + 1) * tq, :]
            zero = jnp.zeros_like(q)
            queries[x] = jnp.concatenate([jnp.where(lane < HEAD_DIM, q, zero),
                                          jnp.where(lane >= HEAD_DIM, q, zero)], axis=0)
        return {}

    def score_steps(unit, st):
        x, seg = unit
        steps = segments_of(x)[seg]
        row_blocks = [(m, x * n_rb + r) for m in range(2) for r in range(n_rb)]
        return _score_steps(queries[x], [(lambda kp=kp: kt_ref[0, :, keys(kp)]) for kp in steps],
                            _tile_kinds(row_blocks, steps), dn_ref, st)

    def weight_steps(unit, st):
        x, seg = unit
        return _weight_steps([(lambda kp=kp: _with_ones(v_ref[0, keys(kp), :]))
                              for kp in segments_of(x)[seg]], st)

    def finish(unit, st):
        x, seg = unit
        done.setdefault(x, []).append(st)
        if seg + 1 < len(segments_of(x)):
            return
        del queries[x]
        acc = _merge_segments(done.pop(x))
        rows = slice(x * tq, (x + 1) * tq)
        a = acc[:, :LANES] / acc[:, LANES:]
        o = a[:tq, :] - lam * a[tq:, :]
        o = o * lax.rsqrt(jnp.mean(o * o, axis=-1, keepdims=True) + EPS) * sub_ref[...]
        o = o * (1.0 - lambda_init)
        z = z_ref[0, rows, :].astype(F32)
        o_ref[0, rows, :] = (o * (z * _sigmoid(z))).astype(o_ref.dtype)

    units = [(x, seg) for x in _tile_order(seq // tq) for seg in range(len(segments_of(x)))]
    _pipeline_tiles(units, lambda u: len(segments_of(u[0])[u[1]]),
                    setup, score_steps, weight_steps, finish)


def _diff_attention(p3, kt3, dn, lq1, lk1, lq2, lk2, subln, lambda_init, tq=2 * TQ):
    b, s, _ = p3.shape
    h = DIFF_HEADS
    vec = lambda a: a.reshape(1, -1)
    small = pl.BlockSpec((1, HEAD_DIM), lambda bi, hi: (0, 0))
    slab = lambda first: pl.BlockSpec((1, s, LANES), lambda bi, hi: (bi, 0, first + hi))
    in_specs = [
        slab(0),
        pl.BlockSpec((1, LANES, s), lambda bi, hi: (bi, hi, 0)),
        slab(h), slab(2 * h),
        pl.BlockSpec((2, N_BIAS_KINDS, TQ, LANES), lambda bi, hi: (hi, 0, 0, 0)),
        small, small, small, small,
        pl.BlockSpec((1, DIFF_VDIM), lambda bi, hi: (0, hi)),
    ]
    return pl.pallas_call(
        functools.partial(_diff_kernel, lambda_init=lambda_init, tq=tq),
        grid=(b, h),
        in_specs=in_specs,
        out_specs=slab(0),
        out_shape=jax.ShapeDtypeStruct((b, s, h * DIFF_VDIM), BF16),
        compiler_params=pltpu.CompilerParams(
            dimension_semantics=("arbitrary", "arbitrary"), vmem_limit_bytes=VMEM_LIMIT),
        name="diff_attention",
    )(p3, kt3, p3, p3, dn, vec(lq1), vec(lk1), vec(lq2), vec(lk2), vec(subln))


def _nsa_weight_layout(w_in):
    d = HEAD_DIM
    kv = NSA_GROUPS * d
    width = NSA_HEADS * d
    pts = np.cumsum([width] + [kv] * 6 + [3 * NSA_HEADS, width])
    q, kc, vc, ks, vs, kw, vw, gate, z = [w_in[:, a:b] for a, b in zip([0] + list(pts[:-1]), pts)]
    grp = lambda w, g: w[:, g * d:(g + 1) * d]
    dup = lambda w: jnp.concatenate(
        [jnp.concatenate([grp(w, g), grp(w, g)], axis=1) for g in range(NSA_GROUPS)], axis=1)
    gate_pad = jnp.pad(gate, ((0, 0), (0, LANES - gate.shape[1])))
    parts = [("q", q * (d ** -0.5 * LOG2E)), ("z", z), ("vsel", dup(vs)), ("vwin", dup(vw)),
             ("gate", gate_pad)]
    col, off = {}, 0
    for name, w in parts:
        col[name] = off
        off += w.shape[1]
    w_main = jnp.concatenate([w for _, w in parts], axis=1).astype(BF16)
    w_keys_t = jnp.concatenate([ks, kw], axis=1).T.astype(BF16)
    w_cmp = jnp.concatenate(
        [jnp.concatenate([grp(kc, g), grp(vc, g)], axis=1) for g in range(NSA_GROUPS)],
        axis=1).astype(BF16)
    return w_main, w_cmp, w_keys_t, col


def _compress_weight_layout(pe_k, w1_k, w2_k, pe_v, w1_v, w2_v):
    d, hdn = HEAD_DIM, CMP_HIDDEN
    pe_kv = jnp.concatenate([pe_k, pe_v], axis=1)
    w1k = w1_k.reshape(CMP_BLOCK, d, hdn)
    w1v = w1_v.reshape(CMP_BLOCK, d, hdn)
    zero = jnp.zeros_like(w1k)
    w1_kv = jnp.concatenate([jnp.concatenate([w1k, zero], axis=2),
                             jnp.concatenate([zero, w1v], axis=2)], axis=1).astype(BF16)
    z2 = jnp.zeros((hdn, 2 * d), w2_k.dtype)
    w2_kv = jnp.concatenate([jnp.concatenate([w2_k, w2_k, z2], axis=1),
                             jnp.concatenate([z2, w2_v, w2_v], axis=1)], axis=0).astype(BF16)
    return pe_kv, w1_kv, w2_kv


def _selection_constants(seq):
    n_cmp_pad = seq // CMP_STRIDE
    n_cmp = (seq - CMP_BLOCK) // CMP_STRIDE + 1
    n_sel = seq // SEL_BLOCK
    cmp_lo = np.arange(n_cmp_pad) * CMP_STRIDE
    sel_lo = np.arange(n_sel) * SEL_BLOCK
    overlap = np.maximum(np.minimum(cmp_lo[:, None] + CMP_BLOCK, sel_lo[None, :] + SEL_BLOCK)
                         - np.maximum(cmp_lo[:, None], sel_lo[None, :]), 0).astype(np.float32) / CMP_BLOCK
    overlap[n_cmp:] = 0.0
    ovt = np.zeros((LANES, n_cmp_pad), np.float32)
    ovt[:n_sel] = overlap.T
    et = np.zeros((LANES, seq), np.float32)
    et[np.arange(seq) // SEL_BLOCK, np.arange(seq)] = 1.0
    n_pairs = NSA_HPG // 2
    gsel = np.zeros((NSA_GROUPS, LANES, 3 * n_pairs * LANES), np.float32)
    for g in range(NSA_GROUPS):
        for pr in range(n_pairs):
            for i in range(3):
                for n in range(LANES):
                    head = g * NSA_HPG + 2 * pr + n // HEAD_DIM
                    gsel[g, 3 * head + i, (pr * 3 + i) * LANES + n] = 1.0
    return jnp.asarray(ovt), jnp.asarray(et, dtype=BF16), jnp.asarray(gsel, dtype=BF16)


def kernel(x, rel_bias_table, norm_pre, norm_post, nsa_w_in, nsa_cmp_pe_k, nsa_cmp_w1_k, nsa_cmp_w2_k,
           nsa_cmp_pe_v, nsa_cmp_w1_v, nsa_cmp_w2_v, nsa_w_out, diff_w_in, diff_lambda_q1,
           diff_lambda_k1, diff_lambda_q2, diff_lambda_k2, diff_subln, diff_w_out):
    b, s, d = x.shape
    n = b * s
    assert d == D_MODEL and s % TK == 0 and s // SEL_BLOCK <= LANES and s // CMP_STRIDE == LANES
    dn, bc = _bias_tiles(rel_bias_table, s)
    ovt, et, gsel = _selection_constants(s)
    x2 = x.reshape(n, d)
    pending = None

    def project(i, weights, out_dtypes, weights_t):
        outs = _norm_proj(x2, norm_pre[i], s, weights, out_dtypes, weights_t, pending)
        return (outs[:-1], outs[-1]) if pending is not None else (outs, x2)

    for i in range(DEPTH):
        j = i // 2
        if i % 2 == 0:
            w_main, w_cmp, w_keys_t, col = _nsa_weight_layout(nsa_w_in[j])
            (p_main, p_cmp, keys_t), x2 = project(i, [w_main, w_cmp], [BF16, F32], [w_keys_t])
            kvc = _compress(p_cmp.reshape(b, s, -1),
                            *_compress_weight_layout(nsa_cmp_pe_k[j], nsa_cmp_w1_k[j], nsa_cmp_w2_k[j],
                                                     nsa_cmp_pe_v[j], nsa_cmp_w1_v[j], nsa_cmp_w2_v[j]))
            o = _nsa_attention(p_main.reshape(b, s, -1), keys_t, kvc, dn, bc, ovt, et, gsel, col)
            w_out = nsa_w_out[j]
        else:
            lambda_init = 0.8 - 0.6 * math.exp(-0.3 * i)
            w = diff_w_in[j]
            w_main = jnp.concatenate([w[:, :d] * (HEAD_DIM ** -0.5 * LOG2E), w[:, 2 * d:]],
                                     axis=1).astype(BF16)
            (p_main, keys_t), x2 = project(i, [w_main], [BF16], [w[:, d:2 * d].astype(BF16).T])
            o = _diff_attention(p_main.reshape(b, s, -1), keys_t, dn, diff_lambda_q1[j],
                                diff_lambda_k1[j], diff_lambda_q2[j], diff_lambda_k2[j],
                                diff_subln[j], lambda_init)
            w_out = diff_w_out[j]
        pending = (o.reshape(n, -1), w_out.astype(BF16), norm_post[i])
    return _out_proj(pending[0], pending[1], x2, pending[2]).reshape(b, s, d)
```

```python
import functools
import math

import jax
import jax.numpy as jnp
import numpy as np
from jax import lax
from jax.experimental import pallas as pl
from jax.experimental.pallas import tpu as pltpu

F32 = jnp.float32
BF16 = jnp.bfloat16

D_MODEL = 1024
DEPTH = 2
REL_BUCKETS = 32
REL_MAX_DIST = 128
NSA_HEADS = 16
NSA_GROUPS = 4
NSA_HPG = NSA_HEADS // NSA_GROUPS
HEAD_DIM = 64
CMP_BLOCK = 32
CMP_STRIDE = 16
CMP_HIDDEN = 128
SEL_BLOCK = 64
SEL_TOPK = 8
WINDOW = 512
DIFF_HEADS = 8
DIFF_VDIM = 128
NEG = -1e30
BIG = 1e9
EPS = 1e-6

LANES = 128
TQ = 128
TK = 2 * LANES
N_BIAS_KINDS = 5
VMEM_LIMIT = 52 * 1024 * 1024
LOG2E = math.log2(math.e)


def _sigmoid(x):
    return 1.0 / (1.0 + jnp.exp(-x))


def _nt_dot(a, b, precision=None):
    return lax.dot_general(a, b, (((1,), (1,)), ((), ())), precision=precision,
                           preferred_element_type=F32)


def _resident(shape, index_map):
    return pl.BlockSpec(shape, index_map, pipeline_mode=pl.Buffered(1))


def _residual_update(o_ref, w_ref, x_ref, g_ref):
    y = jnp.dot(o_ref[...], w_ref[...], preferred_element_type=F32)
    return x_ref[...] + y * lax.rsqrt(jnp.mean(y * y, axis=-1, keepdims=True) + EPS) * g_ref[...]


def _norm_proj_kernel(*refs, n_w, n_wt, fused):
    if fused:
        x = _residual_update(refs[0], refs[1], refs[2], refs[3])
        refs[-1][...] = x
        refs = refs[4:-1]
    else:
        x = refs[0][...]
        refs = refs[1:]
    g_ref, refs = refs[0], refs[1:]
    w_refs, wt_refs, o_refs = refs[:n_w], refs[n_w:n_w + n_wt], refs[n_w + n_wt:]
    u = x * lax.rsqrt(jnp.mean(x * x, axis=-1, keepdims=True) + EPS) * g_ref[...]
    ub = u.astype(BF16)
    for w_ref, o_ref in zip(w_refs, o_refs[:n_w]):
        o_ref[...] = jnp.dot(ub, w_ref[...], preferred_element_type=F32).astype(o_ref.dtype)
    for wt_ref, o_ref in zip(wt_refs, o_refs[n_w:]):
        o_ref[0] = _nt_dot(wt_ref[...], ub).astype(o_ref.dtype)


def _norm_proj(x2, g, seq, weights, out_dtypes, weights_t=(), pending=None, tm=512):
    n, d = x2.shape
    n_st = seq // tm
    row_tile = lambda c: pl.BlockSpec((tm, c), lambda i: (i, 0))
    vector = pl.BlockSpec((1, d), lambda i: (0, 0))
    args, in_specs = [x2, g.reshape(1, d)], [row_tile(d), vector]
    if pending is not None:
        o2, w_out, g_post = pending
        args = [o2, w_out, x2, g_post.reshape(1, d), g.reshape(1, d)]
        in_specs = [row_tile(o2.shape[1]), _resident(w_out.shape, lambda i: (0, 0)), row_tile(d),
                    vector, vector]
    out_specs, out_shape = [], []
    for w, dt in zip(weights, out_dtypes):
        c = w.shape[1]
        in_specs.append(_resident((d, c), lambda i: (0, 0)))
        out_specs.append(pl.BlockSpec((tm, c), lambda i: (i, 0)))
        out_shape.append(jax.ShapeDtypeStruct((n, c), dt))
    for wt in weights_t:
        c = wt.shape[0]
        in_specs.append(_resident((c, d), lambda i: (0, 0)))
        out_specs.append(pl.BlockSpec((1, c, tm), lambda i: (i // n_st, 0, i % n_st)))
        out_shape.append(jax.ShapeDtypeStruct((n // seq, c, seq), BF16))
    if pending is not None:
        out_specs.append(row_tile(d))
        out_shape.append(jax.ShapeDtypeStruct((n, d), F32))
    return pl.pallas_call(
        functools.partial(_norm_proj_kernel, n_w=len(weights), n_wt=len(weights_t),
                          fused=pending is not None),
        grid=(n // tm,),
        in_specs=in_specs,
        out_specs=out_specs,
        out_shape=out_shape,
        compiler_params=pltpu.CompilerParams(
            dimension_semantics=("arbitrary",), vmem_limit_bytes=VMEM_LIMIT),
        name="norm_proj",
    )(*args, *weights, *weights_t)


def _out_proj_kernel(o_ref, w_ref, x_ref, g_ref, y_ref):
    y_ref[...] = _residual_update(o_ref, w_ref, x_ref, g_ref)


def _out_proj(o2, w, x2, g, tm=1024):
    n, d = x2.shape
    return pl.pallas_call(
        _out_proj_kernel,
        grid=(n // tm,),
        in_specs=[pl.BlockSpec((tm, o2.shape[1]), lambda i: (i, 0)),
                  _resident(w.shape, lambda i: (0, 0)),
                  pl.BlockSpec((tm, d), lambda i: (i, 0)),
                  pl.BlockSpec((1, d), lambda i: (0, 0))],
        out_specs=pl.BlockSpec((tm, d), lambda i: (i, 0)),
        out_shape=jax.ShapeDtypeStruct((n, d), F32),
        compiler_params=pltpu.CompilerParams(
            dimension_semantics=("arbitrary",), vmem_limit_bytes=VMEM_LIMIT),
        name="out_proj",
    )(o2, w, x2, g.reshape(1, d))


def _bucket_starts(max_dist):
    max_exact = REL_BUCKETS // 2
    n = np.arange(max_dist)
    nf = np.maximum(n, 1).astype(np.float32)
    large = max_exact + (np.log(nf / max_exact) / np.float32(math.log(REL_MAX_DIST / max_exact))
                         * (REL_BUCKETS - max_exact)).astype(np.int32)
    bucket = np.where(n < max_exact, n, np.minimum(large, REL_BUCKETS - 1))
    assert (np.diff(bucket) >= 0).all()
    return [int(np.argmax(bucket >= b)) if (bucket >= b).any() else None
            for b in range(REL_BUCKETS)]


def _bias_kernel(tbl_ref, dn_ref, bc_ref):
    h = pl.program_id(0)
    c31 = tbl_ref[REL_BUCKETS - 1, h]
    starts = _bucket_starts(bc_ref.shape[1])

    def rel_bias(dist):
        n = jnp.maximum(dist, 0)
        out = jnp.full(dist.shape, tbl_ref[0, h], F32)
        for b in range(1, REL_BUCKETS):
            if starts[b] is not None:
                out = jnp.where(n >= starts[b], tbl_ref[b, h], out)
        return (out - c31) * LOG2E

    i = lax.broadcasted_iota(jnp.int32, (TQ, LANES), 0)
    j = lax.broadcasted_iota(jnp.int32, (TQ, LANES), 1)
    dn_ref[0, 0] = jnp.where(j <= i, rel_bias(i - j), NEG)
    dn_ref[0, 1] = rel_bias(LANES + i - j)
    dn_ref[0, 2] = jnp.zeros((TQ, LANES), F32)
    dn_ref[0, 3] = jnp.where(j > i, 0.0, NEG)
    dn_ref[0, 4] = jnp.full((TQ, LANES), NEG, F32)

    n_cmp = (bc_ref.shape[1] - CMP_BLOCK) // CMP_STRIDE + 1

    def body(r, carry):
        t = r * TQ + i
        d = t - (j * CMP_STRIDE + CMP_BLOCK - 1)
        valid = (d >= 0) & (j < n_cmp)
        bc_ref[0, pl.ds(pl.multiple_of(r * TQ, TQ), TQ), :] = jnp.where(valid, rel_bias(d), NEG)
        return carry

    lax.fori_loop(0, bc_ref.shape[1] // TQ, body, 0)


def _bias_tiles(table, seq):
    n_maps = table.shape[1]
    return pl.pallas_call(
        _bias_kernel,
        grid=(n_maps,),
        in_specs=[pl.BlockSpec(memory_space=pltpu.SMEM)],
        out_specs=[pl.BlockSpec((1, N_BIAS_KINDS, TQ, LANES), lambda h: (h, 0, 0, 0)),
                   pl.BlockSpec((1, seq, LANES), lambda h: (h, 0, 0))],
        out_shape=[jax.ShapeDtypeStruct((n_maps, N_BIAS_KINDS, TQ, LANES), F32),
                   jax.ShapeDtypeStruct((n_maps, seq, LANES), F32)],
        compiler_params=pltpu.CompilerParams(dimension_semantics=("arbitrary",)),
        name="bias_tiles",
    )(table)


def _compress_kernel(x_ref, pe_ref, w1_ref, w2_ref, o_ref, xp_ref):
    seq = x_ref.shape[1]
    n_chunks = seq // CMP_STRIDE
    xp_ref[0:seq, :] = x_ref[0]
    xp_ref[seq:, :] = jnp.zeros((xp_ref.shape[0] - seq, xp_ref.shape[1]), F32)
    hid = jnp.zeros((n_chunks, 2 * CMP_HIDDEN), F32)
    for l in range(CMP_BLOCK):
        a = xp_ref[pl.ds(l, n_chunks, stride=CMP_STRIDE), :]
        hid = hid + jnp.dot((a + pe_ref[l:l + 1, :]).astype(BF16), w1_ref[l],
                            preferred_element_type=F32)
    hid = hid * _sigmoid(hid)
    o_ref[0, 0] = jnp.dot(hid.astype(BF16), w2_ref[...], preferred_element_type=F32).astype(BF16)


def _compress(cmp_in, pe_kv, w1_kv, w2_kv):
    b, s, _ = cmp_in.shape
    n_chunks = s // CMP_STRIDE
    return pl.pallas_call(
        _compress_kernel,
        grid=(b, NSA_GROUPS),
        in_specs=[pl.BlockSpec((1, s, LANES), lambda bi, g: (bi, 0, g)),
                  pl.BlockSpec(pe_kv.shape, lambda bi, g: (0, 0)),
                  pl.BlockSpec(w1_kv.shape, lambda bi, g: (0, 0, 0)),
                  pl.BlockSpec(w2_kv.shape, lambda bi, g: (0, 0))],
        out_specs=pl.BlockSpec((1, 1, n_chunks, 2 * LANES), lambda bi, g: (bi, g, 0, 0)),
        out_shape=jax.ShapeDtypeStruct((b, NSA_GROUPS, n_chunks, 2 * LANES), BF16),
        scratch_shapes=[pltpu.VMEM((s + CMP_BLOCK, LANES), F32)],
        compiler_params=pltpu.CompilerParams(
            dimension_semantics=("arbitrary", "arbitrary"), vmem_limit_bytes=VMEM_LIMIT),
        name="nsa_compress",
    )(cmp_in, pe_kv, w1_kv, w2_kv)


def _tile_kind(delta, window=False):
    if delta < 0 or (window and delta > WINDOW // LANES):
        return 4
    if window and delta == WINDOW // LANES:
        return 3
    return min(delta, 2)


def _tile_kinds(row_blocks, steps, window=False):
    n_hf = TK // LANES
    return [[[(slot, _tile_kind(q_tile - (kp * n_hf + hf), window)) for hf in range(n_hf)]
             for slot, q_tile in row_blocks] for kp in steps]


def _score_steps(qx, kt_tiles, kinds, bias_ref, st):
    n_rb = qx.shape[0] // TQ
    n_hf = TK // LANES
    st["mx"] = [None] * n_rb
    st["sc"] = []
    for i, load_kt in enumerate(kt_tiles):
        s = jnp.dot(qx, load_kt(), preferred_element_type=F32)
        step = []
        for rb in range(n_rb):
            tiles = []
            for hf in range(n_hf):
                slot, kind = kinds[i][rb][hf]
                if kind == 4:
                    tiles.append(None)
                    continue
                t = s[rb * TQ:(rb + 1) * TQ, hf * LANES:(hf + 1) * LANES]
                if kind != 2:
                    t = t + bias_ref[slot, kind]
                tiles.append(t)
                st["mx"][rb] = t if st["mx"][rb] is None else jnp.maximum(st["mx"][rb], t)
            step.append(tiles)
        st["sc"].append(step)
        yield


def _weight_steps(v_tiles, st):
    n_rb = len(st["mx"])
    mb = [jnp.broadcast_to(jnp.max(v, axis=-1, keepdims=True), (TQ, LANES)) for v in st["mx"]]
    st["acc"] = jnp.zeros((n_rb * TQ, 2 * LANES), F32)
    for i, load_v in enumerate(v_tiles):
        p = jnp.concatenate(
            [jnp.concatenate([jnp.zeros((TQ, LANES), F32) if t is None else jnp.exp2(t - mb[rb])
                              for t in st["sc"][i][rb]], axis=1) for rb in range(n_rb)], axis=0)
        st["acc"] = st["acc"] + jnp.dot(p.astype(BF16), load_v(), preferred_element_type=F32)
        yield


def _chain(*gens):
    for g in gens:
        yield from g


def _pipeline_tiles(order, n_steps, setup, score_steps, weight_steps, finish):
    n = len(order)
    state = {}
    qk_at = [0, 0]

    def qk_stream():
        state[order[0]] = setup(order[0])
        for i, x in enumerate(order):
            if i + 1 < n:
                state[order[i + 1]] = setup(order[i + 1])
            qk_at[:] = [i, 0]
            for _ in score_steps(x, state[x]):
                qk_at[1] += 1
                yield
        qk_at[:] = [n, 0]

    def pv_stream():
        for i, x in enumerate(order):
            while not (qk_at[0] > i + 1 or qk_at[0] >= n
                       or (qk_at[0] == i + 1 and 2 * qk_at[1] >= n_steps(order[i + 1]))):
                yield
            yield from weight_steps(x, state[x])
            finish(x, state.pop(x))

    live = [qk_stream(), pv_stream()]
    while live:
        for g in list(live):
            try:
                next(g)
            except StopIteration:
                live.remove(g)


def _tile_order(n):
    return list(range(0, n, 2)) + list(range(n - 1 - n % 2, 0, -2))


def _with_ones(v_tile):
    return jnp.concatenate([v_tile, jnp.ones(v_tile.shape, v_tile.dtype)], axis=1)


def _nsa_kernel(q_ref, kst_ref, kwt_ref, vs_ref, vw_ref, kvc_ref, gate_ref, z_ref, dn_ref, bc_ref,
                ovt_ref, et_ref, gsel_ref, o_ref, *, tq):
    seq = q_ref.shape[1]
    n_rb = tq // TQ
    n_sel = seq // SEL_BLOCK

    keys = lambda kp: slice(kp * TK, (kp + 1) * TK)

    n_pairs = NSA_HPG // 2
    tiles = {}

    def tile_setup(x):
        lane = lax.broadcasted_iota(jnp.int32, (tq, LANES), 1)
        lo_half = lane < HEAD_DIM
        blk = lax.broadcasted_iota(jnp.int32, (n_sel, tq), 0)
        tok = lax.broadcasted_iota(jnp.int32, (n_sel, tq), 1)
        row = lax.broadcasted_iota(jnp.int32, (tq, 1), 0)
        kcc = kvc_ref[0, 0, :, 0:LANES]
        vcc = kvc_ref[0, 0, :, LANES:2 * LANES]
        rows = slice(x * tq, (x + 1) * tq)
        q = q_ref[0, rows, :]
        qx = []
        for j in range(NSA_HPG):
            slab = q[:, (j // 2) * LANES:(j // 2 + 1) * LANES]
            keep = lo_half if j % 2 == 0 else jnp.logical_not(lo_half)
            qx.append(jnp.where(keep, slab, jnp.zeros_like(slab)))

        any_valid = (x * tq + row >= CMP_BLOCK - 1).astype(F32)
        p_sum = jnp.zeros((tq, LANES), F32)
        o_c = []
        for j in range(NSA_HPG):
            sj = _nt_dot(qx[j], kcc) + bc_ref[j, rows, :]
            pj = jnp.exp2(sj - jnp.max(sj, axis=-1, keepdims=True))
            pv = jnp.dot(pj.astype(BF16), _with_ones(vcc), preferred_element_type=F32)
            inv = any_valid / pv[:, LANES:]
            p_sum = p_sum + pj * inv
            o_c.append(pv[:, :LANES] * inv)

        imp = _nt_dot(ovt_ref[...], p_sum, precision=lax.Precision.HIGHEST)[0:n_sel, :]
        cur = jnp.right_shift(x * tq + tok, int(math.log2(SEL_BLOCK)))
        forced = (blk == 0) | (blk == cur) | (blk == cur - 1)
        val = jnp.where(forced, BIG, jnp.where(blk > cur, -BIG, imp))
        rank = jnp.zeros((n_sel, tq), F32)
        for other in range(min(n_sel, (x * tq + tq - 1) // SEL_BLOCK + 1)):
            r = val[other:other + 1, :]
            ahead = (r > val) | ((r == val) & (blk > other))
            rank = rank + jnp.where(ahead, 1.0, 0.0)
        block_mask_t = jnp.where(rank < float(min(SEL_TOPK, n_sel)), 0.0, NEG)
        block_mask = jnp.concatenate(
            [block_mask_t, jnp.zeros((LANES - n_sel, tq), F32)], axis=0).T.astype(BF16)

        return dict(qx=qx, block_mask=block_mask, o_c=o_c)

    order = _tile_order(seq // tq)
    started = []

    def setup(unit):
        x, pr = unit
        if not started:
            started.append(True)
            for y in order:
                tiles[y] = tile_setup(y)
        t = tiles[x] if pr + 1 < n_pairs else tiles.pop(x)
        q_pair = jnp.concatenate(t["qx"][2 * pr:2 * pr + 2], axis=0)
        q_sel = jnp.concatenate([q_pair, jnp.concatenate([t["block_mask"]] * 2, axis=0)], axis=1)
        return dict(q_win=q_pair, q_sel=q_sel, o_c=t["o_c"][2 * pr:2 * pr + 2], win={}, sel={})

    def steps_of(unit):
        x = unit[0]
        sel_steps = range((x * tq + tq - 1) // TK + 1)
        win_steps = range(max(0, (x * tq - (WINDOW - 1)) // TK), sel_steps[-1] + 1)
        return win_steps, sel_steps

    both = lambda t: jnp.concatenate([t, t], axis=0)

    def score_steps(unit, st):
        x, pr = unit
        win_steps, sel_steps = steps_of(unit)
        row_blocks = [(j, x * n_rb + r) for j in (2 * pr, 2 * pr + 1) for r in range(n_rb)]
        return _chain(
            _score_steps(st["q_win"],
                         [(lambda kp=kp: both(kwt_ref[0, :, keys(kp)])) for kp in win_steps],
                         _tile_kinds(row_blocks, win_steps, window=True), dn_ref, st["win"]),
            _score_steps(st["q_sel"],
                         [(lambda kp=kp: jnp.concatenate(
                             [both(kst_ref[0, :, keys(kp)]), et_ref[:, keys(kp)]], axis=0))
                          for kp in sel_steps],
                         _tile_kinds(row_blocks, sel_steps), dn_ref, st["sel"]))

    def weight_steps(unit, st):
        win_steps, sel_steps = steps_of(unit)
        return _chain(
            _weight_steps([(lambda kp=kp: _with_ones(vw_ref[0, keys(kp), :])) for kp in win_steps],
                          st["win"]),
            _weight_steps([(lambda kp=kp: _with_ones(vs_ref[0, keys(kp), :])) for kp in sel_steps],
                          st["sel"]))

    def finish(unit, st):
        x, pr = unit
        rows = slice(x * tq, (x + 1) * tq)
        cols = slice(pr * LANES, (pr + 1) * LANES)
        lo_half = lax.broadcasted_iota(jnp.int32, (tq, LANES), 1) < HEAD_DIM
        gates = _sigmoid(jnp.dot(gate_ref[0, rows, :], gsel_ref[0, :, 3 * pr * LANES:3 * (pr + 1) * LANES],
                                 preferred_element_type=F32))
        merged = lambda a: jnp.where(lo_half, a[:tq, :], a[tq:, :])
        ratio = lambda acc: merged(acc[:, :LANES] / acc[:, LANES:])
        o = (gates[:, 0:LANES] * jnp.where(lo_half, st["o_c"][0], st["o_c"][1])
             + gates[:, LANES:2 * LANES] * ratio(st["sel"]["acc"])
             + gates[:, 2 * LANES:3 * LANES] * ratio(st["win"]["acc"]))
        z = z_ref[0, rows, cols].astype(F32)
        o_ref[0, rows, cols] = (o * (z * _sigmoid(z))).astype(o_ref.dtype)

    units = [(x, pr) for x in order for pr in range(n_pairs)]
    _pipeline_tiles(units, lambda u: sum(len(s) for s in steps_of(u)),
                    setup, score_steps, weight_steps, finish)


def _nsa_attention(p3, kt3, kvc, dn, bc, ovt, et, gsel, col, tq=2 * TQ):
    b, s, _ = p3.shape
    wide = 2 * LANES

    def slab(first_lane_block, per_group, width=LANES):
        return pl.BlockSpec((1, s, width), lambda g, bi: (bi, 0, first_lane_block + per_group * g))

    in_specs = [
        slab(col["q"] // wide, 1, wide),
        pl.BlockSpec((1, HEAD_DIM, s), lambda g, bi: (bi, g, 0)),
        pl.BlockSpec((1, HEAD_DIM, s), lambda g, bi: (bi, NSA_GROUPS + g, 0)),
        slab(col["vsel"] // LANES, 1), slab(col["vwin"] // LANES, 1),
        pl.BlockSpec((1, 1, kvc.shape[2], wide), lambda g, bi: (bi, g, 0, 0)),
        slab(col["gate"] // LANES, 0),
        slab(col["z"] // wide, 1, wide),
        _resident((NSA_HPG, N_BIAS_KINDS, TQ, LANES), lambda g, bi: (g, 0, 0, 0)),
        _resident((NSA_HPG, s, LANES), lambda g, bi: (g, 0, 0)),
        _resident(ovt.shape, lambda g, bi: (0, 0)),
        _resident(et.shape, lambda g, bi: (0, 0)),
        _resident((1,) + gsel.shape[1:], lambda g, bi: (g, 0, 0)),
    ]
    return pl.pallas_call(
        functools.partial(_nsa_kernel, tq=tq),
        grid=(NSA_GROUPS, b),
        in_specs=in_specs,
        out_specs=pl.BlockSpec((1, s, wide), lambda g, bi: (bi, 0, g)),
        out_shape=jax.ShapeDtypeStruct((b, s, NSA_GROUPS * wide), BF16),
        compiler_params=pltpu.CompilerParams(
            dimension_semantics=("arbitrary", "arbitrary"), vmem_limit_bytes=VMEM_LIMIT),
        name="nsa_attention",
    )(p3, kt3, kt3, p3, p3, kvc, p3, p3, dn, bc, ovt, et, gsel)


def _diff_kernel(q_ref, kt_ref, v_ref, z_ref, dn_ref, lq1_ref, lk1_ref, lq2_ref, lk2_ref, sub_ref,
                 o_ref, *, lambda_init, tq):
    seq = q_ref.shape[1]
    n_rb = tq // TQ
    lane = lax.broadcasted_iota(jnp.int32, (tq, LANES), 1)
    lam = (jnp.exp(jnp.sum(lq1_ref[...] * lk1_ref[...], axis=-1, keepdims=True))
           - jnp.exp(jnp.sum(lq2_ref[...] * lk2_ref[...], axis=-1, keepdims=True)) + lambda_init)
    keys = lambda kp: slice(kp * TK, (kp + 1) * TK)
    steps_of = lambda x: range((x * tq + tq - 1) // TK + 1)

    def setup(x):
        q = q_ref[0, x * tq:(x + 1) * tq, :]
        zero = jnp.zeros_like(q)
        return dict(q=jnp.concatenate([jnp.where(lane < HEAD_DIM, q, zero),
                                       jnp.where(lane >= HEAD_DIM, q, zero)], axis=0))

    def score_steps(x, st):
        row_blocks = [(m, x * n_rb + r) for m in range(2) for r in range(n_rb)]
        return _score_steps(st["q"], [(lambda kp=kp: kt_ref[0, :, keys(kp)]) for kp in steps_of(x)],
                            _tile_kinds(row_blocks, steps_of(x)), dn_ref, st)

    def weight_steps(x, st):
        return _weight_steps([(lambda kp=kp: _with_ones(v_ref[0, keys(kp), :]))
                              for kp in steps_of(x)], st)

    def finish(x, st):
        rows = slice(x * tq, (x + 1) * tq)
        a = st["acc"][:, :LANES] / st["acc"][:, LANES:]
        o = a[:tq, :] - lam * a[tq:, :]
        o = o * lax.rsqrt(jnp.mean(o * o, axis=-1, keepdims=True) + EPS) * sub_ref[...]
        o = o * (1.0 - lambda_init)
        z = z_ref[0, rows, :].astype(F32)
        o_ref[0, rows, :] = (o * (z * _sigmoid(z))).astype(o_ref.dtype)

    _pipeline_tiles(_tile_order(seq // tq), lambda x: len(steps_of(x)),
                    setup, score_steps, weight_steps, finish)


def _diff_attention(p3, kt3, dn, lq1, lk1, lq2, lk2, subln, lambda_init, tq=2 * TQ):
    b, s, _ = p3.shape
    h = DIFF_HEADS
    vec = lambda a: a.reshape(1, -1)
    small = pl.BlockSpec((1, HEAD_DIM), lambda bi, hi: (0, 0))
    slab = lambda first: pl.BlockSpec((1, s, LANES), lambda bi, hi: (bi, 0, first + hi))
    in_specs = [
        slab(0),
        pl.BlockSpec((1, LANES, s), lambda bi, hi: (bi, hi, 0)),
        slab(h), slab(2 * h),
        pl.BlockSpec((2, N_BIAS_KINDS, TQ, LANES), lambda bi, hi: (hi, 0, 0, 0)),
        small, small, small, small,
        pl.BlockSpec((1, DIFF_VDIM), lambda bi, hi: (0, hi)),
    ]
    return pl.pallas_call(
        functools.partial(_diff_kernel, lambda_init=lambda_init, tq=tq),
        grid=(b, h),
        in_specs=in_specs,
        out_specs=slab(0),
        out_shape=jax.ShapeDtypeStruct((b, s, h * DIFF_VDIM), BF16),
        compiler_params=pltpu.CompilerParams(
            dimension_semantics=("arbitrary", "arbitrary"), vmem_limit_bytes=VMEM_LIMIT),
        name="diff_attention",
    )(p3, kt3, p3, p3, dn, vec(lq1), vec(lk1), vec(lq2), vec(lk2), vec(subln))


def _nsa_weight_layout(w_in):
    d = HEAD_DIM
    kv = NSA_GROUPS * d
    width = NSA_HEADS * d
    pts = np.cumsum([width] + [kv] * 6 + [3 * NSA_HEADS, width])
    q, kc, vc, ks, vs, kw, vw, gate, z = [w_in[:, a:b] for a, b in zip([0] + list(pts[:-1]), pts)]
    grp = lambda w, g: w[:, g * d:(g + 1) * d]
    dup = lambda w: jnp.concatenate(
        [jnp.concatenate([grp(w, g), grp(w, g)], axis=1) for g in range(NSA_GROUPS)], axis=1)
    gate_pad = jnp.pad(gate, ((0, 0), (0, LANES - gate.shape[1])))
    parts = [("q", q * (d ** -0.5 * LOG2E)), ("z", z), ("vsel", dup(vs)), ("vwin", dup(vw)),
             ("gate", gate_pad)]
    col, off = {}, 0
    for name, w in parts:
        col[name] = off
        off += w.shape[1]
    w_main = jnp.concatenate([w for _, w in parts], axis=1).astype(BF16)
    w_keys_t = jnp.concatenate([ks, kw], axis=1).astype(BF16).T
    w_cmp = jnp.concatenate(
        [jnp.concatenate([grp(kc, g), grp(vc, g)], axis=1) for g in range(NSA_GROUPS)],
        axis=1).astype(BF16)
    return w_main, w_cmp, w_keys_t, col


def _compress_weight_layout(pe_k, w1_k, w2_k, pe_v, w1_v, w2_v):
    d, hdn = HEAD_DIM, CMP_HIDDEN
    pe_kv = jnp.concatenate([pe_k, pe_v], axis=1)
    w1k = w1_k.reshape(CMP_BLOCK, d, hdn)
    w1v = w1_v.reshape(CMP_BLOCK, d, hdn)
    zero = jnp.zeros_like(w1k)
    w1_kv = jnp.concatenate([jnp.concatenate([w1k, zero], axis=2),
                             jnp.concatenate([zero, w1v], axis=2)], axis=1).astype(BF16)
    z2 = jnp.zeros((hdn, 2 * d), w2_k.dtype)
    w2_kv = jnp.concatenate([jnp.concatenate([w2_k, w2_k, z2], axis=1),
                             jnp.concatenate([z2, w2_v, w2_v], axis=1)], axis=0).astype(BF16)
    return pe_kv, w1_kv, w2_kv


def _selection_constants(seq):
    n_cmp_pad = seq // CMP_STRIDE
    n_cmp = (seq - CMP_BLOCK) // CMP_STRIDE + 1
    n_sel = seq // SEL_BLOCK
    cmp_lo = np.arange(n_cmp_pad) * CMP_STRIDE
    sel_lo = np.arange(n_sel) * SEL_BLOCK
    overlap = np.maximum(np.minimum(cmp_lo[:, None] + CMP_BLOCK, sel_lo[None, :] + SEL_BLOCK)
                         - np.maximum(cmp_lo[:, None], sel_lo[None, :]), 0).astype(np.float32) / CMP_BLOCK
    overlap[n_cmp:] = 0.0
    ovt = np.zeros((LANES, n_cmp_pad), np.float32)
    ovt[:n_sel] = overlap.T
    et = np.zeros((LANES, seq), np.float32)
    et[np.arange(seq) // SEL_BLOCK, np.arange(seq)] = 1.0
    n_pairs = NSA_HPG // 2
    gsel = np.zeros((NSA_GROUPS, LANES, 3 * n_pairs * LANES), np.float32)
    for g in range(NSA_GROUPS):
        for pr in range(n_pairs):
            for i in range(3):
                for n in range(LANES):
                    head = g * NSA_HPG + 2 * pr + n // HEAD_DIM
                    gsel[g, 3 * head + i, (pr * 3 + i) * LANES + n] = 1.0
    return jnp.asarray(ovt), jnp.asarray(et, dtype=BF16), jnp.asarray(gsel, dtype=BF16)


def kernel(x, rel_bias_table, norm_pre, norm_post, nsa_w_in, nsa_cmp_pe_k, nsa_cmp_w1_k, nsa_cmp_w2_k,
           nsa_cmp_pe_v, nsa_cmp_w1_v, nsa_cmp_w2_v, nsa_w_out, diff_w_in, diff_lambda_q1,
           diff_lambda_k1, diff_lambda_q2, diff_lambda_k2, diff_subln, diff_w_out):
    b, s, d = x.shape
    n = b * s
    assert d == D_MODEL and s % TK == 0 and s // SEL_BLOCK <= LANES and s // CMP_STRIDE == LANES
    dn, bc = _bias_tiles(rel_bias_table, s)
    ovt, et, gsel = _selection_constants(s)
    x2 = x.reshape(n, d)
    pending = None

    def project(i, weights, out_dtypes, weights_t):
        outs = _norm_proj(x2, norm_pre[i], s, weights, out_dtypes, weights_t, pending)
        return (outs[:-1], outs[-1]) if pending is not None else (outs, x2)

    for i in range(DEPTH):
        j = i // 2
        if i % 2 == 0:
            w_main, w_cmp, w_keys_t, col = _nsa_weight_layout(nsa_w_in[j])
            (p_main, p_cmp, keys_t), x2 = project(i, [w_main, w_cmp], [BF16, F32], [w_keys_t])
            kvc = _compress(p_cmp.reshape(b, s, -1),
                            *_compress_weight_layout(nsa_cmp_pe_k[j], nsa_cmp_w1_k[j], nsa_cmp_w2_k[j],
                                                     nsa_cmp_pe_v[j], nsa_cmp_w1_v[j], nsa_cmp_w2_v[j]))
            o = _nsa_attention(p_main.reshape(b, s, -1), keys_t, kvc, dn, bc, ovt, et, gsel, col)
            w_out = nsa_w_out[j]
        else:
            lambda_init = 0.8 - 0.6 * math.exp(-0.3 * i)
            w = diff_w_in[j]
            w_main = jnp.concatenate([w[:, :d] * (HEAD_DIM ** -0.5 * LOG2E), w[:, 2 * d:]],
                                     axis=1).astype(BF16)
            (p_main, keys_t), x2 = project(i, [w_main], [BF16], [w[:, d:2 * d].astype(BF16).T])
            o = _diff_attention(p_main.reshape(b, s, -1), keys_t, dn, diff_lambda_q1[j],
                                diff_lambda_k1[j], diff_lambda_q2[j], diff_lambda_k2[j],
                                diff_subln[j], lambda_init)
            w_out = diff_w_out[j]
        pending = (o.reshape(n, -1), w_out.astype(BF16), norm_post[i])
    return _out_proj(pending[0], pending[1], x2, pending[2]).reshape(b, s, d)
```

```python
import functools
import math

import jax
import jax.numpy as jnp
import numpy as np
from jax import lax
from jax.experimental import pallas as pl
from jax.experimental.pallas import tpu as pltpu

F32 = jnp.float32
BF16 = jnp.bfloat16

D_MODEL = 1024
DEPTH = 2
REL_BUCKETS = 32
REL_MAX_DIST = 128
NSA_HEADS = 16
NSA_GROUPS = 4
NSA_HPG = NSA_HEADS // NSA_GROUPS
HEAD_DIM = 64
CMP_BLOCK = 32
CMP_STRIDE = 16
CMP_HIDDEN = 128
SEL_BLOCK = 64
SEL_TOPK = 8
WINDOW = 512
DIFF_HEADS = 8
DIFF_VDIM = 128
NEG = -1e30
BIG = 1e9
EPS = 1e-6

LANES = 128
TQ = 128
TK = 2 * LANES
N_BIAS_KINDS = 5
VMEM_LIMIT = 52 * 1024 * 1024
LOG2E = math.log2(math.e)


def _sigmoid(x):
    return 1.0 / (1.0 + jnp.exp(-x))


def _nt_dot(a, b, precision=None):
    return lax.dot_general(a, b, (((1,), (1,)), ((), ())), precision=precision,
                           preferred_element_type=F32)


def _resident(shape, index_map):
    return pl.BlockSpec(shape, index_map, pipeline_mode=pl.Buffered(1))


def _residual_update(o_ref, w_ref, x_ref, g_ref):
    y = jnp.dot(o_ref[...], w_ref[...], preferred_element_type=F32)
    return x_ref[...] + y * lax.rsqrt(jnp.mean(y * y, axis=-1, keepdims=True) + EPS) * g_ref[...]


def _norm_proj_kernel(*refs, n_w, n_wt, fused):
    if fused:
        x = _residual_update(refs[0], refs[1], refs[2], refs[3])
        refs[-1][...] = x
        refs = refs[4:-1]
    else:
        x = refs[0][...]
        refs = refs[1:]
    g_ref, refs = refs[0], refs[1:]
    w_refs, wt_refs, o_refs = refs[:n_w], refs[n_w:n_w + n_wt], refs[n_w + n_wt:]
    u = x * lax.rsqrt(jnp.mean(x * x, axis=-1, keepdims=True) + EPS) * g_ref[...]
    ub = u.astype(BF16)
    for w_ref, o_ref in zip(w_refs, o_refs[:n_w]):
        o_ref[...] = jnp.dot(ub, w_ref[...], preferred_element_type=F32).astype(o_ref.dtype)
    for wt_ref, o_ref in zip(wt_refs, o_refs[n_w:]):
        o_ref[0] = _nt_dot(wt_ref[...], ub).astype(o_ref.dtype)


def _norm_proj(x2, g, seq, weights, out_dtypes, weights_t=(), pending=None, tm=512):
    n, d = x2.shape
    n_st = seq // tm
    row_tile = lambda c: pl.BlockSpec((tm, c), lambda i: (i, 0))
    vector = pl.BlockSpec((1, d), lambda i: (0, 0))
    args, in_specs = [x2, g.reshape(1, d)], [row_tile(d), vector]
    if pending is not None:
        o2, w_out, g_post = pending
        args = [o2, w_out, x2, g_post.reshape(1, d), g.reshape(1, d)]
        in_specs = [row_tile(o2.shape[1]), _resident(w_out.shape, lambda i: (0, 0)), row_tile(d),
                    vector, vector]
    out_specs, out_shape = [], []
    for w, dt in zip(weights, out_dtypes):
        c = w.shape[1]
        in_specs.append(_resident((d, c), lambda i: (0, 0)))
        out_specs.append(pl.BlockSpec((tm, c), lambda i: (i, 0)))
        out_shape.append(jax.ShapeDtypeStruct((n, c), dt))
    for wt in weights_t:
        c = wt.shape[0]
        in_specs.append(_resident((c, d), lambda i: (0, 0)))
        out_specs.append(pl.BlockSpec((1, c, tm), lambda i: (i // n_st, 0, i % n_st)))
        out_shape.append(jax.ShapeDtypeStruct((n // seq, c, seq), BF16))
    if pending is not None:
        out_specs.append(row_tile(d))
        out_shape.append(jax.ShapeDtypeStruct((n, d), F32))
    return pl.pallas_call(
        functools.partial(_norm_proj_kernel, n_w=len(weights), n_wt=len(weights_t),
                          fused=pending is not None),
        grid=(n // tm,),
        in_specs=in_specs,
        out_specs=out_specs,
        out_shape=out_shape,
        compiler_params=pltpu.CompilerParams(
            dimension_semantics=("arbitrary",), vmem_limit_bytes=VMEM_LIMIT),
        name="norm_proj",
    )(*args, *weights, *weights_t)


def _out_proj_kernel(o_ref, w_ref, x_ref, g_ref, y_ref):
    y_ref[...] = _residual_update(o_ref, w_ref, x_ref, g_ref)


def _out_proj(o2, w, x2, g, tm=1024):
    n, d = x2.shape
    return pl.pallas_call(
        _out_proj_kernel,
        grid=(n // tm,),
        in_specs=[pl.BlockSpec((tm, o2.shape[1]), lambda i: (i, 0)),
                  _resident(w.shape, lambda i: (0, 0)),
                  pl.BlockSpec((tm, d), lambda i: (i, 0)),
                  pl.BlockSpec((1, d), lambda i: (0, 0))],
        out_specs=pl.BlockSpec((tm, d), lambda i: (i, 0)),
        out_shape=jax.ShapeDtypeStruct((n, d), F32),
        compiler_params=pltpu.CompilerParams(
            dimension_semantics=("arbitrary",), vmem_limit_bytes=VMEM_LIMIT),
        name="out_proj",
    )(o2, w, x2, g.reshape(1, d))


def _bucket_starts(max_dist):
    max_exact = REL_BUCKETS // 2
    n = np.arange(max_dist)
    nf = np.maximum(n, 1).astype(np.float32)
    large = max_exact + (np.log(nf / max_exact) / np.float32(math.log(REL_MAX_DIST / max_exact))
                         * (REL_BUCKETS - max_exact)).astype(np.int32)
    bucket = np.where(n < max_exact, n, np.minimum(large, REL_BUCKETS - 1))
    assert (np.diff(bucket) >= 0).all()
    return [int(np.argmax(bucket >= b)) if (bucket >= b).any() else None
            for b in range(REL_BUCKETS)]


def _bias_kernel(tbl_ref, dn_ref, bc_ref):
    h = pl.program_id(0)
    c31 = tbl_ref[REL_BUCKETS - 1, h]
    starts = _bucket_starts(bc_ref.shape[1])

    def rel_bias(dist):
        n = jnp.maximum(dist, 0)
        out = jnp.full(dist.shape, tbl_ref[0, h], F32)
        for b in range(1, REL_BUCKETS):
            if starts[b] is not None:
                out = jnp.where(n >= starts[b], tbl_ref[b, h], out)
        return (out - c31) * LOG2E

    i = lax.broadcasted_iota(jnp.int32, (TQ, LANES), 0)
    j = lax.broadcasted_iota(jnp.int32, (TQ, LANES), 1)
    dn_ref[0, 0] = jnp.where(j <= i, rel_bias(i - j), NEG)
    dn_ref[0, 1] = rel_bias(LANES + i - j)
    dn_ref[0, 2] = jnp.zeros((TQ, LANES), F32)
    dn_ref[0, 3] = jnp.where(j > i, 0.0, NEG)
    dn_ref[0, 4] = jnp.full((TQ, LANES), NEG, F32)

    n_cmp = (bc_ref.shape[1] - CMP_BLOCK) // CMP_STRIDE + 1

    def body(r, carry):
        t = r * TQ + i
        d = t - (j * CMP_STRIDE + CMP_BLOCK - 1)
        valid = (d >= 0) & (j < n_cmp)
        bc_ref[0, pl.ds(pl.multiple_of(r * TQ, TQ), TQ), :] = jnp.where(valid, rel_bias(d), NEG)
        return carry

    lax.fori_loop(0, bc_ref.shape[1] // TQ, body, 0)


def _bias_tiles(table, seq):
    n_maps = table.shape[1]
    return pl.pallas_call(
        _bias_kernel,
        grid=(n_maps,),
        in_specs=[pl.BlockSpec(memory_space=pltpu.SMEM)],
        out_specs=[pl.BlockSpec((1, N_BIAS_KINDS, TQ, LANES), lambda h: (h, 0, 0, 0)),
                   pl.BlockSpec((1, seq, LANES), lambda h: (h, 0, 0))],
        out_shape=[jax.ShapeDtypeStruct((n_maps, N_BIAS_KINDS, TQ, LANES), F32),
                   jax.ShapeDtypeStruct((n_maps, seq, LANES), F32)],
        compiler_params=pltpu.CompilerParams(dimension_semantics=("arbitrary",)),
        name="bias_tiles",
    )(table)


def _compress_kernel(x_ref, pe_ref, w1_ref, w2_ref, o_ref, xp_ref):
    seq = x_ref.shape[1]
    n_chunks = seq // CMP_STRIDE
    xp_ref[0:seq, :] = x_ref[0]
    xp_ref[seq:, :] = jnp.zeros((xp_ref.shape[0] - seq, xp_ref.shape[1]), F32)
    hid = jnp.zeros((n_chunks, 2 * CMP_HIDDEN), F32)
    for l in range(CMP_BLOCK):
        a = xp_ref[pl.ds(l, n_chunks, stride=CMP_STRIDE), :]
        hid = hid + jnp.dot((a + pe_ref[l:l + 1, :]).astype(BF16), w1_ref[l],
                            preferred_element_type=F32)
    hid = hid * _sigmoid(hid)
    o_ref[0, 0] = jnp.dot(hid.astype(BF16), w2_ref[...], preferred_element_type=F32).astype(BF16)


def _compress(cmp_in, pe_kv, w1_kv, w2_kv):
    b, s, _ = cmp_in.shape
    n_chunks = s // CMP_STRIDE
    return pl.pallas_call(
        _compress_kernel,
        grid=(b, NSA_GROUPS),
        in_specs=[pl.BlockSpec((1, s, LANES), lambda bi, g: (bi, 0, g)),
                  pl.BlockSpec(pe_kv.shape, lambda bi, g: (0, 0)),
                  pl.BlockSpec(w1_kv.shape, lambda bi, g: (0, 0, 0)),
                  pl.BlockSpec(w2_kv.shape, lambda bi, g: (0, 0))],
        out_specs=pl.BlockSpec((1, 1, n_chunks, 2 * LANES), lambda bi, g: (bi, g, 0, 0)),
        out_shape=jax.ShapeDtypeStruct((b, NSA_GROUPS, n_chunks, 2 * LANES), BF16),
        scratch_shapes=[pltpu.VMEM((s + CMP_BLOCK, LANES), F32)],
        compiler_params=pltpu.CompilerParams(
            dimension_semantics=("arbitrary", "arbitrary"), vmem_limit_bytes=VMEM_LIMIT),
        name="nsa_compress",
    )(cmp_in, pe_kv, w1_kv, w2_kv)


def _tile_kind(delta, window=False):
    if delta < 0 or (window and delta > WINDOW // LANES):
        return 4
    if window and delta == WINDOW // LANES:
        return 3
    return min(delta, 2)


def _tile_kinds(row_blocks, steps, window=False):
    n_hf = TK // LANES
    return [[[(slot, _tile_kind(q_tile - (kp * n_hf + hf), window)) for hf in range(n_hf)]
             for slot, q_tile in row_blocks] for kp in steps]


def _score_steps(qx, kt_tiles, kinds, bias_ref, st):
    n_rb = qx.shape[0] // TQ
    n_hf = TK // LANES
    st["mx"] = [None] * n_rb
    st["sc"] = []
    for i, load_kt in enumerate(kt_tiles):
        s = jnp.dot(qx, load_kt(), preferred_element_type=F32)
        step = []
        for rb in range(n_rb):
            tiles = []
            for hf in range(n_hf):
                slot, kind = kinds[i][rb][hf]
                if kind == 4:
                    tiles.append(None)
                    continue
                t = s[rb * TQ:(rb + 1) * TQ, hf * LANES:(hf + 1) * LANES]
                if kind != 2:
                    t = t + bias_ref[slot, kind]
                tiles.append(t)
                st["mx"][rb] = t if st["mx"][rb] is None else jnp.maximum(st["mx"][rb], t)
            step.append(tiles)
        st["sc"].append(step)
        yield


def _weight_steps(v_tiles, st):
    n_rb = len(st["mx"])
    mb = [jnp.broadcast_to(jnp.max(v, axis=-1, keepdims=True), (TQ, LANES)) for v in st["mx"]]
    st["acc"] = jnp.zeros((n_rb * TQ, 2 * LANES), F32)
    for i, load_v in enumerate(v_tiles):
        p = jnp.concatenate(
            [jnp.concatenate([jnp.zeros((TQ, LANES), F32) if t is None else jnp.exp2(t - mb[rb])
                              for t in st["sc"][i][rb]], axis=1) for rb in range(n_rb)], axis=0)
        st["acc"] = st["acc"] + jnp.dot(p.astype(BF16), load_v(), preferred_element_type=F32)
        yield


def _chain(*gens):
    for g in gens:
        yield from g


def _pipeline_tiles(order, n_steps, setup, score_steps, weight_steps, finish):
    n = len(order)
    state = {}
    qk_at = [0, 0]

    def qk_stream():
        state[order[0]] = setup(order[0])
        for i, x in enumerate(order):
            if i + 1 < n:
                state[order[i + 1]] = setup(order[i + 1])
            qk_at[:] = [i, 0]
            for _ in score_steps(x, state[x]):
                qk_at[1] += 1
                yield
        qk_at[:] = [n, 0]

    def pv_stream():
        for i, x in enumerate(order):
            while not (qk_at[0] > i + 1 or qk_at[0] >= n
                       or (qk_at[0] == i + 1 and 2 * qk_at[1] >= n_steps(order[i + 1]))):
                yield
            yield from weight_steps(x, state[x])
            finish(x, state.pop(x))

    live = [qk_stream(), pv_stream()]
    while live:
        for g in list(live):
            try:
                next(g)
            except StopIteration:
                live.remove(g)


def _tile_order(n):
    return list(range(0, n, 2)) + list(range(n - 1 - n % 2, 0, -2))


def _with_ones(v_tile):
    return jnp.concatenate([v_tile, jnp.ones(v_tile.shape, v_tile.dtype)], axis=1)


def _nsa_kernel(q_ref, kst_ref, kwt_ref, vs_ref, vw_ref, kvc_ref, gate_ref, z_ref, dn_ref, bc_ref,
                ovt_ref, et_ref, gsel_ref, o_ref, *, tq):
    seq = q_ref.shape[1]
    n_rb = tq // TQ
    n_sel = seq // SEL_BLOCK

    keys = lambda kp: slice(kp * TK, (kp + 1) * TK)

    n_pairs = NSA_HPG // 2
    tiles = {}

    def tile_setup(x):
        lane = lax.broadcasted_iota(jnp.int32, (tq, LANES), 1)
        lo_half = lane < HEAD_DIM
        blk = lax.broadcasted_iota(jnp.int32, (n_sel, tq), 0)
        tok = lax.broadcasted_iota(jnp.int32, (n_sel, tq), 1)
        row = lax.broadcasted_iota(jnp.int32, (tq, 1), 0)
        kcc = kvc_ref[0, 0, :, 0:LANES]
        vcc = kvc_ref[0, 0, :, LANES:2 * LANES]
        rows = slice(x * tq, (x + 1) * tq)
        q = q_ref[0, rows, :]
        qx = []
        for j in range(NSA_HPG):
            slab = q[:, (j // 2) * LANES:(j // 2 + 1) * LANES]
            keep = lo_half if j % 2 == 0 else jnp.logical_not(lo_half)
            qx.append(jnp.where(keep, slab, jnp.zeros_like(slab)))

        any_valid = (x * tq + row >= CMP_BLOCK - 1).astype(F32)
        p_sum = jnp.zeros((tq, LANES), F32)
        o_c = []
        for j in range(NSA_HPG):
            sj = _nt_dot(qx[j], kcc) + bc_ref[j, rows, :]
            pj = jnp.exp2(sj - jnp.max(sj, axis=-1, keepdims=True))
            pv = jnp.dot(pj.astype(BF16), _with_ones(vcc), preferred_element_type=F32)
            inv = any_valid / pv[:, LANES:]
            p_sum = p_sum + pj * inv
            o_c.append(pv[:, :LANES] * inv)

        imp = _nt_dot(ovt_ref[...], p_sum, precision=lax.Precision.HIGHEST)[0:n_sel, :]
        cur = jnp.right_shift(x * tq + tok, int(math.log2(SEL_BLOCK)))
        forced = (blk == 0) | (blk == cur) | (blk == cur - 1)
        val = jnp.where(forced, BIG, jnp.where(blk > cur, -BIG, imp))
        rank = jnp.zeros((n_sel, tq), F32)
        for other in range(min(n_sel, (x * tq + tq - 1) // SEL_BLOCK + 1)):
            r = val[other:other + 1, :]
            ahead = (r > val) | ((r == val) & (blk > other))
            rank = rank + jnp.where(ahead, 1.0, 0.0)
        block_mask_t = jnp.where(rank < float(min(SEL_TOPK, n_sel)), 0.0, NEG)
        block_mask = jnp.concatenate(
            [block_mask_t, jnp.zeros((LANES - n_sel, tq), F32)], axis=0).T.astype(BF16)

        return dict(qx=qx, block_mask=block_mask, o_c=o_c)

    order = _tile_order(seq // tq)
    started = []

    pairs = {}

    def setup(unit):
        x, pr = unit
        if not started:
            started.append(True)
            for y in order:
                tiles[y] = tile_setup(y)
        if unit not in pairs:
            t = tiles[x] if pr + 1 < n_pairs else tiles.pop(x)
            q_pair = jnp.concatenate(t["qx"][2 * pr:2 * pr + 2], axis=0)
            q_sel = jnp.concatenate([q_pair, jnp.concatenate([t["block_mask"]] * 2, axis=0)], axis=1)
            pairs[unit] = dict(q_win=q_pair, q_sel=q_sel, o_c=t["o_c"][2 * pr:2 * pr + 2])
        return {}

    def steps_of(unit):
        x = unit[0]
        sel_steps = range((x * tq + tq - 1) // TK + 1)
        win_steps = range(max(0, (x * tq - (WINDOW - 1)) // TK), sel_steps[-1] + 1)
        return win_steps, sel_steps

    both = lambda t: jnp.concatenate([t, t], axis=0)
    row_blocks = lambda unit: [(j, unit[0] * n_rb + r) for j in (2 * unit[1], 2 * unit[1] + 1)
                               for r in range(n_rb)]

    def win_scores(unit, st):
        steps = steps_of(unit)[0]
        return _score_steps(pairs[unit]["q_win"],
                            [(lambda kp=kp: both(kwt_ref[0, :, keys(kp)])) for kp in steps],
                            _tile_kinds(row_blocks(unit), steps, window=True), dn_ref, st)

    def sel_scores(unit, st):
        steps = steps_of(unit)[1]
        return _score_steps(pairs[unit]["q_sel"],
                            [(lambda kp=kp: jnp.concatenate(
                                [both(kst_ref[0, :, keys(kp)]), et_ref[:, keys(kp)]], axis=0))
                             for kp in steps],
                            _tile_kinds(row_blocks(unit), steps), dn_ref, st)

    def win_weights(unit, st):
        return _weight_steps([(lambda kp=kp: _with_ones(vw_ref[0, keys(kp), :]))
                              for kp in steps_of(unit)[0]], st)

    def sel_weights(unit, st):
        return _weight_steps([(lambda kp=kp: _with_ones(vs_ref[0, keys(kp), :]))
                              for kp in steps_of(unit)[1]], st)

    def keep_window(unit, st):
        pairs[unit]["win_acc"] = st["acc"]

    def finish(unit, st):
        x, pr = unit
        pair = pairs.pop(unit)
        rows = slice(x * tq, (x + 1) * tq)
        cols = slice(pr * LANES, (pr + 1) * LANES)
        lo_half = lax.broadcasted_iota(jnp.int32, (tq, LANES), 1) < HEAD_DIM
        gates = _sigmoid(jnp.dot(gate_ref[0, rows, :], gsel_ref[0, :, 3 * pr * LANES:3 * (pr + 1) * LANES],
                                 preferred_element_type=F32))
        merged = lambda a: jnp.where(lo_half, a[:tq, :], a[tq:, :])
        ratio = lambda acc: merged(acc[:, :LANES] / acc[:, LANES:])
        o = (gates[:, 0:LANES] * jnp.where(lo_half, pair["o_c"][0], pair["o_c"][1])
             + gates[:, LANES:2 * LANES] * ratio(st["acc"])
             + gates[:, 2 * LANES:3 * LANES] * ratio(pair["win_acc"]))
        z = z_ref[0, rows, cols].astype(F32)
        o_ref[0, rows, cols] = (o * (z * _sigmoid(z))).astype(o_ref.dtype)

    units = [(x, pr) for x in order for pr in range(n_pairs)]
    _pipeline_tiles(units, lambda u: len(steps_of(u)[0]), setup, win_scores, win_weights,
                    keep_window)
    _pipeline_tiles(units, lambda u: len(steps_of(u)[1]), setup, sel_scores, sel_weights, finish)


def _nsa_attention(p3, kt3, kvc, dn, bc, ovt, et, gsel, col, tq=2 * TQ):
    b, s, _ = p3.shape
    wide = 2 * LANES

    def slab(first_lane_block, per_group, width=LANES):
        return pl.BlockSpec((1, s, width), lambda g, bi: (bi, 0, first_lane_block + per_group * g))

    in_specs = [
        slab(col["q"] // wide, 1, wide),
        pl.BlockSpec((1, HEAD_DIM, s), lambda g, bi: (bi, g, 0)),
        pl.BlockSpec((1, HEAD_DIM, s), lambda g, bi: (bi, NSA_GROUPS + g, 0)),
        slab(col["vsel"] // LANES, 1), slab(col["vwin"] // LANES, 1),
        pl.BlockSpec((1, 1, kvc.shape[2], wide), lambda g, bi: (bi, g, 0, 0)),
        slab(col["gate"] // LANES, 0),
        slab(col["z"] // wide, 1, wide),
        _resident((NSA_HPG, N_BIAS_KINDS, TQ, LANES), lambda g, bi: (g, 0, 0, 0)),
        _resident((NSA_HPG, s, LANES), lambda g, bi: (g, 0, 0)),
        _resident(ovt.shape, lambda g, bi: (0, 0)),
        _resident(et.shape, lambda g, bi: (0, 0)),
        _resident((1,) + gsel.shape[1:], lambda g, bi: (g, 0, 0)),
    ]
    return pl.pallas_call(
        functools.partial(_nsa_kernel, tq=tq),
        grid=(NSA_GROUPS, b),
        in_specs=in_specs,
        out_specs=pl.BlockSpec((1, s, wide), lambda g, bi: (bi, 0, g)),
        out_shape=jax.ShapeDtypeStruct((b, s, NSA_GROUPS * wide), BF16),
        compiler_params=pltpu.CompilerParams(
            dimension_semantics=("arbitrary", "arbitrary"), vmem_limit_bytes=VMEM_LIMIT),
        name="nsa_attention",
    )(p3, kt3, kt3, p3, p3, kvc, p3, p3, dn, bc, ovt, et, gsel)


def _diff_kernel(q_ref, kt_ref, v_ref, z_ref, dn_ref, lq1_ref, lk1_ref, lq2_ref, lk2_ref, sub_ref,
                 o_ref, *, lambda_init, tq):
    seq = q_ref.shape[1]
    n_rb = tq // TQ
    lane = lax.broadcasted_iota(jnp.int32, (tq, LANES), 1)
    lam = (jnp.exp(jnp.sum(lq1_ref[...] * lk1_ref[...], axis=-1, keepdims=True))
           - jnp.exp(jnp.sum(lq2_ref[...] * lk2_ref[...], axis=-1, keepdims=True)) + lambda_init)
    keys = lambda kp: slice(kp * TK, (kp + 1) * TK)
    steps_of = lambda x: range((x * tq + tq - 1) // TK + 1)

    def setup(x):
        q = q_ref[0, x * tq:(x + 1) * tq, :]
        zero = jnp.zeros_like(q)
        return dict(q=jnp.concatenate([jnp.where(lane < HEAD_DIM, q, zero),
                                       jnp.where(lane >= HEAD_DIM, q, zero)], axis=0))

    def score_steps(x, st):
        row_blocks = [(m, x * n_rb + r) for m in range(2) for r in range(n_rb)]
        return _score_steps(st["q"], [(lambda kp=kp: kt_ref[0, :, keys(kp)]) for kp in steps_of(x)],
                            _tile_kinds(row_blocks, steps_of(x)), dn_ref, st)

    def weight_steps(x, st):
        return _weight_steps([(lambda kp=kp: _with_ones(v_ref[0, keys(kp), :]))
                              for kp in steps_of(x)], st)

    def finish(x, st):
        rows = slice(x * tq, (x + 1) * tq)
        a = st["acc"][:, :LANES] / st["acc"][:, LANES:]
        o = a[:tq, :] - lam * a[tq:, :]
        o = o * lax.rsqrt(jnp.mean(o * o, axis=-1, keepdims=True) + EPS) * sub_ref[...]
        o = o * (1.0 - lambda_init)
        z = z_ref[0, rows, :].astype(F32)
        o_ref[0, rows, :] = (o * (z * _sigmoid(z))).astype(o_ref.dtype)

    _pipeline_tiles(_tile_order(seq // tq), lambda x: len(steps_of(x)),
                    setup, score_steps, weight_steps, finish)


def _diff_attention(p3, kt3, dn, lq1, lk1, lq2, lk2, subln, lambda_init, tq=2 * TQ):
    b, s, _ = p3.shape
    h = DIFF_HEADS
    vec = lambda a: a.reshape(1, -1)
    small = pl.BlockSpec((1, HEAD_DIM), lambda bi, hi: (0, 0))
    slab = lambda first: pl.BlockSpec((1, s, LANES), lambda bi, hi: (bi, 0, first + hi))
    in_specs = [
        slab(0),
        pl.BlockSpec((1, LANES, s), lambda bi, hi: (bi, hi, 0)),
        slab(h), slab(2 * h),
        pl.BlockSpec((2, N_BIAS_KINDS, TQ, LANES), lambda bi, hi: (hi, 0, 0, 0)),
        small, small, small, small,
        pl.BlockSpec((1, DIFF_VDIM), lambda bi, hi: (0, hi)),
    ]
    return pl.pallas_call(
        functools.partial(_diff_kernel, lambda_init=lambda_init, tq=tq),
        grid=(b, h),
        in_specs=in_specs,
        out_specs=slab(0),
        out_shape=jax.ShapeDtypeStruct((b, s, h * DIFF_VDIM), BF16),
        compiler_params=pltpu.CompilerParams(
            dimension_semantics=("arbitrary", "arbitrary"), vmem_limit_bytes=VMEM_LIMIT),
        name="diff_attention",
    )(p3, kt3, p3, p3, dn, vec(lq1), vec(lk1), vec(lq2), vec(lk2), vec(subln))


def _nsa_weight_layout(w_in):
    d = HEAD_DIM
    kv = NSA_GROUPS * d
    width = NSA_HEADS * d
    pts = np.cumsum([width] + [kv] * 6 + [3 * NSA_HEADS, width])
    q, kc, vc, ks, vs, kw, vw, gate, z = [w_in[:, a:b] for a, b in zip([0] + list(pts[:-1]), pts)]
    grp = lambda w, g: w[:, g * d:(g + 1) * d]
    dup = lambda w: jnp.concatenate(
        [jnp.concatenate([grp(w, g), grp(w, g)], axis=1) for g in range(NSA_GROUPS)], axis=1)
    gate_pad = jnp.pad(gate, ((0, 0), (0, LANES - gate.shape[1])))
    parts = [("q", q * (d ** -0.5 * LOG2E)), ("z", z), ("vsel", dup(vs)), ("vwin", dup(vw)),
             ("gate", gate_pad)]
    col, off = {}, 0
    for name, w in parts:
        col[name] = off
        off += w.shape[1]
    w_main = jnp.concatenate([w for _, w in parts], axis=1).astype(BF16)
    w_keys_t = jnp.concatenate([ks, kw], axis=1).astype(BF16).T
    w_cmp = jnp.concatenate(
        [jnp.concatenate([grp(kc, g), grp(vc, g)], axis=1) for g in range(NSA_GROUPS)],
        axis=1).astype(BF16)
    return w_main, w_cmp, w_keys_t, col


def _compress_weight_layout(pe_k, w1_k, w2_k, pe_v, w1_v, w2_v):
    d, hdn = HEAD_DIM, CMP_HIDDEN
    pe_kv = jnp.concatenate([pe_k, pe_v], axis=1)
    w1k = w1_k.reshape(CMP_BLOCK, d, hdn)
    w1v = w1_v.reshape(CMP_BLOCK, d, hdn)
    zero = jnp.zeros_like(w1k)
    w1_kv = jnp.concatenate([jnp.concatenate([w1k, zero], axis=2),
                             jnp.concatenate([zero, w1v], axis=2)], axis=1).astype(BF16)
    z2 = jnp.zeros((hdn, 2 * d), w2_k.dtype)
    w2_kv = jnp.concatenate([jnp.concatenate([w2_k, w2_k, z2], axis=1),
                             jnp.concatenate([z2, w2_v, w2_v], axis=1)], axis=0).astype(BF16)
    return pe_kv, w1_kv, w2_kv


def _selection_constants(seq):
    n_cmp_pad = seq // CMP_STRIDE
    n_cmp = (seq - CMP_BLOCK) // CMP_STRIDE + 1
    n_sel = seq // SEL_BLOCK
    cmp_lo = np.arange(n_cmp_pad) * CMP_STRIDE
    sel_lo = np.arange(n_sel) * SEL_BLOCK
    overlap = np.maximum(np.minimum(cmp_lo[:, None] + CMP_BLOCK, sel_lo[None, :] + SEL_BLOCK)
                         - np.maximum(cmp_lo[:, None], sel_lo[None, :]), 0).astype(np.float32) / CMP_BLOCK
    overlap[n_cmp:] = 0.0
    ovt = np.zeros((LANES, n_cmp_pad), np.float32)
    ovt[:n_sel] = overlap.T
    et = np.zeros((LANES, seq), np.float32)
    et[np.arange(seq) // SEL_BLOCK, np.arange(seq)] = 1.0
    n_pairs = NSA_HPG // 2
    gsel = np.zeros((NSA_GROUPS, LANES, 3 * n_pairs * LANES), np.float32)
    for g in range(NSA_GROUPS):
        for pr in range(n_pairs):
            for i in range(3):
                for n in range(LANES):
                    head = g * NSA_HPG + 2 * pr + n // HEAD_DIM
                    gsel[g, 3 * head + i, (pr * 3 + i) * LANES + n] = 1.0
    return jnp.asarray(ovt), jnp.asarray(et, dtype=BF16), jnp.asarray(gsel, dtype=BF16)


def kernel(x, rel_bias_table, norm_pre, norm_post, nsa_w_in, nsa_cmp_pe_k, nsa_cmp_w1_k, nsa_cmp_w2_k,
           nsa_cmp_pe_v, nsa_cmp_w1_v, nsa_cmp_w2_v, nsa_w_out, diff_w_in, diff_lambda_q1,
           diff_lambda_k1, diff_lambda_q2, diff_lambda_k2, diff_subln, diff_w_out):
    b, s, d = x.shape
    n = b * s
    assert d == D_MODEL and s % TK == 0 and s // SEL_BLOCK <= LANES and s // CMP_STRIDE == LANES
    dn, bc = _bias_tiles(rel_bias_table, s)
    ovt, et, gsel = _selection_constants(s)
    x2 = x.reshape(n, d)
    pending = None

    def project(i, weights, out_dtypes, weights_t):
        outs = _norm_proj(x2, norm_pre[i], s, weights, out_dtypes, weights_t, pending)
        return (outs[:-1], outs[-1]) if pending is not None else (outs, x2)

    for i in range(DEPTH):
        j = i // 2
        if i % 2 == 0:
            w_main, w_cmp, w_keys_t, col = _nsa_weight_layout(nsa_w_in[j])
            (p_main, p_cmp, keys_t), x2 = project(i, [w_main, w_cmp], [BF16, F32], [w_keys_t])
            kvc = _compress(p_cmp.reshape(b, s, -1),
                            *_compress_weight_layout(nsa_cmp_pe_k[j], nsa_cmp_w1_k[j], nsa_cmp_w2_k[j],
                                                     nsa_cmp_pe_v[j], nsa_cmp_w1_v[j], nsa_cmp_w2_v[j]))
            o = _nsa_attention(p_main.reshape(b, s, -1), keys_t, kvc, dn, bc, ovt, et, gsel, col)
            w_out = nsa_w_out[j]
        else:
            lambda_init = 0.8 - 0.6 * math.exp(-0.3 * i)
            w = diff_w_in[j]
            w_main = jnp.concatenate([w[:, :d] * (HEAD_DIM ** -0.5 * LOG2E), w[:, 2 * d:]],
                                     axis=1).astype(BF16)
            (p_main, keys_t), x2 = project(i, [w_main], [BF16], [w[:, d:2 * d].astype(BF16).T])
            o = _diff_attention(p_main.reshape(b, s, -1), keys_t, dn, diff_lambda_q1[j],
                                diff_lambda_k1[j], diff_lambda_q2[j], diff_lambda_k2[j],
                                diff_subln[j], lambda_init)
            w_out = diff_w_out[j]
        pending = (o.reshape(n, -1), w_out.astype(BF16), norm_post[i])
    return _out_proj(pending[0], pending[1], x2, pending[2]).reshape(b, s, d)
```

```python
import functools
import math

import jax
import jax.numpy as jnp
import numpy as np
from jax import lax
from jax.experimental import pallas as pl
from jax.experimental.pallas import tpu as pltpu

F32 = jnp.float32
BF16 = jnp.bfloat16

D_MODEL = 1024
DEPTH = 2
REL_BUCKETS = 32
REL_MAX_DIST = 128
NSA_HEADS = 16
NSA_GROUPS = 4
NSA_HPG = NSA_HEADS // NSA_GROUPS
HEAD_DIM = 64
CMP_BLOCK = 32
CMP_STRIDE = 16
CMP_HIDDEN = 128
SEL_BLOCK = 64
SEL_TOPK = 8
WINDOW = 512
DIFF_HEADS = 8
DIFF_VDIM = 128
NEG = -1e30
BIG = 1e9
EPS = 1e-6

LANES = 128
TQ = 128
TK = 2 * LANES
N_BIAS_KINDS = 5
VMEM_LIMIT = 52 * 1024 * 1024
LOG2E = math.log2(math.e)


def _sigmoid(x):
    return 1.0 / (1.0 + jnp.exp(-x))


def _nt_dot(a, b, precision=None):
    return lax.dot_general(a, b, (((1,), (1,)), ((), ())), precision=precision,
                           preferred_element_type=F32)


def _resident(shape, index_map):
    return pl.BlockSpec(shape, index_map, pipeline_mode=pl.Buffered(1))


def _residual_update(o_ref, w_ref, x_ref, g_ref):
    y = jnp.dot(o_ref[...], w_ref[...], preferred_element_type=F32)
    return x_ref[...] + y * lax.rsqrt(jnp.mean(y * y, axis=-1, keepdims=True) + EPS) * g_ref[...]


def _norm_proj_kernel(*refs, n_w, n_wt, fused):
    refs, scr_refs = refs[:len(refs) - n_wt], refs[len(refs) - n_wt:]
    if fused:
        x = _residual_update(refs[0], refs[1], refs[2], refs[3])
        refs[-1][...] = x
        refs = refs[4:-1]
    else:
        x = refs[0][...]
        refs = refs[1:]
    g_ref, refs = refs[0], refs[1:]
    w_refs, wt_refs, o_refs = refs[:n_w], refs[n_w:n_w + n_wt], refs[n_w + n_wt:]

    @pl.when(pl.program_id(0) == 0)
    def _():
        for wt_ref, scr_ref in zip(wt_refs, scr_refs):
            scr_ref[...] = wt_ref[...].T.astype(BF16)

    u = x * lax.rsqrt(jnp.mean(x * x, axis=-1, keepdims=True) + EPS) * g_ref[...]
    ub = u.astype(BF16)
    for w_ref, o_ref in zip(w_refs, o_refs[:n_w]):
        o_ref[...] = jnp.dot(ub, w_ref[...], preferred_element_type=F32).astype(o_ref.dtype)
    for scr_ref, o_ref in zip(scr_refs, o_refs[n_w:]):
        o_ref[0] = _nt_dot(scr_ref[...], ub).astype(o_ref.dtype)


def _norm_proj(x2, g, seq, weights, out_dtypes, weights_t=(), pending=None, tm=512):
    n, d = x2.shape
    n_st = seq // tm
    row_tile = lambda c: pl.BlockSpec((tm, c), lambda i: (i, 0))
    vector = pl.BlockSpec((1, d), lambda i: (0, 0))
    args, in_specs = [x2, g.reshape(1, d)], [row_tile(d), vector]
    if pending is not None:
        o2, w_out, g_post = pending
        args = [o2, w_out, x2, g_post.reshape(1, d), g.reshape(1, d)]
        in_specs = [row_tile(o2.shape[1]), _resident(w_out.shape, lambda i: (0, 0)), row_tile(d),
                    vector, vector]
    out_specs, out_shape = [], []
    for w, dt in zip(weights, out_dtypes):
        c = w.shape[1]
        in_specs.append(_resident((d, c), lambda i: (0, 0)))
        out_specs.append(pl.BlockSpec((tm, c), lambda i: (i, 0)))
        out_shape.append(jax.ShapeDtypeStruct((n, c), dt))
    for wt in weights_t:
        c = wt.shape[1]
        in_specs.append(_resident((d, c), lambda i: (0, 0)))
        out_specs.append(pl.BlockSpec((1, c, tm), lambda i: (i // n_st, 0, i % n_st)))
        out_shape.append(jax.ShapeDtypeStruct((n // seq, c, seq), BF16))
    if pending is not None:
        out_specs.append(row_tile(d))
        out_shape.append(jax.ShapeDtypeStruct((n, d), F32))
    return pl.pallas_call(
        functools.partial(_norm_proj_kernel, n_w=len(weights), n_wt=len(weights_t),
                          fused=pending is not None),
        grid=(n // tm,),
        in_specs=in_specs,
        out_specs=out_specs,
        out_shape=out_shape,
        scratch_shapes=[pltpu.VMEM((wt.shape[1], d), BF16) for wt in weights_t],
        compiler_params=pltpu.CompilerParams(
            dimension_semantics=("arbitrary",), vmem_limit_bytes=VMEM_LIMIT),
        name="norm_proj",
    )(*args, *weights, *weights_t)


def _out_proj_kernel(o_ref, w_ref, x_ref, g_ref, y_ref):
    y_ref[...] = _residual_update(o_ref, w_ref, x_ref, g_ref)


def _out_proj(o2, w, x2, g, tm=1024):
    n, d = x2.shape
    return pl.pallas_call(
        _out_proj_kernel,
        grid=(n // tm,),
        in_specs=[pl.BlockSpec((tm, o2.shape[1]), lambda i: (i, 0)),
                  _resident(w.shape, lambda i: (0, 0)),
                  pl.BlockSpec((tm, d), lambda i: (i, 0)),
                  pl.BlockSpec((1, d), lambda i: (0, 0))],
        out_specs=pl.BlockSpec((tm, d), lambda i: (i, 0)),
        out_shape=jax.ShapeDtypeStruct((n, d), F32),
        compiler_params=pltpu.CompilerParams(
            dimension_semantics=("arbitrary",), vmem_limit_bytes=VMEM_LIMIT),
        name="out_proj",
    )(o2, w, x2, g.reshape(1, d))


def _bucket_starts(max_dist):
    max_exact = REL_BUCKETS // 2
    n = np.arange(max_dist)
    nf = np.maximum(n, 1).astype(np.float32)
    large = max_exact + (np.log(nf / max_exact) / np.float32(math.log(REL_MAX_DIST / max_exact))
                         * (REL_BUCKETS - max_exact)).astype(np.int32)
    bucket = np.where(n < max_exact, n, np.minimum(large, REL_BUCKETS - 1))
    assert (np.diff(bucket) >= 0).all()
    return [int(np.argmax(bucket >= b)) if (bucket >= b).any() else None
            for b in range(REL_BUCKETS)]


def _bias_kernel(tbl_ref, dn_ref, bc_ref):
    h = pl.program_id(0)
    c31 = tbl_ref[REL_BUCKETS - 1, h]
    starts = _bucket_starts(bc_ref.shape[1])

    def rel_bias(dist):
        n = jnp.maximum(dist, 0)
        out = jnp.full(dist.shape, tbl_ref[0, h], F32)
        for b in range(1, REL_BUCKETS):
            if starts[b] is not None:
                out = jnp.where(n >= starts[b], tbl_ref[b, h], out)
        return (out - c31) * LOG2E

    i = lax.broadcasted_iota(jnp.int32, (TQ, LANES), 0)
    j = lax.broadcasted_iota(jnp.int32, (TQ, LANES), 1)
    dn_ref[0, 0] = jnp.where(j <= i, rel_bias(i - j), NEG)
    dn_ref[0, 1] = rel_bias(LANES + i - j)
    dn_ref[0, 2] = jnp.zeros((TQ, LANES), F32)
    dn_ref[0, 3] = jnp.where(j > i, 0.0, NEG)
    dn_ref[0, 4] = jnp.full((TQ, LANES), NEG, F32)

    n_cmp = (bc_ref.shape[1] - CMP_BLOCK) // CMP_STRIDE + 1

    def body(r, carry):
        t = r * TQ + i
        d = t - (j * CMP_STRIDE + CMP_BLOCK - 1)
        valid = (d >= 0) & (j < n_cmp)
        bc_ref[0, pl.ds(pl.multiple_of(r * TQ, TQ), TQ), :] = jnp.where(valid, rel_bias(d), NEG)
        return carry

    lax.fori_loop(0, bc_ref.shape[1] // TQ, body, 0)


def _bias_tiles(table, seq):
    n_maps = table.shape[1]
    return pl.pallas_call(
        _bias_kernel,
        grid=(n_maps,),
        in_specs=[pl.BlockSpec(memory_space=pltpu.SMEM)],
        out_specs=[pl.BlockSpec((1, N_BIAS_KINDS, TQ, LANES), lambda h: (h, 0, 0, 0)),
                   pl.BlockSpec((1, seq, LANES), lambda h: (h, 0, 0))],
        out_shape=[jax.ShapeDtypeStruct((n_maps, N_BIAS_KINDS, TQ, LANES), F32),
                   jax.ShapeDtypeStruct((n_maps, seq, LANES), F32)],
        compiler_params=pltpu.CompilerParams(dimension_semantics=("arbitrary",)),
        name="bias_tiles",
    )(table)


def _compress_kernel(x_ref, pe_ref, w1_ref, w2_ref, o_ref, xp_ref):
    seq = x_ref.shape[1]
    n_chunks = seq // CMP_STRIDE
    xp_ref[0:seq, :] = x_ref[0]
    xp_ref[seq:, :] = jnp.zeros((xp_ref.shape[0] - seq, xp_ref.shape[1]), F32)
    hid = jnp.zeros((n_chunks, 2 * CMP_HIDDEN), F32)
    for l in range(CMP_BLOCK):
        a = xp_ref[pl.ds(l, n_chunks, stride=CMP_STRIDE), :]
        hid = hid + jnp.dot((a + pe_ref[l:l + 1, :]).astype(BF16), w1_ref[l],
                            preferred_element_type=F32)
    hid = hid * _sigmoid(hid)
    o_ref[0, 0] = jnp.dot(hid.astype(BF16), w2_ref[...], preferred_element_type=F32).astype(BF16)


def _compress(cmp_in, pe_kv, w1_kv, w2_kv):
    b, s, _ = cmp_in.shape
    n_chunks = s // CMP_STRIDE
    return pl.pallas_call(
        _compress_kernel,
        grid=(b, NSA_GROUPS),
        in_specs=[pl.BlockSpec((1, s, LANES), lambda bi, g: (bi, 0, g)),
                  pl.BlockSpec(pe_kv.shape, lambda bi, g: (0, 0)),
                  pl.BlockSpec(w1_kv.shape, lambda bi, g: (0, 0, 0)),
                  pl.BlockSpec(w2_kv.shape, lambda bi, g: (0, 0))],
        out_specs=pl.BlockSpec((1, 1, n_chunks, 2 * LANES), lambda bi, g: (bi, g, 0, 0)),
        out_shape=jax.ShapeDtypeStruct((b, NSA_GROUPS, n_chunks, 2 * LANES), BF16),
        scratch_shapes=[pltpu.VMEM((s + CMP_BLOCK, LANES), F32)],
        compiler_params=pltpu.CompilerParams(
            dimension_semantics=("arbitrary", "arbitrary"), vmem_limit_bytes=VMEM_LIMIT),
        name="nsa_compress",
    )(cmp_in, pe_kv, w1_kv, w2_kv)


def _tile_kind(delta, window=False):
    if delta < 0 or (window and delta > WINDOW // LANES):
        return 4
    if window and delta == WINDOW // LANES:
        return 3
    return min(delta, 2)


def _tile_kinds(row_blocks, steps, window=False):
    n_hf = TK // LANES
    return [[[(slot, _tile_kind(q_tile - (kp * n_hf + hf), window)) for hf in range(n_hf)]
             for slot, q_tile in row_blocks] for kp in steps]


def _score_steps(qx, kt_tiles, kinds, bias_ref, st):
    n_rb = qx.shape[0] // TQ
    n_hf = TK // LANES
    st["mx"] = [None] * n_rb
    st["sc"] = []
    for i, load_kt in enumerate(kt_tiles):
        s = jnp.dot(qx, load_kt(), preferred_element_type=F32)
        step = []
        for rb in range(n_rb):
            tiles = []
            for hf in range(n_hf):
                slot, kind = kinds[i][rb][hf]
                if kind == 4:
                    tiles.append(None)
                    continue
                t = s[rb * TQ:(rb + 1) * TQ, hf * LANES:(hf + 1) * LANES]
                if kind != 2:
                    t = t + bias_ref[slot, kind]
                tiles.append(t)
                st["mx"][rb] = t if st["mx"][rb] is None else jnp.maximum(st["mx"][rb], t)
            step.append(tiles)
        st["sc"].append(step)
        yield


def _weight_steps(v_tiles, st):
    n_rb = len(st["mx"])
    mb = [jnp.broadcast_to(jnp.max(v, axis=-1, keepdims=True), (TQ, LANES)) for v in st["mx"]]
    st["acc"] = jnp.zeros((n_rb * TQ, 2 * LANES), F32)
    for i, load_v in enumerate(v_tiles):
        p = jnp.concatenate(
            [jnp.concatenate([jnp.zeros((TQ, LANES), F32) if t is None else jnp.exp2(t - mb[rb])
                              for t in st["sc"][i][rb]], axis=1) for rb in range(n_rb)], axis=0)
        st["acc"] = st["acc"] + jnp.dot(p.astype(BF16), load_v(), preferred_element_type=F32)
        yield


def _chain(*gens):
    for g in gens:
        yield from g


def _pipeline_tiles(order, n_steps, setup, score_steps, weight_steps, finish):
    n = len(order)
    state = {}
    qk_at = [0, 0]

    def qk_stream():
        state[order[0]] = setup(order[0])
        for i, x in enumerate(order):
            if i + 1 < n:
                state[order[i + 1]] = setup(order[i + 1])
            qk_at[:] = [i, 0]
            for _ in score_steps(x, state[x]):
                qk_at[1] += 1
                yield
        qk_at[:] = [n, 0]

    def pv_stream():
        for i, x in enumerate(order):
            while not (qk_at[0] > i + 1 or qk_at[0] >= n
                       or (qk_at[0] == i + 1 and 2 * qk_at[1] >= n_steps(order[i + 1]))):
                yield
            yield from weight_steps(x, state[x])
            finish(x, state.pop(x))

    live = [qk_stream(), pv_stream()]
    while live:
        for g in list(live):
            try:
                next(g)
            except StopIteration:
                live.remove(g)


def _tile_order(n):
    return list(range(0, n, 2)) + list(range(n - 1 - n % 2, 0, -2))


def _with_ones(v_tile):
    return jnp.concatenate([v_tile, jnp.ones(v_tile.shape, v_tile.dtype)], axis=1)


def _nsa_kernel(q_ref, kst_ref, kwt_ref, vs_ref, vw_ref, kvc_ref, gate_ref, z_ref, dn_ref, bc_ref,
                ovt_ref, et_ref, gsel_ref, o_ref, *, tq):
    seq = q_ref.shape[1]
    n_rb = tq // TQ
    n_sel = seq // SEL_BLOCK

    keys = lambda kp: slice(kp * TK, (kp + 1) * TK)

    n_pairs = NSA_HPG // 2
    tiles = {}

    def tile_setup(x):
        lane = lax.broadcasted_iota(jnp.int32, (tq, LANES), 1)
        lo_half = lane < HEAD_DIM
        blk = lax.broadcasted_iota(jnp.int32, (n_sel, tq), 0)
        tok = lax.broadcasted_iota(jnp.int32, (n_sel, tq), 1)
        row = lax.broadcasted_iota(jnp.int32, (tq, 1), 0)
        kcc = kvc_ref[0, 0, :, 0:LANES]
        vcc = kvc_ref[0, 0, :, LANES:2 * LANES]
        rows = slice(x * tq, (x + 1) * tq)
        q = q_ref[0, rows, :]
        qx = []
        for j in range(NSA_HPG):
            slab = q[:, (j // 2) * LANES:(j // 2 + 1) * LANES]
            keep = lo_half if j % 2 == 0 else jnp.logical_not(lo_half)
            qx.append(jnp.where(keep, slab, jnp.zeros_like(slab)))

        any_valid = (x * tq + row >= CMP_BLOCK - 1).astype(F32)
        p_sum = jnp.zeros((tq, LANES), F32)
        o_c = []
        for j in range(NSA_HPG):
            sj = _nt_dot(qx[j], kcc) + bc_ref[j, rows, :]
            pj = jnp.exp2(sj - jnp.max(sj, axis=-1, keepdims=True))
            pv = jnp.dot(pj.astype(BF16), _with_ones(vcc), preferred_element_type=F32)
            inv = any_valid / pv[:, LANES:]
            p_sum = p_sum + pj * inv
            o_c.append(pv[:, :LANES] * inv)

        imp = _nt_dot(ovt_ref[...], p_sum, precision=lax.Precision.HIGHEST)[0:n_sel, :]
        cur = jnp.right_shift(x * tq + tok, int(math.log2(SEL_BLOCK)))
        forced = (blk == 0) | (blk == cur) | (blk == cur - 1)
        val = jnp.where(forced, BIG, jnp.where(blk > cur, -BIG, imp))
        rank = jnp.zeros((n_sel, tq), F32)
        for other in range(min(n_sel, (x * tq + tq - 1) // SEL_BLOCK + 1)):
            r = val[other:other + 1, :]
            ahead = (r > val) | ((r == val) & (blk > other))
            rank = rank + jnp.where(ahead, 1.0, 0.0)
        block_mask_t = jnp.where(rank < float(min(SEL_TOPK, n_sel)), 0.0, NEG)
        block_mask = jnp.concatenate(
            [block_mask_t, jnp.zeros((LANES - n_sel, tq), F32)], axis=0).T.astype(BF16)

        return dict(qx=qx, block_mask=block_mask, o_c=o_c)

    order = _tile_order(seq // tq)
    started = []

    pairs = {}

    def setup(unit):
        x, pr = unit
        if not started:
            started.append(True)
            for y in order:
                tiles[y] = tile_setup(y)
        if unit not in pairs:
            t = tiles[x] if pr + 1 < n_pairs else tiles.pop(x)
            q_pair = jnp.concatenate(t["qx"][2 * pr:2 * pr + 2], axis=0)
            q_sel = jnp.concatenate([q_pair, jnp.concatenate([t["block_mask"]] * 2, axis=0)], axis=1)
            pairs[unit] = dict(q_win=q_pair, q_sel=q_sel, o_c=t["o_c"][2 * pr:2 * pr + 2])
        return {}

    def steps_of(unit):
        x = unit[0]
        sel_steps = range((x * tq + tq - 1) // TK + 1)
        win_steps = range(max(0, (x * tq - (WINDOW - 1)) // TK), sel_steps[-1] + 1)
        return win_steps, sel_steps

    both = lambda t: jnp.concatenate([t, t], axis=0)
    row_blocks = lambda unit: [(j, unit[0] * n_rb + r) for j in (2 * unit[1], 2 * unit[1] + 1)
                               for r in range(n_rb)]

    def win_scores(unit, st):
        steps = steps_of(unit)[0]
        return _score_steps(pairs[unit]["q_win"],
                            [(lambda kp=kp: both(kwt_ref[0, :, keys(kp)])) for kp in steps],
                            _tile_kinds(row_blocks(unit), steps, window=True), dn_ref, st)

    def sel_scores(unit, st):
        steps = steps_of(unit)[1]
        return _score_steps(pairs[unit]["q_sel"],
                            [(lambda kp=kp: jnp.concatenate(
                                [both(kst_ref[0, :, keys(kp)]), et_ref[:, keys(kp)]], axis=0))
                             for kp in steps],
                            _tile_kinds(row_blocks(unit), steps), dn_ref, st)

    def win_weights(unit, st):
        return _weight_steps([(lambda kp=kp: _with_ones(vw_ref[0, keys(kp), :]))
                              for kp in steps_of(unit)[0]], st)

    def sel_weights(unit, st):
        return _weight_steps([(lambda kp=kp: _with_ones(vs_ref[0, keys(kp), :]))
                              for kp in steps_of(unit)[1]], st)

    def keep_window(unit, st):
        pairs[unit]["win_acc"] = st["acc"]

    def finish(unit, st):
        x, pr = unit
        pair = pairs.pop(unit)
        rows = slice(x * tq, (x + 1) * tq)
        cols = slice(pr * LANES, (pr + 1) * LANES)
        lo_half = lax.broadcasted_iota(jnp.int32, (tq, LANES), 1) < HEAD_DIM
        gates = _sigmoid(jnp.dot(gate_ref[0, rows, :], gsel_ref[0, :, 3 * pr * LANES:3 * (pr + 1) * LANES],
                                 preferred_element_type=F32))
        merged = lambda a: jnp.where(lo_half, a[:tq, :], a[tq:, :])
        ratio = lambda acc: merged(acc[:, :LANES] / acc[:, LANES:])
        o = (gates[:, 0:LANES] * jnp.where(lo_half, pair["o_c"][0], pair["o_c"][1])
             + gates[:, LANES:2 * LANES] * ratio(st["acc"])
             + gates[:, 2 * LANES:3 * LANES] * ratio(pair["win_acc"]))
        z = z_ref[0, rows, cols].astype(F32)
        o_ref[0, rows, cols] = (o * (z * _sigmoid(z))).astype(o_ref.dtype)

    units = [(x, pr) for x in order for pr in range(n_pairs)]
    _pipeline_tiles(units, lambda u: len(steps_of(u)[0]), setup, win_scores, win_weights,
                    keep_window)
    _pipeline_tiles(units, lambda u: len(steps_of(u)[1]), setup, sel_scores, sel_weights, finish)


def _nsa_attention(p3, kt3, kvc, dn, bc, ovt, et, gsel, col, tq=2 * TQ):
    b, s, _ = p3.shape
    wide = 2 * LANES

    def slab(first_lane_block, per_group, width=LANES):
        return pl.BlockSpec((1, s, width), lambda g, bi: (bi, 0, first_lane_block + per_group * g))

    in_specs = [
        slab(col["q"] // wide, 1, wide),
        pl.BlockSpec((1, HEAD_DIM, s), lambda g, bi: (bi, g, 0)),
        pl.BlockSpec((1, HEAD_DIM, s), lambda g, bi: (bi, NSA_GROUPS + g, 0)),
        slab(col["vsel"] // LANES, 1), slab(col["vwin"] // LANES, 1),
        pl.BlockSpec((1, 1, kvc.shape[2], wide), lambda g, bi: (bi, g, 0, 0)),
        slab(col["gate"] // LANES, 0),
        slab(col["z"] // wide, 1, wide),
        _resident((NSA_HPG, N_BIAS_KINDS, TQ, LANES), lambda g, bi: (g, 0, 0, 0)),
        _resident((NSA_HPG, s, LANES), lambda g, bi: (g, 0, 0)),
        _resident(ovt.shape, lambda g, bi: (0, 0)),
        _resident(et.shape, lambda g, bi: (0, 0)),
        _resident((1,) + gsel.shape[1:], lambda g, bi: (g, 0, 0)),
    ]
    return pl.pallas_call(
        functools.partial(_nsa_kernel, tq=tq),
        grid=(NSA_GROUPS, b),
        in_specs=in_specs,
        out_specs=pl.BlockSpec((1, s, wide), lambda g, bi: (bi, 0, g)),
        out_shape=jax.ShapeDtypeStruct((b, s, NSA_GROUPS * wide), BF16),
        compiler_params=pltpu.CompilerParams(
            dimension_semantics=("arbitrary", "arbitrary"), vmem_limit_bytes=VMEM_LIMIT),
        name="nsa_attention",
    )(p3, kt3, kt3, p3, p3, kvc, p3, p3, dn, bc, ovt, et, gsel)


def _diff_kernel(q_ref, kt_ref, v_ref, z_ref, dn_ref, lq1_ref, lk1_ref, lq2_ref, lk2_ref, sub_ref,
                 o_ref, *, lambda_init, tq):
    seq = q_ref.shape[1]
    n_rb = tq // TQ
    lane = lax.broadcasted_iota(jnp.int32, (tq, LANES), 1)
    lam = (jnp.exp(jnp.sum(lq1_ref[...] * lk1_ref[...], axis=-1, keepdims=True))
           - jnp.exp(jnp.sum(lq2_ref[...] * lk2_ref[...], axis=-1, keepdims=True)) + lambda_init)
    keys = lambda kp: slice(kp * TK, (kp + 1) * TK)
    steps_of = lambda x: range((x * tq + tq - 1) // TK + 1)

    def setup(x):
        q = q_ref[0, x * tq:(x + 1) * tq, :]
        zero = jnp.zeros_like(q)
        return dict(q=jnp.concatenate([jnp.where(lane < HEAD_DIM, q, zero),
                                       jnp.where(lane >= HEAD_DIM, q, zero)], axis=0))

    def score_steps(x, st):
        row_blocks = [(m, x * n_rb + r) for m in range(2) for r in range(n_rb)]
        return _score_steps(st["q"], [(lambda kp=kp: kt_ref[0, :, keys(kp)]) for kp in steps_of(x)],
                            _tile_kinds(row_blocks, steps_of(x)), dn_ref, st)

    def weight_steps(x, st):
        return _weight_steps([(lambda kp=kp: _with_ones(v_ref[0, keys(kp), :]))
                              for kp in steps_of(x)], st)

    def finish(x, st):
        rows = slice(x * tq, (x + 1) * tq)
        a = st["acc"][:, :LANES] / st["acc"][:, LANES:]
        o = a[:tq, :] - lam * a[tq:, :]
        o = o * lax.rsqrt(jnp.mean(o * o, axis=-1, keepdims=True) + EPS) * sub_ref[...]
        o = o * (1.0 - lambda_init)
        z = z_ref[0, rows, :].astype(F32)
        o_ref[0, rows, :] = (o * (z * _sigmoid(z))).astype(o_ref.dtype)

    _pipeline_tiles(_tile_order(seq // tq), lambda x: len(steps_of(x)),
                    setup, score_steps, weight_steps, finish)


def _diff_attention(p3, kt3, dn, lq1, lk1, lq2, lk2, subln, lambda_init, tq=2 * TQ):
    b, s, _ = p3.shape
    h = DIFF_HEADS
    vec = lambda a: a.reshape(1, -1)
    small = pl.BlockSpec((1, HEAD_DIM), lambda bi, hi: (0, 0))
    slab = lambda first: pl.BlockSpec((1, s, LANES), lambda bi, hi: (bi, 0, first + hi))
    in_specs = [
        slab(0),
        pl.BlockSpec((1, LANES, s), lambda bi, hi: (bi, hi, 0)),
        slab(h), slab(2 * h),
        pl.BlockSpec((2, N_BIAS_KINDS, TQ, LANES), lambda bi, hi: (hi, 0, 0, 0)),
        small, small, small, small,
        pl.BlockSpec((1, DIFF_VDIM), lambda bi, hi: (0, hi)),
    ]
    return pl.pallas_call(
        functools.partial(_diff_kernel, lambda_init=lambda_init, tq=tq),
        grid=(b, h),
        in_specs=in_specs,
        out_specs=slab(0),
        out_shape=jax.ShapeDtypeStruct((b, s, h * DIFF_VDIM), BF16),
        compiler_params=pltpu.CompilerParams(
            dimension_semantics=("arbitrary", "arbitrary"), vmem_limit_bytes=VMEM_LIMIT),
        name="diff_attention",
    )(p3, kt3, p3, p3, dn, vec(lq1), vec(lk1), vec(lq2), vec(lk2), vec(subln))


def _nsa_weight_layout(w_in):
    d = HEAD_DIM
    kv = NSA_GROUPS * d
    width = NSA_HEADS * d
    pts = np.cumsum([width] + [kv] * 6 + [3 * NSA_HEADS, width])
    q, kc, vc, ks, vs, kw, vw, gate, z = [w_in[:, a:b] for a, b in zip([0] + list(pts[:-1]), pts)]
    grp = lambda w, g: w[:, g * d:(g + 1) * d]
    dup = lambda w: jnp.concatenate(
        [jnp.concatenate([grp(w, g), grp(w, g)], axis=1) for g in range(NSA_GROUPS)], axis=1)
    gate_pad = jnp.pad(gate, ((0, 0), (0, LANES - gate.shape[1])))
    parts = [("q", q * (d ** -0.5 * LOG2E)), ("z", z), ("vsel", dup(vs)), ("vwin", dup(vw)),
             ("gate", gate_pad)]
    col, off = {}, 0
    for name, w in parts:
        col[name] = off
        off += w.shape[1]
    w_main = jnp.concatenate([w for _, w in parts], axis=1).astype(BF16)
    w_keys_t = jnp.concatenate([ks, kw], axis=1)
    w_cmp = jnp.concatenate(
        [jnp.concatenate([grp(kc, g), grp(vc, g)], axis=1) for g in range(NSA_GROUPS)],
        axis=1).astype(BF16)
    return w_main, w_cmp, w_keys_t, col


def _compress_weight_layout(pe_k, w1_k, w2_k, pe_v, w1_v, w2_v):
    d, hdn = HEAD_DIM, CMP_HIDDEN
    pe_kv = jnp.concatenate([pe_k, pe_v], axis=1)
    w1k = w1_k.reshape(CMP_BLOCK, d, hdn)
    w1v = w1_v.reshape(CMP_BLOCK, d, hdn)
    zero = jnp.zeros_like(w1k)
    w1_kv = jnp.concatenate([jnp.concatenate([w1k, zero], axis=2),
                             jnp.concatenate([zero, w1v], axis=2)], axis=1).astype(BF16)
    z2 = jnp.zeros((hdn, 2 * d), w2_k.dtype)
    w2_kv = jnp.concatenate([jnp.concatenate([w2_k, w2_k, z2], axis=1),
                             jnp.concatenate([z2, w2_v, w2_v], axis=1)], axis=0).astype(BF16)
    return pe_kv, w1_kv, w2_kv


def _selection_constants(seq):
    n_cmp_pad = seq // CMP_STRIDE
    n_cmp = (seq - CMP_BLOCK) // CMP_STRIDE + 1
    n_sel = seq // SEL_BLOCK
    cmp_lo = np.arange(n_cmp_pad) * CMP_STRIDE
    sel_lo = np.arange(n_sel) * SEL_BLOCK
    overlap = np.maximum(np.minimum(cmp_lo[:, None] + CMP_BLOCK, sel_lo[None, :] + SEL_BLOCK)
                         - np.maximum(cmp_lo[:, None], sel_lo[None, :]), 0).astype(np.float32) / CMP_BLOCK
    overlap[n_cmp:] = 0.0
    ovt = np.zeros((LANES, n_cmp_pad), np.float32)
    ovt[:n_sel] = overlap.T
    et = np.zeros((LANES, seq), np.float32)
    et[np.arange(seq) // SEL_BLOCK, np.arange(seq)] = 1.0
    n_pairs = NSA_HPG // 2
    gsel = np.zeros((NSA_GROUPS, LANES, 3 * n_pairs * LANES), np.float32)
    for g in range(NSA_GROUPS):
        for pr in range(n_pairs):
            for i in range(3):
                for n in range(LANES):
                    head = g * NSA_HPG + 2 * pr + n // HEAD_DIM
                    gsel[g, 3 * head + i, (pr * 3 + i) * LANES + n] = 1.0
    return jnp.asarray(ovt), jnp.asarray(et, dtype=BF16), jnp.asarray(gsel, dtype=BF16)


def kernel(x, rel_bias_table, norm_pre, norm_post, nsa_w_in, nsa_cmp_pe_k, nsa_cmp_w1_k, nsa_cmp_w2_k,
           nsa_cmp_pe_v, nsa_cmp_w1_v, nsa_cmp_w2_v, nsa_w_out, diff_w_in, diff_lambda_q1,
           diff_lambda_k1, diff_lambda_q2, diff_lambda_k2, diff_subln, diff_w_out):
    b, s, d = x.shape
    n = b * s
    assert d == D_MODEL and s % TK == 0 and s // SEL_BLOCK <= LANES and s // CMP_STRIDE == LANES
    dn, bc = _bias_tiles(rel_bias_table, s)
    ovt, et, gsel = _selection_constants(s)
    x2 = x.reshape(n, d)
    pending = None

    def project(i, weights, out_dtypes, weights_t):
        outs = _norm_proj(x2, norm_pre[i], s, weights, out_dtypes, weights_t, pending)
        return (outs[:-1], outs[-1]) if pending is not None else (outs, x2)

    for i in range(DEPTH):
        j = i // 2
        if i % 2 == 0:
            w_main, w_cmp, w_keys_t, col = _nsa_weight_layout(nsa_w_in[j])
            (p_main, p_cmp, keys_t), x2 = project(i, [w_main, w_cmp], [BF16, F32], [w_keys_t])
            kvc = _compress(p_cmp.reshape(b, s, -1),
                            *_compress_weight_layout(nsa_cmp_pe_k[j], nsa_cmp_w1_k[j], nsa_cmp_w2_k[j],
                                                     nsa_cmp_pe_v[j], nsa_cmp_w1_v[j], nsa_cmp_w2_v[j]))
            o = _nsa_attention(p_main.reshape(b, s, -1), keys_t, kvc, dn, bc, ovt, et, gsel, col)
            w_out = nsa_w_out[j]
        else:
            lambda_init = 0.8 - 0.6 * math.exp(-0.3 * i)
            w = diff_w_in[j]
            w_main = jnp.concatenate([w[:, :d] * (HEAD_DIM ** -0.5 * LOG2E), w[:, 2 * d:]],
                                     axis=1).astype(BF16)
            (p_main, keys_t), x2 = project(i, [w_main], [BF16], [w[:, d:2 * d]])
            o = _diff_attention(p_main.reshape(b, s, -1), keys_t, dn, diff_lambda_q1[j],
                                diff_lambda_k1[j], diff_lambda_q2[j], diff_lambda_k2[j],
                                diff_subln[j], lambda_init)
            w_out = diff_w_out[j]
        pending = (o.reshape(n, -1), w_out.astype(BF16), norm_post[i])
    return _out_proj(pending[0], pending[1], x2, pending[2]).reshape(b, s, d)
```

```python
import functools
import math

import jax
import jax.numpy as jnp
import numpy as np
from jax import lax
from jax.experimental import pallas as pl
from jax.experimental.pallas import tpu as pltpu

F32 = jnp.float32
BF16 = jnp.bfloat16

D_MODEL = 1024
DEPTH = 2
REL_BUCKETS = 32
REL_MAX_DIST = 128
NSA_HEADS = 16
NSA_GROUPS = 4
NSA_HPG = NSA_HEADS // NSA_GROUPS
HEAD_DIM = 64
CMP_BLOCK = 32
CMP_STRIDE = 16
CMP_HIDDEN = 128
SEL_BLOCK = 64
SEL_TOPK = 8
WINDOW = 512
DIFF_HEADS = 8
DIFF_VDIM = 128
NEG = -1e30
BIG = 1e9
EPS = 1e-6

LANES = 128
TQ = 128
TK = 2 * LANES
N_BIAS_KINDS = 5
VMEM_LIMIT = 52 * 1024 * 1024
LOG2E = math.log2(math.e)


def _sigmoid(x):
    return 1.0 / (1.0 + jnp.exp(-x))


def _nt_dot(a, b, precision=None):
    return lax.dot_general(a, b, (((1,), (1,)), ((), ())), precision=precision,
                           preferred_element_type=F32)


def _resident(shape, index_map):
    return pl.BlockSpec(shape, index_map, pipeline_mode=pl.Buffered(1))


def _residual_update(o_ref, w_ref, x_ref, g_ref):
    y = jnp.dot(o_ref[...], w_ref[...], preferred_element_type=F32)
    return x_ref[...] + y * lax.rsqrt(jnp.mean(y * y, axis=-1, keepdims=True) + EPS) * g_ref[...]


def _norm_proj_kernel(*refs, n_w, n_wt, fused):
    refs, scr_refs = refs[:len(refs) - n_wt], refs[len(refs) - n_wt:]
    if fused:
        x = _residual_update(refs[0], refs[1], refs[2], refs[3])
        refs[-1][...] = x
        refs = refs[4:-1]
    else:
        x = refs[0][...]
        refs = refs[1:]
    g_ref, refs = refs[0], refs[1:]
    w_refs, wt_refs, o_refs = refs[:n_w], refs[n_w:n_w + n_wt], refs[n_w + n_wt:]

    @pl.when(pl.program_id(0) == 0)
    def _():
        for wt_ref, scr_ref in zip(wt_refs, scr_refs):
            scr_ref[...] = wt_ref[...].T.astype(BF16)

    u = x * lax.rsqrt(jnp.mean(x * x, axis=-1, keepdims=True) + EPS) * g_ref[...]
    ub = u.astype(BF16)
    for w_ref, o_ref in zip(w_refs, o_refs[:n_w]):
        o_ref[...] = jnp.dot(ub, w_ref[...], preferred_element_type=F32).astype(o_ref.dtype)
    for scr_ref, o_ref in zip(scr_refs, o_refs[n_w:]):
        o_ref[0] = _nt_dot(scr_ref[...], ub).astype(o_ref.dtype)


def _norm_proj(x2, g, seq, weights, out_dtypes, weights_t=(), pending=None, tm=512):
    n, d = x2.shape
    n_st = seq // tm
    row_tile = lambda c: pl.BlockSpec((tm, c), lambda i: (i, 0))
    vector = pl.BlockSpec((1, d), lambda i: (0, 0))
    args, in_specs = [x2, g.reshape(1, d)], [row_tile(d), vector]
    if pending is not None:
        o2, w_out, g_post = pending
        args = [o2, w_out, x2, g_post.reshape(1, d), g.reshape(1, d)]
        in_specs = [row_tile(o2.shape[1]), _resident(w_out.shape, lambda i: (0, 0)), row_tile(d),
                    vector, vector]
    out_specs, out_shape = [], []
    for w, dt in zip(weights, out_dtypes):
        c = w.shape[1]
        in_specs.append(_resident((d, c), lambda i: (0, 0)))
        out_specs.append(pl.BlockSpec((tm, c), lambda i: (i, 0)))
        out_shape.append(jax.ShapeDtypeStruct((n, c), dt))
    for wt in weights_t:
        c = wt.shape[1]
        in_specs.append(_resident((d, c), lambda i: (0, 0)))
        out_specs.append(pl.BlockSpec((1, c, tm), lambda i: (i // n_st, 0, i % n_st)))
        out_shape.append(jax.ShapeDtypeStruct((n // seq, c, seq), BF16))
    if pending is not None:
        out_specs.append(row_tile(d))
        out_shape.append(jax.ShapeDtypeStruct((n, d), F32))
    return pl.pallas_call(
        functools.partial(_norm_proj_kernel, n_w=len(weights), n_wt=len(weights_t),
                          fused=pending is not None),
        grid=(n // tm,),
        in_specs=in_specs,
        out_specs=out_specs,
        out_shape=out_shape,
        scratch_shapes=[pltpu.VMEM((wt.shape[1], d), BF16) for wt in weights_t],
        compiler_params=pltpu.CompilerParams(
            dimension_semantics=("arbitrary",), vmem_limit_bytes=VMEM_LIMIT),
        name="norm_proj",
    )(*args, *weights, *weights_t)


def _out_proj_kernel(o_ref, w_ref, x_ref, g_ref, y_ref):
    y_ref[...] = _residual_update(o_ref, w_ref, x_ref, g_ref)


def _out_proj(o2, w, x2, g, tm=1024):
    n, d = x2.shape
    return pl.pallas_call(
        _out_proj_kernel,
        grid=(n // tm,),
        in_specs=[pl.BlockSpec((tm, o2.shape[1]), lambda i: (i, 0)),
                  _resident(w.shape, lambda i: (0, 0)),
                  pl.BlockSpec((tm, d), lambda i: (i, 0)),
                  pl.BlockSpec((1, d), lambda i: (0, 0))],
        out_specs=pl.BlockSpec((tm, d), lambda i: (i, 0)),
        out_shape=jax.ShapeDtypeStruct((n, d), F32),
        compiler_params=pltpu.CompilerParams(
            dimension_semantics=("arbitrary",), vmem_limit_bytes=VMEM_LIMIT),
        name="out_proj",
    )(o2, w, x2, g.reshape(1, d))


def _bucket_starts(max_dist):
    max_exact = REL_BUCKETS // 2
    n = np.arange(max_dist)
    nf = np.maximum(n, 1).astype(np.float32)
    large = max_exact + (np.log(nf / max_exact) / np.float32(math.log(REL_MAX_DIST / max_exact))
                         * (REL_BUCKETS - max_exact)).astype(np.int32)
    bucket = np.where(n < max_exact, n, np.minimum(large, REL_BUCKETS - 1))
    assert (np.diff(bucket) >= 0).all()
    return [int(np.argmax(bucket >= b)) if (bucket >= b).any() else None
            for b in range(REL_BUCKETS)]


def _bias_kernel(tbl_ref, dn_ref, bc_ref):
    h = pl.program_id(0)
    c31 = tbl_ref[REL_BUCKETS - 1, h]
    starts = _bucket_starts(bc_ref.shape[1])

    def rel_bias(dist):
        n = jnp.maximum(dist, 0)
        out = jnp.full(dist.shape, tbl_ref[0, h], F32)
        for b in range(1, REL_BUCKETS):
            if starts[b] is not None:
                out = jnp.where(n >= starts[b], tbl_ref[b, h], out)
        return (out - c31) * LOG2E

    i = lax.broadcasted_iota(jnp.int32, (TQ, LANES), 0)
    j = lax.broadcasted_iota(jnp.int32, (TQ, LANES), 1)
    dn_ref[0, 0] = jnp.where(j <= i, rel_bias(i - j), NEG)
    dn_ref[0, 1] = rel_bias(LANES + i - j)
    dn_ref[0, 2] = jnp.zeros((TQ, LANES), F32)
    dn_ref[0, 3] = jnp.where(j > i, 0.0, NEG)
    dn_ref[0, 4] = jnp.full((TQ, LANES), NEG, F32)

    n_cmp = (bc_ref.shape[1] - CMP_BLOCK) // CMP_STRIDE + 1

    def body(r, carry):
        t = r * TQ + i
        d = t - (j * CMP_STRIDE + CMP_BLOCK - 1)
        valid = (d >= 0) & (j < n_cmp)
        bc_ref[0, pl.ds(pl.multiple_of(r * TQ, TQ), TQ), :] = jnp.where(valid, rel_bias(d), NEG)
        return carry

    lax.fori_loop(0, bc_ref.shape[1] // TQ, body, 0)


def _bias_tiles(table, seq):
    n_maps = table.shape[1]
    return pl.pallas_call(
        _bias_kernel,
        grid=(n_maps,),
        in_specs=[pl.BlockSpec(memory_space=pltpu.SMEM)],
        out_specs=[pl.BlockSpec((1, N_BIAS_KINDS, TQ, LANES), lambda h: (h, 0, 0, 0)),
                   pl.BlockSpec((1, seq, LANES), lambda h: (h, 0, 0))],
        out_shape=[jax.ShapeDtypeStruct((n_maps, N_BIAS_KINDS, TQ, LANES), F32),
                   jax.ShapeDtypeStruct((n_maps, seq, LANES), F32)],
        compiler_params=pltpu.CompilerParams(dimension_semantics=("arbitrary",)),
        name="bias_tiles",
    )(table)


def _compress_kernel(x_ref, pe_ref, w1_ref, w2_ref, o_ref, xp_ref):
    seq = x_ref.shape[1]
    n_chunks = seq // CMP_STRIDE
    xp_ref[0:seq, :] = x_ref[0]
    xp_ref[seq:, :] = jnp.zeros((xp_ref.shape[0] - seq, xp_ref.shape[1]), F32)
    hid = jnp.zeros((n_chunks, 2 * CMP_HIDDEN), F32)
    for l in range(CMP_BLOCK):
        a = xp_ref[pl.ds(l, n_chunks, stride=CMP_STRIDE), :]
        hid = hid + jnp.dot((a + pe_ref[l:l + 1, :]).astype(BF16), w1_ref[l],
                            preferred_element_type=F32)
    hid = hid * _sigmoid(hid)
    o_ref[0, 0] = jnp.dot(hid.astype(BF16), w2_ref[...], preferred_element_type=F32).astype(BF16)


def _compress(cmp_in, pe_kv, w1_kv, w2_kv):
    b, s, _ = cmp_in.shape
    n_chunks = s // CMP_STRIDE
    return pl.pallas_call(
        _compress_kernel,
        grid=(b, NSA_GROUPS),
        in_specs=[pl.BlockSpec((1, s, LANES), lambda bi, g: (bi, 0, g)),
                  pl.BlockSpec(pe_kv.shape, lambda bi, g: (0, 0)),
                  pl.BlockSpec(w1_kv.shape, lambda bi, g: (0, 0, 0)),
                  pl.BlockSpec(w2_kv.shape, lambda bi, g: (0, 0))],
        out_specs=pl.BlockSpec((1, 1, n_chunks, 2 * LANES), lambda bi, g: (bi, g, 0, 0)),
        out_shape=jax.ShapeDtypeStruct((b, NSA_GROUPS, n_chunks, 2 * LANES), BF16),
        scratch_shapes=[pltpu.VMEM((s + CMP_BLOCK, LANES), F32)],
        compiler_params=pltpu.CompilerParams(
            dimension_semantics=("arbitrary", "arbitrary"), vmem_limit_bytes=VMEM_LIMIT),
        name="nsa_compress",
    )(cmp_in, pe_kv, w1_kv, w2_kv)


def _tile_kind(delta, window=False):
    if delta < 0 or (window and delta > WINDOW // LANES):
        return 4
    if window and delta == WINDOW // LANES:
        return 3
    return min(delta, 2)


def _tile_kinds(row_blocks, steps, window=False):
    n_hf = TK // LANES
    return [[[(slot, _tile_kind(q_tile - (kp * n_hf + hf), window)) for hf in range(n_hf)]
             for slot, q_tile in row_blocks] for kp in steps]


def _score_steps(qx, kt_tiles, kinds, bias_ref, st):
    n_rb = qx.shape[0] // TQ
    n_hf = TK // LANES
    st["mx"] = [None] * n_rb
    st["sc"] = []
    for i, load_kt in enumerate(kt_tiles):
        s = jnp.dot(qx, load_kt(), preferred_element_type=F32)
        step = []
        for rb in range(n_rb):
            tiles = []
            for hf in range(n_hf):
                slot, kind = kinds[i][rb][hf]
                if kind == 4:
                    tiles.append(None)
                    continue
                t = s[rb * TQ:(rb + 1) * TQ, hf * LANES:(hf + 1) * LANES]
                if kind != 2:
                    t = t + bias_ref[slot, kind]
                tiles.append(t)
                st["mx"][rb] = t if st["mx"][rb] is None else jnp.maximum(st["mx"][rb], t)
            step.append(tiles)
        st["sc"].append(step)
        yield


def _weight_steps(v_tiles, st):
    n_rb = len(st["mx"])
    mb = [jnp.broadcast_to(jnp.max(v, axis=-1, keepdims=True), (TQ, LANES)) for v in st["mx"]]
    st["acc"] = jnp.zeros((n_rb * TQ, 2 * LANES), F32)
    for i, load_v in enumerate(v_tiles):
        p = jnp.concatenate(
            [jnp.concatenate([jnp.zeros((TQ, LANES), F32) if t is None else jnp.exp2(t - mb[rb])
                              for t in st["sc"][i][rb]], axis=1) for rb in range(n_rb)], axis=0)
        st["acc"] = st["acc"] + jnp.dot(p.astype(BF16), load_v(), preferred_element_type=F32)
        yield


def _pipeline_tiles(order, n_steps, setup, score_steps, weight_steps, finish):
    n = len(order)
    state = {}
    qk_at = [0, 0]

    def qk_stream():
        state[order[0]] = setup(order[0])
        for i, x in enumerate(order):
            if i + 1 < n:
                state[order[i + 1]] = setup(order[i + 1])
            qk_at[:] = [i, 0]
            for _ in score_steps(x, state[x]):
                qk_at[1] += 1
                yield
        qk_at[:] = [n, 0]

    def pv_stream():
        for i, x in enumerate(order):
            while not (qk_at[0] > i + 1 or qk_at[0] >= n
                       or (qk_at[0] == i + 1 and 2 * qk_at[1] >= n_steps(order[i + 1]))):
                yield
            yield from weight_steps(x, state[x])
            finish(x, state.pop(x))

    live = [qk_stream(), pv_stream()]
    while live:
        for g in list(live):
            try:
                next(g)
            except StopIteration:
                live.remove(g)


def _tile_order(n):
    return list(range(0, n, 2)) + list(range(n - 1 - n % 2, 0, -2))


def _with_ones(v_tile):
    return jnp.concatenate([v_tile, jnp.ones(v_tile.shape, v_tile.dtype)], axis=1)


def _nsa_kernel(q_ref, kst_ref, kwt_ref, vs_ref, vw_ref, kvc_ref, gate_ref, z_ref, dn_ref, bc_ref,
                ovt_ref, et_ref, gsel_ref, o_ref, *, tq):
    seq = q_ref.shape[1]
    n_rb = tq // TQ
    n_sel = seq // SEL_BLOCK

    keys = lambda kp: slice(kp * TK, (kp + 1) * TK)

    n_pairs = NSA_HPG // 2
    tiles = {}

    def tile_setup(x):
        lane = lax.broadcasted_iota(jnp.int32, (tq, LANES), 1)
        lo_half = lane < HEAD_DIM
        blk = lax.broadcasted_iota(jnp.int32, (n_sel, tq), 0)
        tok = lax.broadcasted_iota(jnp.int32, (n_sel, tq), 1)
        row = lax.broadcasted_iota(jnp.int32, (tq, 1), 0)
        kcc = kvc_ref[0, 0, :, 0:LANES]
        vcc = kvc_ref[0, 0, :, LANES:2 * LANES]
        rows = slice(x * tq, (x + 1) * tq)
        q = q_ref[0, rows, :]
        qx = []
        for j in range(NSA_HPG):
            slab = q[:, (j // 2) * LANES:(j // 2 + 1) * LANES]
            keep = lo_half if j % 2 == 0 else jnp.logical_not(lo_half)
            qx.append(jnp.where(keep, slab, jnp.zeros_like(slab)))

        any_valid = (x * tq + row >= CMP_BLOCK - 1).astype(F32)
        p_sum = jnp.zeros((tq, LANES), F32)
        o_c = []
        for j in range(NSA_HPG):
            sj = _nt_dot(qx[j], kcc) + bc_ref[j, rows, :]
            pj = jnp.exp2(sj - jnp.max(sj, axis=-1, keepdims=True))
            pv = jnp.dot(pj.astype(BF16), _with_ones(vcc), preferred_element_type=F32)
            inv = any_valid / pv[:, LANES:]
            p_sum = p_sum + pj * inv
            o_c.append(pv[:, :LANES] * inv)

        imp = _nt_dot(ovt_ref[...], p_sum, precision=lax.Precision.HIGHEST)[0:n_sel, :]
        cur = jnp.right_shift(x * tq + tok, int(math.log2(SEL_BLOCK)))
        forced = (blk == 0) | (blk == cur) | (blk == cur - 1)
        val = jnp.where(forced, BIG, jnp.where(blk > cur, -BIG, imp))
        rank = jnp.zeros((n_sel, tq), F32)
        for other in range(min(n_sel, (x * tq + tq - 1) // SEL_BLOCK + 1)):
            r = val[other:other + 1, :]
            ahead = (r > val) | ((r == val) & (blk > other))
            rank = rank + jnp.where(ahead, 1.0, 0.0)
        block_mask_t = jnp.where(rank < float(min(SEL_TOPK, n_sel)), 0.0, NEG)
        block_mask = jnp.concatenate(
            [block_mask_t, jnp.zeros((LANES - n_sel, tq), F32)], axis=0).T.astype(BF16)

        return dict(qx=qx, block_mask=block_mask, o_c=o_c)

    order = _tile_order(seq // tq)
    started = []

    pairs = {}

    def setup(unit):
        x, pr = unit
        if not started:
            started.append(True)
            for y in order:
                tiles[y] = tile_setup(y)
        if unit not in pairs:
            t = tiles[x] if pr + 1 < n_pairs else tiles.pop(x)
            q_pair = jnp.concatenate(t["qx"][2 * pr:2 * pr + 2], axis=0)
            q_sel = jnp.concatenate([q_pair, jnp.concatenate([t["block_mask"]] * 2, axis=0)], axis=1)
            pairs[unit] = dict(q_win=q_pair, q_sel=q_sel, o_c=t["o_c"][2 * pr:2 * pr + 2])
        return {}

    def steps_of(unit):
        x = unit[0]
        sel_steps = range((x * tq + tq - 1) // TK + 1)
        win_steps = range(max(0, (x * tq - (WINDOW - 1)) // TK), sel_steps[-1] + 1)
        return win_steps, sel_steps

    both = lambda t: jnp.concatenate([t, t], axis=0)
    row_blocks = lambda unit: [(j, unit[0] * n_rb + r) for j in (2 * unit[1], 2 * unit[1] + 1)
                               for r in range(n_rb)]

    def win_scores(unit, st):
        steps = steps_of(unit)[0]
        return _score_steps(pairs[unit]["q_win"],
                            [(lambda kp=kp: both(kwt_ref[0, :, keys(kp)])) for kp in steps],
                            _tile_kinds(row_blocks(unit), steps, window=True), dn_ref, st)

    def sel_scores(unit, st):
        steps = steps_of(unit)[1]
        return _score_steps(pairs[unit]["q_sel"],
                            [(lambda kp=kp: jnp.concatenate(
                                [both(kst_ref[0, :, keys(kp)]), et_ref[:, keys(kp)]], axis=0))
                             for kp in steps],
                            _tile_kinds(row_blocks(unit), steps), dn_ref, st)

    def win_weights(unit, st):
        return _weight_steps([(lambda kp=kp: _with_ones(vw_ref[0, keys(kp), :]))
                              for kp in steps_of(unit)[0]], st)

    def sel_weights(unit, st):
        return _weight_steps([(lambda kp=kp: _with_ones(vs_ref[0, keys(kp), :]))
                              for kp in steps_of(unit)[1]], st)

    def keep_window(unit, st):
        pairs[unit]["win_acc"] = st["acc"]

    def finish(unit, st):
        x, pr = unit
        pair = pairs.pop(unit)
        rows = slice(x * tq, (x + 1) * tq)
        cols = slice(pr * LANES, (pr + 1) * LANES)
        lo_half = lax.broadcasted_iota(jnp.int32, (tq, LANES), 1) < HEAD_DIM
        gates = _sigmoid(jnp.dot(gate_ref[0, rows, :], gsel_ref[0, :, 3 * pr * LANES:3 * (pr + 1) * LANES],
                                 preferred_element_type=F32))
        merged = lambda a: jnp.where(lo_half, a[:tq, :], a[tq:, :])
        ratio = lambda acc: merged(acc[:, :LANES] / acc[:, LANES:])
        o = (gates[:, 0:LANES] * jnp.where(lo_half, pair["o_c"][0], pair["o_c"][1])
             + gates[:, LANES:2 * LANES] * ratio(st["acc"])
             + gates[:, 2 * LANES:3 * LANES] * ratio(pair["win_acc"]))
        z = z_ref[0, rows, cols].astype(F32)
        o_ref[0, rows, cols] = (o * (z * _sigmoid(z))).astype(o_ref.dtype)

    units = [(x, pr) for x in order for pr in range(n_pairs)]
    _pipeline_tiles(units, lambda u: len(steps_of(u)[0]), setup, win_scores, win_weights,
                    keep_window)
    _pipeline_tiles(units, lambda u: len(steps_of(u)[1]), setup, sel_scores, sel_weights, finish)


def _nsa_attention(p3, kt3, kvc, dn, bc, ovt, et, gsel, col, tq=2 * TQ):
    b, s, _ = p3.shape
    wide = 2 * LANES

    def slab(first_lane_block, per_group, width=LANES):
        return pl.BlockSpec((1, s, width), lambda g, bi: (bi, 0, first_lane_block + per_group * g))

    in_specs = [
        slab(col["q"] // wide, 1, wide),
        pl.BlockSpec((1, HEAD_DIM, s), lambda g, bi: (bi, g, 0)),
        pl.BlockSpec((1, HEAD_DIM, s), lambda g, bi: (bi, NSA_GROUPS + g, 0)),
        slab(col["vsel"] // LANES, 1), slab(col["vwin"] // LANES, 1),
        pl.BlockSpec((1, 1, kvc.shape[2], wide), lambda g, bi: (bi, g, 0, 0)),
        slab(col["gate"] // LANES, 0),
        slab(col["z"] // wide, 1, wide),
        _resident((NSA_HPG, N_BIAS_KINDS, TQ, LANES), lambda g, bi: (g, 0, 0, 0)),
        _resident((NSA_HPG, s, LANES), lambda g, bi: (g, 0, 0)),
        _resident(ovt.shape, lambda g, bi: (0, 0)),
        _resident(et.shape, lambda g, bi: (0, 0)),
        _resident((1,) + gsel.shape[1:], lambda g, bi: (g, 0, 0)),
    ]
    return pl.pallas_call(
        functools.partial(_nsa_kernel, tq=tq),
        grid=(NSA_GROUPS, b),
        in_specs=in_specs,
        out_specs=pl.BlockSpec((1, s, wide), lambda g, bi: (bi, 0, g)),
        out_shape=jax.ShapeDtypeStruct((b, s, NSA_GROUPS * wide), BF16),
        compiler_params=pltpu.CompilerParams(
            dimension_semantics=("arbitrary", "arbitrary"), vmem_limit_bytes=VMEM_LIMIT),
        name="nsa_attention",
    )(p3, kt3, kt3, p3, p3, kvc, p3, p3, dn, bc, ovt, et, gsel)


def _diff_kernel(q_ref, kt_ref, v_ref, z_ref, dn_ref, lq1_ref, lk1_ref, lq2_ref, lk2_ref, sub_ref,
                 o_ref, *, lambda_init, tq):
    seq = q_ref.shape[1]
    n_rb = tq // TQ
    lane = lax.broadcasted_iota(jnp.int32, (tq, LANES), 1)
    lam = (jnp.exp(jnp.sum(lq1_ref[...] * lk1_ref[...], axis=-1, keepdims=True))
           - jnp.exp(jnp.sum(lq2_ref[...] * lk2_ref[...], axis=-1, keepdims=True)) + lambda_init)
    keys = lambda kp: slice(kp * TK, (kp + 1) * TK)
    steps_of = lambda x: range((x * tq + tq - 1) // TK + 1)

    def setup(x):
        q = q_ref[0, x * tq:(x + 1) * tq, :]
        zero = jnp.zeros_like(q)
        return dict(q=jnp.concatenate([jnp.where(lane < HEAD_DIM, q, zero),
                                       jnp.where(lane >= HEAD_DIM, q, zero)], axis=0))

    def score_steps(x, st):
        row_blocks = [(m, x * n_rb + r) for m in range(2) for r in range(n_rb)]
        return _score_steps(st["q"], [(lambda kp=kp: kt_ref[0, :, keys(kp)]) for kp in steps_of(x)],
                            _tile_kinds(row_blocks, steps_of(x)), dn_ref, st)

    def weight_steps(x, st):
        return _weight_steps([(lambda kp=kp: _with_ones(v_ref[0, keys(kp), :]))
                              for kp in steps_of(x)], st)

    def finish(x, st):
        rows = slice(x * tq, (x + 1) * tq)
        a = st["acc"][:, :LANES] / st["acc"][:, LANES:]
        o = a[:tq, :] - lam * a[tq:, :]
        o = o * lax.rsqrt(jnp.mean(o * o, axis=-1, keepdims=True) + EPS) * sub_ref[...]
        o = o * (1.0 - lambda_init)
        z = z_ref[0, rows, :].astype(F32)
        o_ref[0, rows, :] = (o * (z * _sigmoid(z))).astype(o_ref.dtype)

    _pipeline_tiles(_tile_order(seq // tq), lambda x: len(steps_of(x)),
                    setup, score_steps, weight_steps, finish)


def _diff_attention(p3, kt3, dn, lq1, lk1, lq2, lk2, subln, lambda_init, tq=2 * TQ):
    b, s, _ = p3.shape
    h = DIFF_HEADS
    vec = lambda a: a.reshape(1, -1)
    small = pl.BlockSpec((1, HEAD_DIM), lambda bi, hi: (0, 0))
    slab = lambda first: pl.BlockSpec((1, s, LANES), lambda bi, hi: (bi, 0, first + hi))
    in_specs = [
        slab(0),
        pl.BlockSpec((1, LANES, s), lambda bi, hi: (bi, hi, 0)),
        slab(h), slab(2 * h),
        pl.BlockSpec((2, N_BIAS_KINDS, TQ, LANES), lambda bi, hi: (hi, 0, 0, 0)),
        small, small, small, small,
        pl.BlockSpec((1, DIFF_VDIM), lambda bi, hi: (0, hi)),
    ]
    return pl.pallas_call(
        functools.partial(_diff_kernel, lambda_init=lambda_init, tq=tq),
        grid=(b, h),
        in_specs=in_specs,
        out_specs=slab(0),
        out_shape=jax.ShapeDtypeStruct((b, s, h * DIFF_VDIM), BF16),
        compiler_params=pltpu.CompilerParams(
            dimension_semantics=("arbitrary", "arbitrary"), vmem_limit_bytes=VMEM_LIMIT),
        name="diff_attention",
    )(p3, kt3, p3, p3, dn, vec(lq1), vec(lk1), vec(lq2), vec(lk2), vec(subln))


def _nsa_weight_layout(w_in):
    d = HEAD_DIM
    kv = NSA_GROUPS * d
    width = NSA_HEADS * d
    pts = np.cumsum([width] + [kv] * 6 + [3 * NSA_HEADS, width])
    q, kc, vc, ks, vs, kw, vw, gate, z = [w_in[:, a:b] for a, b in zip([0] + list(pts[:-1]), pts)]
    grp = lambda w, g: w[:, g * d:(g + 1) * d]
    dup = lambda w: jnp.concatenate(
        [jnp.concatenate([grp(w, g), grp(w, g)], axis=1) for g in range(NSA_GROUPS)], axis=1)
    gate_pad = jnp.pad(gate, ((0, 0), (0, LANES - gate.shape[1])))
    parts = [("q", q * (d ** -0.5 * LOG2E)), ("z", z), ("vsel", dup(vs)), ("vwin", dup(vw)),
             ("gate", gate_pad)]
    col, off = {}, 0
    for name, w in parts:
        col[name] = off
        off += w.shape[1]
    w_main = jnp.concatenate([w for _, w in parts], axis=1).astype(BF16)
    w_keys_t = jnp.concatenate([ks, kw], axis=1)
    w_cmp = jnp.concatenate(
        [jnp.concatenate([grp(kc, g), grp(vc, g)], axis=1) for g in range(NSA_GROUPS)],
        axis=1).astype(BF16)
    return w_main, w_cmp, w_keys_t, col


def _compress_weight_layout(pe_k, w1_k, w2_k, pe_v, w1_v, w2_v):
    d, hdn = HEAD_DIM, CMP_HIDDEN
    pe_kv = jnp.concatenate([pe_k, pe_v], axis=1)
    w1k = w1_k.reshape(CMP_BLOCK, d, hdn)
    w1v = w1_v.reshape(CMP_BLOCK, d, hdn)
    zero = jnp.zeros_like(w1k)
    w1_kv = jnp.concatenate([jnp.concatenate([w1k, zero], axis=2),
                             jnp.concatenate([zero, w1v], axis=2)], axis=1).astype(BF16)
    z2 = jnp.zeros((hdn, 2 * d), w2_k.dtype)
    w2_kv = jnp.concatenate([jnp.concatenate([w2_k, w2_k, z2], axis=1),
                             jnp.concatenate([z2, w2_v, w2_v], axis=1)], axis=0).astype(BF16)
    return pe_kv, w1_kv, w2_kv


def _selection_constants(seq):
    n_cmp_pad = seq // CMP_STRIDE
    n_cmp = (seq - CMP_BLOCK) // CMP_STRIDE + 1
    n_sel = seq // SEL_BLOCK
    cmp_lo = np.arange(n_cmp_pad) * CMP_STRIDE
    sel_lo = np.arange(n_sel) * SEL_BLOCK
    overlap = np.maximum(np.minimum(cmp_lo[:, None] + CMP_BLOCK, sel_lo[None, :] + SEL_BLOCK)
                         - np.maximum(cmp_lo[:, None], sel_lo[None, :]), 0).astype(np.float32) / CMP_BLOCK
    overlap[n_cmp:] = 0.0
    ovt = np.zeros((LANES, n_cmp_pad), np.float32)
    ovt[:n_sel] = overlap.T
    et = np.zeros((LANES, seq), np.float32)
    et[np.arange(seq) // SEL_BLOCK, np.arange(seq)] = 1.0
    n_pairs = NSA_HPG // 2
    gsel = np.zeros((NSA_GROUPS, LANES, 3 * n_pairs * LANES), np.float32)
    for g in range(NSA_GROUPS):
        for pr in range(n_pairs):
            for i in range(3):
                for n in range(LANES):
                    head = g * NSA_HPG + 2 * pr + n // HEAD_DIM
                    gsel[g, 3 * head + i, (pr * 3 + i) * LANES + n] = 1.0
    return jnp.asarray(ovt), jnp.asarray(et, dtype=BF16), jnp.asarray(gsel, dtype=BF16)


def kernel(x, rel_bias_table, norm_pre, norm_post, nsa_w_in, nsa_cmp_pe_k, nsa_cmp_w1_k, nsa_cmp_w2_k,
           nsa_cmp_pe_v, nsa_cmp_w1_v, nsa_cmp_w2_v, nsa_w_out, diff_w_in, diff_lambda_q1,
           diff_lambda_k1, diff_lambda_q2, diff_lambda_k2, diff_subln, diff_w_out):
    b, s, d = x.shape
    n = b * s
    assert d == D_MODEL and s % TK == 0 and s // SEL_BLOCK <= LANES and s // CMP_STRIDE == LANES
    dn, bc = _bias_tiles(rel_bias_table, s)
    ovt, et, gsel = _selection_constants(s)
    x2 = x.reshape(n, d)
    pending = None

    def project(i, weights, out_dtypes, weights_t):
        outs = _norm_proj(x2, norm_pre[i], s, weights, out_dtypes, weights_t, pending)
        return (outs[:-1], outs[-1]) if pending is not None else (outs, x2)

    for i in range(DEPTH):
        j = i // 2
        if i % 2 == 0:
            w_main, w_cmp, w_keys_t, col = _nsa_weight_layout(nsa_w_in[j])
            (p_main, p_cmp, keys_t), x2 = project(i, [w_main, w_cmp], [BF16, F32], [w_keys_t])
            kvc = _compress(p_cmp.reshape(b, s, -1),
                            *_compress_weight_layout(nsa_cmp_pe_k[j], nsa_cmp_w1_k[j], nsa_cmp_w2_k[j],
                                                     nsa_cmp_pe_v[j], nsa_cmp_w1_v[j], nsa_cmp_w2_v[j]))
            o = _nsa_attention(p_main.reshape(b, s, -1), keys_t, kvc, dn, bc, ovt, et, gsel, col)
            w_out = nsa_w_out[j]
        else:
            lambda_init = 0.8 - 0.6 * math.exp(-0.3 * i)
            w = diff_w_in[j]
            w_main = jnp.concatenate([w[:, :d] * (HEAD_DIM ** -0.5 * LOG2E), w[:, 2 * d:]],
                                     axis=1).astype(BF16)
            (p_main, keys_t), x2 = project(i, [w_main], [BF16], [w[:, d:2 * d]])
            o = _diff_attention(p_main.reshape(b, s, -1), keys_t, dn, diff_lambda_q1[j],
                                diff_lambda_k1[j], diff_lambda_q2[j], diff_lambda_k2[j],
                                diff_subln[j], lambda_init)
            w_out = diff_w_out[j]
        pending = (o.reshape(n, -1), w_out.astype(BF16), norm_post[i])
    return _out_proj(pending[0], pending[1], x2, pending[2]).reshape(b, s, d)
```
